```python
import jax, jax.numpy as jnp
from jax import lax
import numpy as np

D_MODEL = 1024
BATCH = 8
SEQ = 4096
DEPTH = 1

RET_HEADS = 8
RET_DK = 64
RET_DV = 128
RET_CHUNK = 128
ATT_GROUPS = ((128, 1), (512, 4), (2048, 16))
ATT_HEADS_PER_GROUP = 4
ATT_HEADS = 12
ATT_DH = 128
ATT_BLOCK = 128
PEER_HEADS = 8
PEER_NKEYS = 128
PEER_EXPERTS = PEER_NKEYS * PEER_NKEYS
PEER_DKEY = 256
PEER_TOPK = 16
PEER_TOKEN_BLOCK = 128
EPS = 1e-6

RET_QK_W = RET_HEADS * RET_DK
RET_V_W = RET_HEADS * RET_DV
ATT_W = ATT_HEADS * ATT_DH
ATT_OUT_W = ATT_HEADS_PER_GROUP * ATT_DH
IN_SPLITS = (RET_QK_W, RET_QK_W, RET_V_W, RET_V_W, ATT_W, ATT_W, ATT_W, D_MODEL, D_MODEL)
IN_W = 2 * RET_QK_W + 2 * RET_V_W + 3 * ATT_W + 2 * D_MODEL

kernel_name = "hybrid_retention_dilated_attn_peer"


def rmsnorm(x, g):
    xf = x.astype(jnp.float32)
    y = xf * lax.rsqrt(jnp.mean(xf * xf, axis=-1, keepdims=True) + EPS)
    return (y * g.astype(jnp.float32)).astype(x.dtype)


def head_norm(x):
    xf = x.astype(jnp.float32)
    return (xf * lax.rsqrt(jnp.mean(xf * xf, axis=-1, keepdims=True) + EPS)).astype(x.dtype)


def split_points():
    pts, acc = [], 0
    for w in IN_SPLITS[:-1]:
        acc += w
        pts.append(acc)
    return pts


def retention(q, k, v):
    B, S, H, DK = q.shape
    DV = v.shape[-1]
    C = RET_CHUNK
    n = S // C
    dt = q.dtype
    log_g = jnp.log1p(-jnp.exp2(-5.0 - jnp.arange(H, dtype=jnp.float32)))
    pos = jnp.arange(C, dtype=jnp.float32)
    q = q.reshape(B, n, C, H, DK)
    k = (k * (DK ** -0.5)).reshape(B, n, C, H, DK)
    v = v.reshape(B, n, C, H, DV)
    diff = pos[:, None] - pos[None, :]
    decay = jnp.where(diff[None] >= 0, jnp.exp(jnp.maximum(diff, 0.0)[None] * log_g[:, None, None]), 0.0)
    qk = jnp.einsum('bnihd,bnjhd->bnhij', q, k) * decay.astype(dt)
    inner = jnp.einsum('bnhij,bnjhe->bnihe', qk, v)
    w_k = jnp.exp((C - 1 - pos)[:, None] * log_g[None]).astype(dt)
    kv = jnp.einsum('bnjhd,bnjhe->nbhde', k * w_k[:, :, None], v)
    chunk_decay = jnp.exp(C * log_g).astype(dt)[None, :, None, None]

    def step(state, kv_c):
        return chunk_decay * state + kv_c, state

    _, r_prev = lax.scan(step, jnp.zeros((B, H, DK, DV), dt), kv)
    w_q = jnp.exp((pos + 1)[:, None] * log_g[None]).astype(dt)
    cross = jnp.einsum('bnihd,nbhde->bnihe', q * w_q[:, :, None], r_prev)
    return (inner + cross).reshape(B, S, H, DV)


def alibi_slopes(n_heads):
    return jnp.exp2(-8.0 * jnp.arange(1, n_heads + 1, dtype=jnp.float32) / n_heads)


def dilated_group(q, k, v, window, dilation, slopes):
    B, S, H, Dh = q.shape
    dt = q.dtype
    L = S // dilation
    n = -(-L // ATT_BLOCK)
    Lp = n * ATT_BLOCK
    span = window // dilation

    def strided(t):
        t = t.reshape(B, L, dilation, H, Dh).transpose(0, 2, 1, 3, 4)
        t = jnp.pad(t, ((0, 0), (0, 0), (0, Lp - L), (0, 0), (0, 0)))
        return t.reshape(B, dilation, n, ATT_BLOCK, H, Dh)

    def with_prev(t):
        prev = jnp.concatenate([jnp.zeros_like(t[:, :, :1]), t[:, :, :-1]], axis=2)
        return jnp.concatenate([prev, t], axis=3)

    qb = strided(q)
    kk = with_prev(strided(k))
    vv = with_prev(strided(v))
    s = jnp.einsum('brnqhd,brnkhd->brnhqk', qb, kk).astype(jnp.float32) * (Dh ** -0.5)
    iq = jnp.arange(ATT_BLOCK)[:, None]
    jk = jnp.arange(2 * ATT_BLOCK)[None, :]
    dist = iq + ATT_BLOCK - jk
    blk = jnp.arange(n)[:, None, None]
    valid = (dist >= 0) & (dist <= span) & (blk * ATT_BLOCK + jk - ATT_BLOCK >= 0)
    bias = -(slopes.astype(jnp.float32)[:, None, None] * (dilation * dist).astype(jnp.float32))
    s = jnp.where(valid[:, None], s + bias, -jnp.inf)
    m = jnp.max(s, axis=-1)
    p = jnp.exp(s - m[..., None])
    den = jnp.sum(p, axis=-1)
    o = jnp.einsum('brnhqk,brnkhd->brnqhd', (p / den[..., None]).astype(dt), vv)

    def unstride(t):
        t = t.reshape(B, dilation, Lp, *t.shape[4:])[:, :, :L]
        t = jnp.moveaxis(t, 1, 2)
        return t.reshape(B, S, *t.shape[3:])

    return (unstride(o), unstride(jnp.moveaxis(m, 3, 4)), unstride(jnp.moveaxis(den, 3, 4)))


def dilated_attention(q, k, v):
    B, S = q.shape[:2]
    dt = q.dtype
    slopes = alibi_slopes(ATT_HEADS)
    outs, maxs, dens = [], [], []
    for g, (w, d) in enumerate(ATT_GROUPS):
        sl = slice(g * ATT_HEADS_PER_GROUP, (g + 1) * ATT_HEADS_PER_GROUP)
        o, m, den = dilated_group(q[:, :, sl], k[:, :, sl], v[:, :, sl], w, d, slopes[sl])
        outs.append(o)
        maxs.append(m)
        dens.append(den)
    m_all = jnp.stack(maxs)
    wts = jnp.stack(dens) * jnp.exp(m_all - jnp.max(m_all, axis=0))
    wts = wts / jnp.sum(wts, axis=0)
    o = jnp.einsum('gbsh,gbshd->bshd', wts.astype(dt), jnp.stack(outs))
    return o.reshape(B, S, ATT_OUT_W)


def peer(x, wq, subkeys, u_tab, v_tab):
    B, S, D = x.shape
    T = PEER_TOKEN_BLOCK
    K = PEER_TOPK
    xt = x.reshape(-1, T, D)

    def block(xb):
        q = (xb @ wq).reshape(T, PEER_HEADS, 2, PEER_DKEY // 2)
        sc = jnp.einsum('thpc,hpnc->thpn', q, subkeys).astype(jnp.float32)
        top_s, top_i = lax.top_k(sc, K)
        cand = (top_s[:, :, 0, :, None] + top_s[:, :, 1, None, :]).reshape(T, PEER_HEADS, K * K)
        best_s, best_c = lax.top_k(cand, K)
        ia = jnp.take_along_axis(top_i[:, :, 0], best_c // K, axis=-1)
        ib = jnp.take_along_axis(top_i[:, :, 1], best_c % K, axis=-1)
        expert = ia * PEER_NKEYS + ib
        gate = jax.nn.softmax(best_s, axis=-1).astype(xb.dtype)
        act = jax.nn.gelu(jnp.einsum('thkd,td->thk', u_tab[expert], xb), approximate=False)
        return jnp.einsum('thk,thkd->td', gate * act, v_tab[expert])

    return lax.map(block, xt).reshape(B, S, D)


def setup_inputs(seed: int = 0) -> dict:
    key = jax.random.key(seed)
    ks = jax.random.split(key, 13)
    f32 = jnp.float32
    nrm = lambda k, shape, scale: jax.random.normal(k, shape, f32) * scale
    return {
        "x": jax.random.normal(ks[0], (BATCH, SEQ, D_MODEL), f32),
        "norm1_g": 1.0 + nrm(ks[1], (DEPTH, D_MODEL), 0.02),
        "w_in": nrm(ks[2], (DEPTH, D_MODEL, IN_W), D_MODEL ** -0.5),
        "w_ret_out": nrm(ks[3], (DEPTH, RET_V_W, D_MODEL), RET_V_W ** -0.5),
        "w_att_out": nrm(ks[4], (DEPTH, ATT_OUT_W, D_MODEL), ATT_OUT_W ** -0.5),
        "w_out": nrm(ks[5], (DEPTH, D_MODEL, D_MODEL), D_MODEL ** -0.5),
        "norm2_g": 1.0 + nrm(ks[6], (DEPTH, D_MODEL), 0.02),
        "peer_wq": nrm(ks[7], (DEPTH, D_MODEL, PEER_HEADS * PEER_DKEY), D_MODEL ** -0.5),
        "peer_subkeys": nrm(ks[8], (DEPTH, PEER_HEADS, 2, PEER_NKEYS, PEER_DKEY // 2), (PEER_DKEY // 2) ** -0.5),
        "peer_u": nrm(ks[9], (DEPTH, PEER_EXPERTS, D_MODEL), D_MODEL ** -0.5),
        "peer_v": nrm(ks[10], (DEPTH, PEER_EXPERTS, D_MODEL), PEER_HEADS ** -0.5),
        "normf_g": 1.0 + nrm(ks[11], (D_MODEL,), 0.02),
    }


def reference(x, norm1_g, w_in, w_ret_out, w_att_out, w_out, norm2_g, peer_wq, peer_subkeys, peer_u, peer_v, normf_g):
    B, S, D = x.shape
    h = x
    for l in range(DEPTH):
        xn = rmsnorm(h, norm1_g[l])
        proj = xn @ w_in[l]
        rq, rk, rv, rg, aq, ak, av, gr, ga = jnp.split(proj, split_points(), axis=-1)
        ret = retention(rq.reshape(B, S, RET_HEADS, RET_DK), rk.reshape(B, S, RET_HEADS, RET_DK),
                        rv.reshape(B, S, RET_HEADS, RET_DV))
        ret = jax.nn.silu(rg) * head_norm(ret).reshape(B, S, RET_V_W)
        r_branch = ret @ w_ret_out[l]
        att = dilated_attention(aq.reshape(B, S, ATT_HEADS, ATT_DH), ak.reshape(B, S, ATT_HEADS, ATT_DH),
                                av.reshape(B, S, ATT_HEADS, ATT_DH))
        a_branch = att @ w_att_out[l]
        merged = jax.nn.sigmoid(gr) * r_branch + jax.nn.sigmoid(ga) * a_branch
        h = h + merged @ w_out[l]
        h = h + peer(rmsnorm(h, norm2_g[l]), peer_wq[l], peer_subkeys[l], peer_u[l], peer_v[l])
    return rmsnorm(h, normf_g)
```

```python
import functools
import math

import jax
import jax.numpy as jnp
from jax import lax
from jax.experimental import pallas as pl
from jax.experimental.pallas import tpu as pltpu

F32 = jnp.float32
BF16 = jnp.bfloat16

D_MODEL = 1024
RET_HEADS = 8
RET_DK = 64
RET_DV = 128
RET_CHUNK = 128
ATT_GROUPS = ((128, 1), (512, 4), (2048, 16))
ATT_HPG = 4
ATT_HEADS = 12
ATT_DH = 128
ATT_BLOCK = 128
PEER_HEADS = 8
PEER_NKEYS = 128
PEER_DKEY = 256
PEER_TOPK = 16
PEER_PICKS = PEER_HEADS * PEER_TOPK
EPS = 1e-6

RET_QK_W = RET_HEADS * RET_DK
RET_V_W = RET_HEADS * RET_DV
ATT_W = ATT_HEADS * ATT_DH
ATT_OUT_W = ATT_HPG * ATT_DH
OFF_RQ, OFF_RK, OFF_RV, OFF_RG = 0, 512, 1024, 2048
OFF_GR, OFF_GA = 3072, 4096
OFF_AQ, OFF_AK, OFF_AV = 5120, 6656, 8192
IN_W = 9728

VMEM_LIMIT_BYTES = 60 * 1024 * 1024

PROJ_TM = 256
PROJ_TN = 512
MIX_TM = 256
TOPK_TB = 256
PEER_TB = 128
FINAL_TM = 512


def _cparams(sem):
    return pltpu.CompilerParams(dimension_semantics=sem, vmem_limit_bytes=VMEM_LIMIT_BYTES)


def _const_spec(shape):
    nd = len(shape)
    return pl.BlockSpec(shape, lambda *_: (0,) * nd)


def _proj_kernel(x_ref, g_ref, w_ref, o_ref):
    x = x_ref[...]
    ms = jnp.mean(x * x, axis=-1, keepdims=True)
    xn = (x * lax.rsqrt(ms + EPS) * g_ref[...]).astype(BF16)
    for j in range(IN_W // PROJ_TN):
        sl = slice(j * PROJ_TN, (j + 1) * PROJ_TN)
        o_ref[:, sl] = jnp.dot(xn, w_ref[:, sl], preferred_element_type=F32).astype(BF16)


def _proj(x2, g, w):
    T = x2.shape[0]
    return pl.pallas_call(
        _proj_kernel,
        grid=(T // PROJ_TM,),
        in_specs=[
            pl.BlockSpec((PROJ_TM, D_MODEL), lambda i: (i, 0)),
            _const_spec((1, D_MODEL)),
            pl.BlockSpec((D_MODEL, IN_W), lambda i: (0, 0), pipeline_mode=pl.Buffered(1)),
        ],
        out_specs=pl.BlockSpec((PROJ_TM, IN_W), lambda i: (i, 0)),
        out_shape=jax.ShapeDtypeStruct((T, IN_W), BF16),
        compiler_params=_cparams(("parallel",)),
        name="proj",
    )(x2, g, w)


def _retention_kernel(q_ref, k_ref, v_ref, rg_ref, o_ref, state_ref):
    C = RET_CHUNK

    @pl.when(pl.program_id(1) == 0)
    def _():
        state_ref[...] = jnp.zeros_like(state_ref)

    pi = lax.broadcasted_iota(jnp.int32, (C, C), 0)
    pj = lax.broadcasted_iota(jnp.int32, (C, C), 1)
    diff = (pi - pj).astype(F32)
    causal = diff >= 0
    diffc = jnp.maximum(diff, 0.0)
    pos = lax.broadcasted_iota(jnp.int32, (C, 1), 0).astype(F32)
    q = q_ref[0]
    k = k_ref[0]
    v = v_ref[0]
    rg = rg_ref[0]
    for h in range(RET_HEADS):
        lg = math.log1p(-(2.0 ** (-5.0 - h)))
        decay = jnp.where(causal, jnp.exp(diffc * lg), 0.0)
        w_k = jnp.exp((C - 1 - pos) * lg)
        w_q = jnp.exp((pos + 1.0) * lg)
        qh = q[:, h * RET_DK:(h + 1) * RET_DK]
        kh = k[:, h * RET_DK:(h + 1) * RET_DK].astype(F32) * (RET_DK ** -0.5)
        vh = v[:, h * RET_DV:(h + 1) * RET_DV]
        s = lax.dot_general(qh, kh.astype(BF16), (((1,), (1,)), ((), ())), preferred_element_type=F32)
        p = (s * decay).astype(BF16)
        inner = jnp.dot(p, vh, preferred_element_type=F32)
        st = state_ref[h]
        cross = jnp.dot((qh.astype(F32) * w_q).astype(BF16), st.astype(BF16), preferred_element_type=F32)
        kw = (kh * w_k).astype(BF16)
        kv = lax.dot_general(kw, vh, (((0,), (0,)), ((), ())), preferred_element_type=F32)
        state_ref[h] = math.exp(C * lg) * st + kv
        ret = inner + cross
        rn = ret * lax.rsqrt(jnp.mean(ret * ret, axis=-1, keepdims=True) + EPS)
        g = rg[:, h * RET_DV:(h + 1) * RET_DV].astype(F32)
        o_ref[0, :, h * RET_DV:(h + 1) * RET_DV] = (g * jax.nn.sigmoid(g) * rn).astype(BF16)


def _retention(proj3):
    B, S, _ = proj3.shape
    n = S // RET_CHUNK
    return pl.pallas_call(
        _retention_kernel,
        grid=(B, n),
        in_specs=[
            pl.BlockSpec((1, RET_CHUNK, RET_QK_W), lambda b, c: (b, c, OFF_RQ // RET_QK_W)),
            pl.BlockSpec((1, RET_CHUNK, RET_QK_W), lambda b, c: (b, c, OFF_RK // RET_QK_W)),
            pl.BlockSpec((1, RET_CHUNK, RET_V_W), lambda b, c: (b, c, OFF_RV // RET_V_W)),
            pl.BlockSpec((1, RET_CHUNK, RET_V_W), lambda b, c: (b, c, OFF_RG // RET_V_W)),
        ],
        out_specs=pl.BlockSpec((1, RET_CHUNK, RET_V_W), lambda b, c: (b, c, 0)),
        out_shape=jax.ShapeDtypeStruct((B, S, RET_V_W), BF16),
        scratch_shapes=[pltpu.VMEM((RET_HEADS, RET_DK, RET_DV), F32)],
        compiler_params=_cparams(("parallel", "arbitrary")),
        name="retention",
    )(proj3, proj3, proj3, proj3)


def _att_kernel(q_ref, kp_ref, kc_ref, vp_ref, vc_ref, o_ref, lse_ref, *, group, dilation, span):
    n = pl.program_id(2)
    iq = lax.broadcasted_iota(jnp.int32, (ATT_BLOCK, 2 * ATT_BLOCK), 0)
    jk = lax.broadcasted_iota(jnp.int32, (ATT_BLOCK, 2 * ATT_BLOCK), 1)
    dist = iq + ATT_BLOCK - jk
    valid = (dist >= 0) & (dist <= span) & ((jk >= ATT_BLOCK) | (n > 0))
    distf = (dilation * dist).astype(F32)
    q = q_ref[0]
    kk = jnp.concatenate([kp_ref[0], kc_ref[0]], axis=0)
    vv = jnp.concatenate([vp_ref[0], vc_ref[0]], axis=0)
    for i in range(ATT_HPG):
        slope = 2.0 ** (-8.0 * (group * ATT_HPG + i + 1) / ATT_HEADS)
        sl = slice(i * ATT_DH, (i + 1) * ATT_DH)
        s = lax.dot_general(q[:, sl], kk[:, sl], (((1,), (1,)), ((), ())), preferred_element_type=F32)
        s = s * (ATT_DH ** -0.5)
        s = jnp.where(valid, s - slope * distf, -jnp.inf)
        m = jnp.max(s, axis=-1, keepdims=True)
        p = jnp.exp(s - m)
        den = jnp.sum(p, axis=-1, keepdims=True)
        o = jnp.dot(p.astype(BF16), vv[:, sl], preferred_element_type=F32) / den
        o_ref[0, :, sl] = o.astype(BF16)
        lse_ref[0, :, sl] = jnp.broadcast_to(m + jnp.log(den), (ATT_BLOCK, ATT_DH))


def _att_group(proj3, group):
    window, d = ATT_GROUPS[group]
    B, S, _ = proj3.shape
    L = S // d
    nL = L // ATT_BLOCK
    view = proj3.reshape(B, L, d * IN_W)
    cb = IN_W // ATT_OUT_W

    def cur(off):
        return pl.BlockSpec((1, ATT_BLOCK, ATT_OUT_W),
                            lambda b, r, n: (b, n, r * cb + off // ATT_OUT_W + group))

    def prev(off):
        return pl.BlockSpec((1, ATT_BLOCK, ATT_OUT_W),
                            lambda b, r, n: (b, jnp.maximum(n - 1, 0), r * cb + off // ATT_OUT_W + group))

    out_spec = pl.BlockSpec((1, ATT_BLOCK, ATT_OUT_W), lambda b, r, n: (b, n, r))
    o, lse = pl.pallas_call(
        functools.partial(_att_kernel, group=group, dilation=d, span=window // d),
        grid=(B, d, nL),
        in_specs=[cur(OFF_AQ), prev(OFF_AK), cur(OFF_AK), prev(OFF_AV), cur(OFF_AV)],
        out_specs=[out_spec, out_spec],
        out_shape=[jax.ShapeDtypeStruct((B, L, d * ATT_OUT_W), BF16),
                   jax.ShapeDtypeStruct((B, L, d * ATT_OUT_W), F32)],
        compiler_params=_cparams(("parallel", "parallel", "arbitrary")),
        name=f"att_g{group}",
    )(view, view, view, view, view)
    return o.reshape(B * S, ATT_OUT_W), lse.reshape(B * S, ATT_OUT_W)


def _mix_kernel(retg_ref, o1_ref, o2_ref, o3_ref, l1_ref, l2_ref, l3_ref, gr_ref, ga_ref, x_ref,
                wro_ref, wao_ref, wo_ref, g2_ref, wq_ref, sk_ref, h_ref, xn_ref, sc_ref):
    l1, l2, l3 = l1_ref[...], l2_ref[...], l3_ref[...]
    mx = jnp.maximum(jnp.maximum(l1, l2), l3)
    e1, e2, e3 = jnp.exp(l1 - mx), jnp.exp(l2 - mx), jnp.exp(l3 - mx)
    att = (e1 * o1_ref[...].astype(F32) + e2 * o2_ref[...].astype(F32) + e3 * o3_ref[...].astype(F32))
    att = att / (e1 + e2 + e3)
    a_branch = jnp.dot(att.astype(BF16), wao_ref[...], preferred_element_type=F32)
    r_branch = jnp.dot(retg_ref[...], wro_ref[...], preferred_element_type=F32)
    merged = (jax.nn.sigmoid(gr_ref[...].astype(F32)) * r_branch
              + jax.nn.sigmoid(ga_ref[...].astype(F32)) * a_branch)
    h = x_ref[...] + jnp.dot(merged.astype(BF16), wo_ref[...], preferred_element_type=F32)
    h_ref[...] = h
    xn = h * lax.rsqrt(jnp.mean(h * h, axis=-1, keepdims=True) + EPS) * g2_ref[...]
    xn_ref[...] = xn
    qp = jnp.dot(xn.astype(BF16), wq_ref[...], preferred_element_type=F32).astype(BF16)
    half = PEER_DKEY // 2
    for hp in range(2 * PEER_HEADS):
        sc_ref[hp] = lax.dot_general(sk_ref[hp], qp[:, hp * half:(hp + 1) * half],
                                     (((1,), (1,)), ((), ())), preferred_element_type=F32)


def _mix(retg, o1, o2, o3, l1, l2, l3, proj, x2, wro, wao, wo, g2, wq, sk):
    T = x2.shape[0]
    tm = MIX_TM
    row = lambda w: pl.BlockSpec((tm, w), lambda i: (i, 0))
    nhp = 2 * PEER_HEADS
    return pl.pallas_call(
        _mix_kernel,
        grid=(T // tm,),
        in_specs=[
            row(RET_V_W), row(ATT_OUT_W), row(ATT_OUT_W), row(ATT_OUT_W),
            row(ATT_OUT_W), row(ATT_OUT_W), row(ATT_OUT_W),
            pl.BlockSpec((tm, D_MODEL), lambda i: (i, OFF_GR // D_MODEL)),
            pl.BlockSpec((tm, D_MODEL), lambda i: (i, OFF_GA // D_MODEL)),
            row(D_MODEL),
            _const_spec(wro.shape), _const_spec(wao.shape), _const_spec(wo.shape),
            _const_spec(g2.shape), _const_spec(wq.shape), _const_spec(sk.shape),
        ],
        out_specs=[row(D_MODEL), row(D_MODEL),
                   pl.BlockSpec((nhp, PEER_NKEYS, tm), lambda i: (0, 0, i))],
        out_shape=[jax.ShapeDtypeStruct((T, D_MODEL), F32),
                   jax.ShapeDtypeStruct((T, D_MODEL), F32),
                   jax.ShapeDtypeStruct((nhp, PEER_NKEYS, T), F32)],
        compiler_params=_cparams(("parallel",)),
        name="mix",
    )(retg, o1, o2, o3, l1, l2, l3, proj, proj, x2, wro, wao, wo, g2, wq, sk)


def _topk_rows(s, k):
    n = s.shape[0]
    iota = lax.broadcasted_iota(jnp.int32, s.shape, 0)
    vals, idxs = [], []
    for _ in range(k):
        m = jnp.max(s, axis=0, keepdims=True)
        i = jnp.min(jnp.where(s == m, iota, n), axis=0, keepdims=True)
        vals.append(m)
        idxs.append(i)
        s = jnp.where(iota == i, -jnp.inf, s)
    return jnp.concatenate(vals, axis=0), jnp.concatenate(idxs, axis=0)


def _topk_kernel(sc_ref, idx_ref, gate_ref):
    K = PEER_TOPK
    v0, i0 = _topk_rows(sc_ref[0], K)
    v1, i1 = _topk_rows(sc_ref[1], K)
    cand = jnp.concatenate([v0[a:a + 1] + v1 for a in range(K)], axis=0)
    best_s, best_c = _topk_rows(cand, K)
    ca = best_c // K
    cb = best_c % K
    ia = jnp.zeros_like(best_c)
    ib = jnp.zeros_like(best_c)
    for a in range(K):
        ia = jnp.where(ca == a, i0[a:a + 1], ia)
        ib = jnp.where(cb == a, i1[a:a + 1], ib)
    idx_ref[0] = ia * PEER_NKEYS + ib
    e = jnp.exp(best_s - best_s[0:1])
    gate_ref[0] = e / jnp.sum(e, axis=0, keepdims=True)


def _topk(scT):
    T = scT.shape[-1]
    tb = TOPK_TB
    out_spec = pl.BlockSpec((1, PEER_TOPK, tb), lambda i, h: (h, 0, i))
    return pl.pallas_call(
        _topk_kernel,
        grid=(T // tb, PEER_HEADS),
        in_specs=[pl.BlockSpec((2, PEER_NKEYS, tb), lambda i, h: (h, 0, i))],
        out_specs=[out_spec, out_spec],
        out_shape=[jax.ShapeDtypeStruct((PEER_HEADS, PEER_TOPK, T), jnp.int32),
                   jax.ShapeDtypeStruct((PEER_HEADS, PEER_TOPK, T), F32)],
        compiler_params=_cparams(("parallel", "parallel")),
        name="topk",
    )(scT)


def _pack_table(tab):
    E = tab.shape[0]
    b = lax.bitcast_convert_type(tab.astype(BF16), jnp.uint16).astype(jnp.uint32)
    b = b.reshape(E // 2, 2, 8, 128)
    return b[:, 0] | (b[:, 1] << 16)


def _expert_row(tab_ref, e):
    word = tab_ref[lax.shift_right_logical(e, 1)]
    sh = ((1 - (e & 1)) * 16).astype(jnp.uint32)
    return pltpu.bitcast((word << sh) & jnp.uint32(0xFFFF0000), F32)


def _sublane_sums(ps):
    sub = lax.broadcasted_iota(jnp.int32, (8, 128), 0)
    lo4 = sub < 4
    t = [jnp.where(lo4, ps[j], ps[j + 4]) + pltpu.roll(jnp.where(lo4, ps[j + 4], ps[j]), 4, 0)
         for j in range(4)]
    m2 = (sub & 2) == 0
    u = [jnp.where(m2, t[j] + pltpu.roll(t[j], 6, 0), t[j + 2] + pltpu.roll(t[j + 2], 2, 0))
         for j in range(2)]
    m1 = (sub & 1) == 0
    return jnp.where(m1, u[0] + pltpu.roll(u[0], 7, 0), u[1] + pltpu.roll(u[1], 1, 0))


def _peer_u_kernel(idx_ref, x_ref, gate_ref, tab_ref, w_ref, act_ref):
    tb = x_ref.shape[0]
    ones = jnp.ones((8, 128), BF16)

    def body(t, carry):
        xt = x_ref[t]
        rs = []
        for g in range(PEER_PICKS // 8):
            ps = [_expert_row(tab_ref, idx_ref[0, 0, t * PEER_PICKS + g * 8 + j]) * xt for j in range(8)]
            rs.append(_sublane_sums(ps))
        r = jnp.concatenate(rs, axis=0)
        hi = r.astype(BF16)
        lo = (r - hi.astype(F32)).astype(BF16)
        nt = (((1,), (1,)), ((), ()))
        act = (lax.dot_general(ones, hi, nt, preferred_element_type=F32)
               + lax.dot_general(ones, lo, nt, preferred_element_type=F32))
        act_ref[pl.ds(t, 1), :] = act[0:1]
        return carry

    lax.fori_loop(0, tb, body, 0)
    a = act_ref[...]
    gelu = 0.5 * a * (1.0 + lax.erf(a * (2.0 ** -0.5)))
    w_ref[...] = gate_ref[...] * gelu


def _peer_u(idx_blocks, x3, gate, tab):
    T = x3.shape[0]
    tb = PEER_TB
    return pl.pallas_call(
        _peer_u_kernel,
        grid=(T // tb,),
        in_specs=[
            pl.BlockSpec((1, 1, tb * PEER_PICKS), lambda i: (i, 0, 0), memory_space=pltpu.SMEM),
            pl.BlockSpec((tb, 8, 128), lambda i: (i, 0, 0)),
            pl.BlockSpec((tb, PEER_PICKS), lambda i: (i, 0)),
            pl.BlockSpec(tab.shape, lambda i: (0, 0, 0), pipeline_mode=pl.Buffered(1)),
        ],
        out_specs=pl.BlockSpec((tb, PEER_PICKS), lambda i: (i, 0)),
        out_shape=jax.ShapeDtypeStruct((T, PEER_PICKS), F32),
        scratch_shapes=[pltpu.VMEM((tb, PEER_PICKS), F32)],
        compiler_params=_cparams(("parallel",)),
        name="peer_u",
    )(idx_blocks, x3, gate, tab)


def _peer_v_kernel(idx_ref, w_ref, tab_ref, o_ref):
    tb = o_ref.shape[0]
    nacc = 4

    def body(t, carry):
        acc = [jnp.zeros((8, 128), F32) for _ in range(nacc)]
        for j in range(PEER_PICKS):
            k = t * PEER_PICKS + j
            acc[j % nacc] = acc[j % nacc] + w_ref[0, 0, k] * _expert_row(tab_ref, idx_ref[0, 0, k])
        o_ref[t] = (acc[0] + acc[1]) + (acc[2] + acc[3])
        return carry

    lax.fori_loop(0, tb, body, 0)


def _peer_v(idx_blocks, w_blocks, tab):
    nb = idx_blocks.shape[0]
    tb = PEER_TB
    smem = lambda: pl.BlockSpec((1, 1, tb * PEER_PICKS), lambda i: (i, 0, 0), memory_space=pltpu.SMEM)
    return pl.pallas_call(
        _peer_v_kernel,
        grid=(nb,),
        in_specs=[smem(), smem(),
                  pl.BlockSpec(tab.shape, lambda i: (0, 0, 0), pipeline_mode=pl.Buffered(1))],
        out_specs=pl.BlockSpec((tb, 8, 128), lambda i: (i, 0, 0)),
        out_shape=jax.ShapeDtypeStruct((nb * tb, 8, 128), F32),
        compiler_params=_cparams(("parallel",)),
        name="peer_v",
    )(idx_blocks, w_blocks, tab)


def _final_kernel(h_ref, p_ref, g_ref, o_ref):
    h = h_ref[...] + p_ref[...]
    o_ref[...] = h * lax.rsqrt(jnp.mean(h * h, axis=-1, keepdims=True) + EPS) * g_ref[...]


def _final(h, p, g):
    T = h.shape[0]
    row = pl.BlockSpec((FINAL_TM, D_MODEL), lambda i: (i, 0))
    return pl.pallas_call(
        _final_kernel,
        grid=(T // FINAL_TM,),
        in_specs=[row, row, _const_spec((1, D_MODEL))],
        out_specs=row,
        out_shape=jax.ShapeDtypeStruct((T, D_MODEL), F32),
        compiler_params=_cparams(("parallel",)),
        name="final_norm",
    )(h, p, g)


def _layer(h2, B, S, norm1_g, w_in, w_ret_out, w_att_out, w_out, norm2_g, peer_wq, peer_subkeys, peer_u, peer_v):
    T = B * S
    w_perm = jnp.concatenate([w_in[:, :3072], w_in[:, 7680:9728], w_in[:, 3072:7680]], axis=1).astype(BF16)
    proj = _proj(h2, norm1_g.reshape(1, D_MODEL), w_perm)
    proj3 = proj.reshape(B, S, IN_W)
    retg = _retention(proj3).reshape(T, RET_V_W)
    (o1, l1), (o2, l2), (o3, l3) = (_att_group(proj3, g) for g in range(len(ATT_GROUPS)))
    sk = peer_subkeys.reshape(2 * PEER_HEADS, PEER_NKEYS, PEER_DKEY // 2).astype(BF16)
    h_mid, xn2, scT = _mix(retg, o1, o2, o3, l1, l2, l3, proj, h2,
                           w_ret_out.astype(BF16), w_att_out.astype(BF16), w_out.astype(BF16),
                           norm2_g.reshape(1, D_MODEL), peer_wq.astype(BF16), sk)
    idxT, gateT = _topk(scT)
    nb = T // PEER_TB
    idx_blocks = idxT.reshape(PEER_PICKS, T).T.reshape(nb, 1, PEER_TB * PEER_PICKS)
    gate = gateT.reshape(PEER_PICKS, T).T
    w = _peer_u(idx_blocks, xn2.reshape(T, 8, 128), gate, _pack_table(peer_u))
    pout = _peer_v(idx_blocks, w.reshape(nb, 1, PEER_TB * PEER_PICKS), _pack_table(peer_v))
    return h_mid, pout.reshape(T, D_MODEL)


def kernel(x, norm1_g, w_in, w_ret_out, w_att_out, w_out, norm2_g, peer_wq, peer_subkeys, peer_u, peer_v, normf_g):
    B, S, D = x.shape
    assert w_in.shape[0] == 1, "single-layer block"
    h, pout = _layer(x.reshape(B * S, D), B, S, norm1_g[0], w_in[0], w_ret_out[0], w_att_out[0], w_out[0],
                     norm2_g[0], peer_wq[0], peer_subkeys[0], peer_u[0], peer_v[0])
    return _final(h, pout, normf_g.reshape(1, D)).reshape(B, S, D)
```

```python
import functools
import math

import jax
import jax.numpy as jnp
from jax import lax
from jax.experimental import pallas as pl
from jax.experimental.pallas import tpu as pltpu

F32 = jnp.float32
BF16 = jnp.bfloat16

D_MODEL = 1024
RET_HEADS = 8
RET_DK = 64
RET_DV = 128
RET_CHUNK = 128
ATT_GROUPS = ((128, 1), (512, 4), (2048, 16))
ATT_HPG = 4
ATT_HEADS = 12
ATT_DH = 128
ATT_BLOCK = 128
PEER_HEADS = 8
PEER_NKEYS = 128
PEER_DKEY = 256
PEER_TOPK = 16
PEER_PICKS = PEER_HEADS * PEER_TOPK
EPS = 1e-6

RET_QK_W = RET_HEADS * RET_DK
RET_V_W = RET_HEADS * RET_DV
ATT_W = ATT_HEADS * ATT_DH
ATT_OUT_W = ATT_HPG * ATT_DH
OFF_RQ, OFF_RK, OFF_RV, OFF_RG = 0, 512, 1024, 2048
OFF_GR, OFF_GA = 3072, 4096
OFF_AQ, OFF_AK, OFF_AV = 5120, 6656, 8192
IN_W = 9728

VMEM_LIMIT_BYTES = 60 * 1024 * 1024

PROJ_TM = 256
PROJ_TN = 512
MIX_TM = 256
TOPK_TB = 256
PEER_TB = 128
PEER_UNROLL = 8
FINAL_TM = 512


def _cparams(sem):
    return pltpu.CompilerParams(dimension_semantics=sem, vmem_limit_bytes=VMEM_LIMIT_BYTES)


def _const_spec(shape):
    nd = len(shape)
    return pl.BlockSpec(shape, lambda *_: (0,) * nd)


def _proj_kernel(x_ref, g_ref, w_ref, o_ref):
    x = x_ref[...]
    ms = jnp.mean(x * x, axis=-1, keepdims=True)
    xn = (x * lax.rsqrt(ms + EPS) * g_ref[...]).astype(BF16)
    for j in range(IN_W // PROJ_TN):
        sl = slice(j * PROJ_TN, (j + 1) * PROJ_TN)
        o_ref[:, sl] = jnp.dot(xn, w_ref[:, sl], preferred_element_type=F32).astype(BF16)


def _proj(x2, g, w):
    T = x2.shape[0]
    return pl.pallas_call(
        _proj_kernel,
        grid=(T // PROJ_TM,),
        in_specs=[
            pl.BlockSpec((PROJ_TM, D_MODEL), lambda i: (i, 0)),
            _const_spec((1, D_MODEL)),
            pl.BlockSpec((D_MODEL, IN_W), lambda i: (0, 0), pipeline_mode=pl.Buffered(1)),
        ],
        out_specs=pl.BlockSpec((PROJ_TM, IN_W), lambda i: (i, 0)),
        out_shape=jax.ShapeDtypeStruct((T, IN_W), BF16),
        compiler_params=_cparams(("parallel",)),
        name="proj",
    )(x2, g, w)


def _retention_kernel(q_ref, k_ref, v_ref, rg_ref, o_ref, state_ref):
    C = RET_CHUNK

    @pl.when(pl.program_id(1) == 0)
    def _():
        state_ref[...] = jnp.zeros_like(state_ref)

    pi = lax.broadcasted_iota(jnp.int32, (C, C), 0)
    pj = lax.broadcasted_iota(jnp.int32, (C, C), 1)
    diff = (pi - pj).astype(F32)
    causal = diff >= 0
    diffc = jnp.maximum(diff, 0.0)
    pos = lax.broadcasted_iota(jnp.int32, (C, 1), 0).astype(F32)
    q = q_ref[0]
    k = k_ref[0]
    v = v_ref[0]
    rg = rg_ref[0]
    for h in range(RET_HEADS):
        lg = math.log1p(-(2.0 ** (-5.0 - h)))
        decay = jnp.where(causal, jnp.exp(diffc * lg), 0.0)
        w_k = jnp.exp((C - 1 - pos) * lg)
        w_q = jnp.exp((pos + 1.0) * lg)
        qh = q[:, h * RET_DK:(h + 1) * RET_DK]
        kh = k[:, h * RET_DK:(h + 1) * RET_DK].astype(F32) * (RET_DK ** -0.5)
        vh = v[:, h * RET_DV:(h + 1) * RET_DV]
        s = lax.dot_general(qh, kh.astype(BF16), (((1,), (1,)), ((), ())), preferred_element_type=F32)
        p = (s * decay).astype(BF16)
        inner = jnp.dot(p, vh, preferred_element_type=F32)
        st = state_ref[h]
        cross = jnp.dot((qh.astype(F32) * w_q).astype(BF16), st.astype(BF16), preferred_element_type=F32)
        kw = (kh * w_k).astype(BF16)
        kv = lax.dot_general(kw, vh, (((0,), (0,)), ((), ())), preferred_element_type=F32)
        state_ref[h] = math.exp(C * lg) * st + kv
        ret = inner + cross
        rn = ret * lax.rsqrt(jnp.mean(ret * ret, axis=-1, keepdims=True) + EPS)
        g = rg[:, h * RET_DV:(h + 1) * RET_DV].astype(F32)
        o_ref[0, :, h * RET_DV:(h + 1) * RET_DV] = (g * jax.nn.sigmoid(g) * rn).astype(BF16)


def _retention(proj3):
    B, S, _ = proj3.shape
    n = S // RET_CHUNK
    return pl.pallas_call(
        _retention_kernel,
        grid=(B, n),
        in_specs=[
            pl.BlockSpec((1, RET_CHUNK, RET_QK_W), lambda b, c: (b, c, OFF_RQ // RET_QK_W)),
            pl.BlockSpec((1, RET_CHUNK, RET_QK_W), lambda b, c: (b, c, OFF_RK // RET_QK_W)),
            pl.BlockSpec((1, RET_CHUNK, RET_V_W), lambda b, c: (b, c, OFF_RV // RET_V_W)),
            pl.BlockSpec((1, RET_CHUNK, RET_V_W), lambda b, c: (b, c, OFF_RG // RET_V_W)),
        ],
        out_specs=pl.BlockSpec((1, RET_CHUNK, RET_V_W), lambda b, c: (b, c, 0)),
        out_shape=jax.ShapeDtypeStruct((B, S, RET_V_W), BF16),
        scratch_shapes=[pltpu.VMEM((RET_HEADS, RET_DK, RET_DV), F32)],
        compiler_params=_cparams(("parallel", "arbitrary")),
        name="retention",
    )(proj3, proj3, proj3, proj3)


def _att_kernel(q_ref, kp_ref, kc_ref, vp_ref, vc_ref, o_ref, lse_ref, *, group, dilation, span):
    n = pl.program_id(2)
    iq = lax.broadcasted_iota(jnp.int32, (ATT_BLOCK, 2 * ATT_BLOCK), 0)
    jk = lax.broadcasted_iota(jnp.int32, (ATT_BLOCK, 2 * ATT_BLOCK), 1)
    dist = iq + ATT_BLOCK - jk
    valid = (dist >= 0) & (dist <= span) & ((jk >= ATT_BLOCK) | (n > 0))
    distf = (dilation * dist).astype(F32)
    q = q_ref[0]
    kk = jnp.concatenate([kp_ref[0], kc_ref[0]], axis=0)
    vv = jnp.concatenate([vp_ref[0], vc_ref[0]], axis=0)
    for i in range(ATT_HPG):
        slope = 2.0 ** (-8.0 * (group * ATT_HPG + i + 1) / ATT_HEADS)
        sl = slice(i * ATT_DH, (i + 1) * ATT_DH)
        s = lax.dot_general(q[:, sl], kk[:, sl], (((1,), (1,)), ((), ())), preferred_element_type=F32)
        s = s * (ATT_DH ** -0.5)
        s = jnp.where(valid, s - slope * distf, -jnp.inf)
        m = jnp.max(s, axis=-1, keepdims=True)
        p = jnp.exp(s - m)
        den = jnp.sum(p, axis=-1, keepdims=True)
        o = jnp.dot(p.astype(BF16), vv[:, sl], preferred_element_type=F32) / den
        o_ref[0, :, sl] = o.astype(BF16)
        lse_ref[0, :, sl] = jnp.broadcast_to(m + jnp.log(den), (ATT_BLOCK, ATT_DH))


def _att_group(proj3, group):
    window, d = ATT_GROUPS[group]
    B, S, _ = proj3.shape
    L = S // d
    nL = L // ATT_BLOCK
    view = proj3.reshape(B, L, d * IN_W)
    cb = IN_W // ATT_OUT_W

    def cur(off):
        return pl.BlockSpec((1, ATT_BLOCK, ATT_OUT_W),
                            lambda b, r, n: (b, n, r * cb + off // ATT_OUT_W + group))

    def prev(off):
        return pl.BlockSpec((1, ATT_BLOCK, ATT_OUT_W),
                            lambda b, r, n: (b, jnp.maximum(n - 1, 0), r * cb + off // ATT_OUT_W + group))

    out_spec = pl.BlockSpec((1, ATT_BLOCK, ATT_OUT_W), lambda b, r, n: (b, n, r))
    o, lse = pl.pallas_call(
        functools.partial(_att_kernel, group=group, dilation=d, span=window // d),
        grid=(B, d, nL),
        in_specs=[cur(OFF_AQ), prev(OFF_AK), cur(OFF_AK), prev(OFF_AV), cur(OFF_AV)],
        out_specs=[out_spec, out_spec],
        out_shape=[jax.ShapeDtypeStruct((B, L, d * ATT_OUT_W), BF16),
                   jax.ShapeDtypeStruct((B, L, d * ATT_OUT_W), F32)],
        compiler_params=_cparams(("parallel", "parallel", "arbitrary")),
        name=f"att_g{group}",
    )(view, view, view, view, view)
    return o.reshape(B * S, ATT_OUT_W), lse.reshape(B * S, ATT_OUT_W)


def _mix_kernel(retg_ref, o1_ref, o2_ref, o3_ref, l1_ref, l2_ref, l3_ref, gr_ref, ga_ref, x_ref,
                wro_ref, wao_ref, wo_ref, g2_ref, wq_ref, sk_ref, h_ref, xn_ref, sc_ref):
    l1, l2, l3 = l1_ref[...], l2_ref[...], l3_ref[...]
    mx = jnp.maximum(jnp.maximum(l1, l2), l3)
    e1, e2, e3 = jnp.exp(l1 - mx), jnp.exp(l2 - mx), jnp.exp(l3 - mx)
    att = (e1 * o1_ref[...].astype(F32) + e2 * o2_ref[...].astype(F32) + e3 * o3_ref[...].astype(F32))
    att = att / (e1 + e2 + e3)
    a_branch = jnp.dot(att.astype(BF16), wao_ref[...], preferred_element_type=F32)
    r_branch = jnp.dot(retg_ref[...], wro_ref[...], preferred_element_type=F32)
    merged = (jax.nn.sigmoid(gr_ref[...].astype(F32)) * r_branch
              + jax.nn.sigmoid(ga_ref[...].astype(F32)) * a_branch)
    h = x_ref[...] + jnp.dot(merged.astype(BF16), wo_ref[...], preferred_element_type=F32)
    h_ref[...] = h
    xn = h * lax.rsqrt(jnp.mean(h * h, axis=-1, keepdims=True) + EPS) * g2_ref[...]
    xn_ref[...] = xn
    qp = jnp.dot(xn.astype(BF16), wq_ref[...], preferred_element_type=F32).astype(BF16)
    half = PEER_DKEY // 2
    for hp in range(2 * PEER_HEADS):
        sc_ref[hp] = lax.dot_general(sk_ref[hp], qp[:, hp * half:(hp + 1) * half],
                                     (((1,), (1,)), ((), ())), preferred_element_type=F32)


def _mix(retg, o1, o2, o3, l1, l2, l3, proj, x2, wro, wao, wo, g2, wq, sk):
    T = x2.shape[0]
    tm = MIX_TM
    row = lambda w: pl.BlockSpec((tm, w), lambda i: (i, 0))
    nhp = 2 * PEER_HEADS
    return pl.pallas_call(
        _mix_kernel,
        grid=(T // tm,),
        in_specs=[
            row(RET_V_W), row(ATT_OUT_W), row(ATT_OUT_W), row(ATT_OUT_W),
            row(ATT_OUT_W), row(ATT_OUT_W), row(ATT_OUT_W),
            pl.BlockSpec((tm, D_MODEL), lambda i: (i, OFF_GR // D_MODEL)),
            pl.BlockSpec((tm, D_MODEL), lambda i: (i, OFF_GA // D_MODEL)),
            row(D_MODEL),
            _const_spec(wro.shape), _const_spec(wao.shape), _const_spec(wo.shape),
            _const_spec(g2.shape), _const_spec(wq.shape), _const_spec(sk.shape),
        ],
        out_specs=[row(D_MODEL), row(D_MODEL),
                   pl.BlockSpec((nhp, PEER_NKEYS, tm), lambda i: (0, 0, i))],
        out_shape=[jax.ShapeDtypeStruct((T, D_MODEL), F32),
                   jax.ShapeDtypeStruct((T, D_MODEL), F32),
                   jax.ShapeDtypeStruct((nhp, PEER_NKEYS, T), F32)],
        compiler_params=_cparams(("parallel",)),
        name="mix",
    )(retg, o1, o2, o3, l1, l2, l3, proj, proj, x2, wro, wao, wo, g2, wq, sk)


def _topk_rows(s, ids, k, fill):
    vals, idxs = [], []
    for _ in range(k):
        m = jnp.max(s, axis=0, keepdims=True)
        i = jnp.min(jnp.where(s == m, ids, fill), axis=0, keepdims=True)
        vals.append(m)
        idxs.append(i)
        s = jnp.where(ids == i, -jnp.inf, s)
    return jnp.concatenate(vals, axis=0), jnp.concatenate(idxs, axis=0)


_CAND_B = [PEER_TOPK // (a + 1) for a in range(PEER_TOPK)]
_CAND_ROWS = -(-sum(_CAND_B) // 8) * 8


def _topk_kernel(sc_ref, off_ref, par_ref, gate_ref, e_scr, g_scr):
    K = PEER_TOPK
    W = sc_ref.shape[-1]
    key_ids = lax.broadcasted_iota(jnp.int32, (PEER_NKEYS, W), 0)
    npad = _CAND_ROWS - sum(_CAND_B)
    cand_ids = jnp.concatenate(
        [jnp.full((1, W), a * K + b, jnp.int32) for a in range(K) for b in range(_CAND_B[a])]
        + [jnp.full((npad, W), K * K, jnp.int32)], axis=0)
    pad = jnp.full((npad, W), -jnp.inf, F32)

    def head(h, carry):
        v0, i0 = _topk_rows(sc_ref[2 * h], key_ids, K, PEER_NKEYS)
        v1, i1 = _topk_rows(sc_ref[2 * h + 1], key_ids, K, PEER_NKEYS)
        cand = jnp.concatenate([v0[a:a + 1] + v1[0:_CAND_B[a]] for a in range(K)] + [pad], axis=0)
        best_s, best_c = _topk_rows(cand, cand_ids, K, K * K)
        ca = best_c // K
        cb = best_c % K
        ia = jnp.zeros_like(best_c)
        ib = jnp.zeros_like(best_c)
        for a in range(K):
            ia = jnp.where(ca == a, i0[a:a + 1], ia)
            ib = jnp.where(cb == a, i1[a:a + 1], ib)
        e_scr[h] = (ia * PEER_NKEYS + ib).astype(F32)
        ex = jnp.exp(best_s - best_s[0:1])
        g_scr[h] = ex / jnp.sum(ex, axis=0, keepdims=True)
        return carry

    lax.fori_loop(0, PEER_HEADS, head, 0)
    e = e_scr[...].reshape(PEER_PICKS, W).T.astype(jnp.int32)
    off_ref[...] = lax.shift_right_logical(e, 1) * 8
    par_ref[...] = (e & 1).astype(F32)
    gate_ref[...] = g_scr[...].reshape(PEER_PICKS, W).T


def _topk(scT):
    T = scT.shape[-1]
    tb = TOPK_TB
    out_spec = pl.BlockSpec((tb, PEER_PICKS), lambda i: (i, 0))
    return pl.pallas_call(
        _topk_kernel,
        grid=(T // tb,),
        in_specs=[pl.BlockSpec((2 * PEER_HEADS, PEER_NKEYS, tb), lambda i: (0, 0, i))],
        out_specs=[out_spec, out_spec, out_spec],
        out_shape=[jax.ShapeDtypeStruct((T, PEER_PICKS), jnp.int32),
                   jax.ShapeDtypeStruct((T, PEER_PICKS), F32),
                   jax.ShapeDtypeStruct((T, PEER_PICKS), F32)],
        scratch_shapes=[pltpu.VMEM((PEER_HEADS, PEER_TOPK, tb), F32),
                        pltpu.VMEM((PEER_HEADS, PEER_TOPK, tb), F32)],
        compiler_params=_cparams(("parallel",)),
        name="topk",
    )(scT)


PAIRS = PEER_PICKS // 2
PAIR_K = 16 * PAIRS


def _pack_table(tab):
    E = tab.shape[0]
    b = lax.bitcast_convert_type(tab.astype(BF16), jnp.uint16).astype(jnp.uint32).reshape(E // 2, 2, 8, 128)
    return (b[:, 0] | (b[:, 1] << 16)).reshape(4 * E, 128)


def _split3(a):
    p0 = a.astype(BF16)
    r1 = a - p0.astype(F32)
    p1 = r1.astype(BF16)
    p2 = (r1 - p1.astype(F32)).astype(BF16)
    return p0, p1, p2


def _pair_expand(v, which):
    j = lax.broadcasted_iota(jnp.int32, (PEER_PICKS, PAIR_K), 0)
    k = lax.broadcasted_iota(jnp.int32, (PEER_PICKS, PAIR_K), 1)
    onehot = jnp.where(j == 2 * (k // 16) + which, 1.0, 0.0).astype(BF16)
    return sum(jnp.dot(p, onehot, preferred_element_type=F32) for p in _split3(v))


def _pair_targets(par, which):
    r = lax.broadcasted_iota(jnp.int32, (par.shape[0], PAIR_K), 1) % 16
    return 0.5 * (r.astype(F32) - _pair_expand(par, which))


def _gather_tiles(tab_ref, off_ref, base):
    tiles = []
    for j in range(PEER_PICKS):
        off = pl.multiple_of(off_ref[0, 0, base + j], 8)
        tiles.append(pltpu.bitcast(tab_ref[pl.ds(off, 8), :], BF16))
    return tiles


def _sublane_sums(ps):
    sub = lax.broadcasted_iota(jnp.int32, (8, 128), 0)
    lo4 = sub < 4
    t = [jnp.where(lo4, ps[j], ps[j + 4]) + pltpu.roll(jnp.where(lo4, ps[j + 4], ps[j]), 4, 0)
         for j in range(4)]
    m2 = (sub & 2) == 0
    u = [jnp.where(m2, t[j] + pltpu.roll(t[j], 6, 0), t[j + 2] + pltpu.roll(t[j + 2], 2, 0))
         for j in range(2)]
    m1 = (sub & 1) == 0
    return jnp.where(m1, u[0] + pltpu.roll(u[0], 7, 0), u[1] + pltpu.roll(u[1], 1, 0))


def _peer_u_kernel(off_ref, par_ref, x_ref, gate_ref, tab_ref, w_ref, tga_ref, tgb_ref, v3_ref):
    tb = x_ref.shape[0]
    par = par_ref[...]
    tga_ref[...] = _pair_targets(par,0)
    tgb_ref[...] = _pair_targets(par,1)
    m_iota = lax.broadcasted_iota(jnp.int32, (8, PAIR_K), 0).astype(F32)
    zero = jnp.zeros((8, 128), F32)
    nt = (((1,), (1,)), ((), ()))

    def body(t, carry):
        tiles = _gather_tiles(tab_ref, off_ref, t * PEER_PICKS)
        wmat = jnp.concatenate(
            [jnp.concatenate([tiles[2 * q], tiles[2 * q + 1]], axis=1) for q in range(PAIRS)], axis=0)
        xt = x_ref[t]
        xf = jnp.concatenate([jnp.concatenate([xt, zero], axis=1),
                              jnp.concatenate([zero, xt], axis=1)], axis=0)
        xh = xf.astype(BF16)
        xl = (xf - xh.astype(F32)).astype(BF16)
        o = lax.dot_general(jnp.concatenate([xh, xl], axis=0), wmat, nt, preferred_element_type=F32)
        o = o[0:16] + o[16:32]
        za = jnp.where(tga_ref[pl.ds(t, 1), :] == m_iota, o[0:8], 0.0)
        zb = jnp.where(tgb_ref[pl.ds(t, 1), :] == m_iota, o[8:16], 0.0)
        va = _sublane_sums([za[:, i * 128:(i + 1) * 128] for i in range(8)])
        vb = _sublane_sums([zb[:, i * 128:(i + 1) * 128] for i in range(8)])
        for i in range(8):
            v3_ref[i, pl.ds(t, 1), :] = va[i:i + 1]
            v3_ref[8 + i, pl.ds(t, 1), :] = vb[i:i + 1]
        return carry

    def group(i, carry):
        for s in range(PEER_UNROLL):
            body(i * PEER_UNROLL + s, carry)
        return carry

    lax.fori_loop(0, tb // PEER_UNROLL, group, 0)
    lane = lax.broadcasted_iota(jnp.int32, (128, PEER_PICKS), 0)
    pick = lax.broadcasted_iota(jnp.int32, (128, PEER_PICKS), 1)
    act = jnp.zeros((tb, PEER_PICKS), F32)
    for c in range(16):
        sel = jnp.where(pick == 2 * (8 * (c % 8) + lane // 16) + c // 8, 1.0, 0.0).astype(BF16)
        for p in _split3(v3_ref[c]):
            act = act + jnp.dot(p, sel, preferred_element_type=F32)
    gelu = 0.5 * act * (1.0 + lax.erf(act * (2.0 ** -0.5)))
    w_ref[...] = gate_ref[...] * gelu


def _smem_block(tb):
    return pl.BlockSpec((1, 1, tb * PEER_PICKS), lambda i: (i, 0, 0), memory_space=pltpu.SMEM)


def _peer_u(idx_blocks, idx, x3, gate, tab):
    T = x3.shape[0]
    tb = PEER_TB
    rows = pl.BlockSpec((tb, PEER_PICKS), lambda i: (i, 0))
    return pl.pallas_call(
        _peer_u_kernel,
        grid=(T // tb,),
        in_specs=[
            _smem_block(tb), rows,
            pl.BlockSpec((tb, 8, 128), lambda i: (i, 0, 0)),
            rows,
            pl.BlockSpec(tab.shape, lambda i: (0, 0), pipeline_mode=pl.Buffered(1)),
        ],
        out_specs=rows,
        out_shape=jax.ShapeDtypeStruct((T, PEER_PICKS), F32),
        scratch_shapes=[pltpu.VMEM((tb, PAIR_K), F32), pltpu.VMEM((tb, PAIR_K), F32),
                        pltpu.VMEM((16, tb, 128), F32)],
        compiler_params=_cparams(("parallel",)),
        name="peer_u",
    )(idx_blocks, idx, x3, gate, tab)


def _peer_v_kernel(off_ref, par_ref, w_ref, tab_ref, o_ref, wa_ref, wb_ref, tga_ref, tgb_ref):
    tb = o_ref.shape[0]
    par = par_ref[...]
    w = w_ref[...]
    wa_ref[...] = _pair_expand(w, 0)
    wb_ref[...] = _pair_expand(w, 1)
    tga_ref[...] = _pair_targets(par,0)
    tgb_ref[...] = _pair_targets(par,1)
    m_iota = lax.broadcasted_iota(jnp.int32, (8, PAIR_K), 0).astype(F32)

    def body(t, carry):
        tiles = _gather_tiles(tab_ref, off_ref, t * PEER_PICKS)
        vmat = jnp.concatenate([jnp.concatenate(tiles[0::2], axis=0),
                                jnp.concatenate(tiles[1::2], axis=0)], axis=1)
        row = pl.ds(t, 1)
        la = jnp.where(tga_ref[row, :] == m_iota, wa_ref[row, :], 0.0)
        lb = jnp.where(tgb_ref[row, :] == m_iota, wb_ref[row, :], 0.0)
        lf = jnp.concatenate([la, lb], axis=0)
        lh = lf.astype(BF16)
        ll = (lf - lh.astype(F32)).astype(BF16)
        o = jnp.dot(jnp.concatenate([lh, ll], axis=0), vmat, preferred_element_type=F32)
        o_ref[t] = (o[0:8, 0:128] + o[8:16, 128:256]) + (o[16:24, 0:128] + o[24:32, 128:256])
        return carry

    def group(i, carry):
        for s in range(PEER_UNROLL):
            body(i * PEER_UNROLL + s, carry)
        return carry

    lax.fori_loop(0, tb // PEER_UNROLL, group, 0)


def _peer_v(idx_blocks, idx, w, tab):
    T = idx.shape[0]
    tb = PEER_TB
    rows = pl.BlockSpec((tb, PEER_PICKS), lambda i: (i, 0))
    return pl.pallas_call(
        _peer_v_kernel,
        grid=(T // tb,),
        in_specs=[_smem_block(tb), rows, rows,
                  pl.BlockSpec(tab.shape, lambda i: (0, 0), pipeline_mode=pl.Buffered(1))],
        out_specs=pl.BlockSpec((tb, 8, 128), lambda i: (i, 0, 0)),
        out_shape=jax.ShapeDtypeStruct((T, 8, 128), F32),
        scratch_shapes=[pltpu.VMEM((tb, PAIR_K), F32) for _ in range(4)],
        compiler_params=_cparams(("parallel",)),
        name="peer_v",
    )(idx_blocks, idx, w, tab)


def _final_kernel(h_ref, p_ref, g_ref, o_ref):
    h = h_ref[...] + p_ref[...]
    o_ref[...] = h * lax.rsqrt(jnp.mean(h * h, axis=-1, keepdims=True) + EPS) * g_ref[...]


def _final(h, p, g):
    T = h.shape[0]
    row = pl.BlockSpec((FINAL_TM, D_MODEL), lambda i: (i, 0))
    return pl.pallas_call(
        _final_kernel,
        grid=(T // FINAL_TM,),
        in_specs=[row, row, _const_spec((1, D_MODEL))],
        out_specs=row,
        out_shape=jax.ShapeDtypeStruct((T, D_MODEL), F32),
        compiler_params=_cparams(("parallel",)),
        name="final_norm",
    )(h, p, g)


def _layer(h2, B, S, norm1_g, w_in, w_ret_out, w_att_out, w_out, norm2_g, peer_wq, peer_subkeys, peer_u, peer_v):
    T = B * S
    w_perm = jnp.concatenate([w_in[:, :3072], w_in[:, 7680:9728], w_in[:, 3072:7680]], axis=1).astype(BF16)
    proj = _proj(h2, norm1_g.reshape(1, D_MODEL), w_perm)
    proj3 = proj.reshape(B, S, IN_W)
    retg = _retention(proj3).reshape(T, RET_V_W)
    (o1, l1), (o2, l2), (o3, l3) = (_att_group(proj3, g) for g in range(len(ATT_GROUPS)))
    sk = peer_subkeys.reshape(2 * PEER_HEADS, PEER_NKEYS, PEER_DKEY // 2).astype(BF16)
    h_mid, xn2, scT = _mix(retg, o1, o2, o3, l1, l2, l3, proj, h2,
                           w_ret_out.astype(BF16), w_att_out.astype(BF16), w_out.astype(BF16),
                           norm2_g.reshape(1, D_MODEL), peer_wq.astype(BF16), sk)
    off, par, gate = _topk(scT)
    off_blocks = off.reshape(T // PEER_TB, 1, PEER_TB * PEER_PICKS)
    w = _peer_u(off_blocks, par, xn2.reshape(T, 8, 128), gate, _pack_table(peer_u))
    pout = _peer_v(off_blocks, par, w, _pack_table(peer_v))
    return h_mid, pout.reshape(T, D_MODEL)


def kernel(x, norm1_g, w_in, w_ret_out, w_att_out, w_out, norm2_g, peer_wq, peer_subkeys, peer_u, peer_v, normf_g):
    B, S, D = x.shape
    assert w_in.shape[0] == 1, "single-layer block"
    h, pout = _layer(x.reshape(B * S, D), B, S, norm1_g[0], w_in[0], w_ret_out[0], w_att_out[0], w_out[0],
                     norm2_g[0], peer_wq[0], peer_subkeys[0], peer_u[0], peer_v[0])
    return _final(h, pout, normf_g.reshape(1, D)).reshape(B, S, D)
```

```python
import functools
import math

import jax
import jax.numpy as jnp
from jax import lax
from jax.experimental import pallas as pl
from jax.experimental.pallas import tpu as pltpu

F32 = jnp.float32
BF16 = jnp.bfloat16

D_MODEL = 1024
RET_HEADS = 8
RET_DK = 64
RET_DV = 128
RET_CHUNK = 128
ATT_GROUPS = ((128, 1), (512, 4), (2048, 16))
ATT_HPG = 4
ATT_HEADS = 12
ATT_DH = 128
ATT_BLOCK = 128
PEER_HEADS = 8
PEER_NKEYS = 128
PEER_DKEY = 256
PEER_TOPK = 16
PEER_PICKS = PEER_HEADS * PEER_TOPK
EPS = 1e-6

RET_QK_W = RET_HEADS * RET_DK
RET_V_W = RET_HEADS * RET_DV
ATT_W = ATT_HEADS * ATT_DH
ATT_OUT_W = ATT_HPG * ATT_DH
OFF_RQ, OFF_RK, OFF_RV, OFF_RG = 0, 512, 1024, 2048
OFF_GR, OFF_GA = 3072, 4096
MAIN_W = 5120
ATT_SEC_W = 3 * ATT_OUT_W
IN_W = MAIN_W + len(ATT_GROUPS) * ATT_SEC_W

VMEM_LIMIT_BYTES = 60 * 1024 * 1024

PROJ_TM = 256
PROJ_TN = 512
MIX_TM = 256
TOPK_TB = 256
PEER_TB = 128
PEER_UNROLL = 8
FINAL_TM = 512


def _cparams(sem):
    return pltpu.CompilerParams(dimension_semantics=sem, vmem_limit_bytes=VMEM_LIMIT_BYTES)


def _const_spec(shape):
    nd = len(shape)
    return pl.BlockSpec(shape, lambda *_: (0,) * nd)


def _proj_kernel(x_ref, g_ref, w_ref, main_ref, a0_ref, a1_ref, a2_ref, scr_ref):
    x = x_ref[...]
    ms = jnp.mean(x * x, axis=-1, keepdims=True)
    xn = (x * lax.rsqrt(ms + EPS) * g_ref[...]).astype(BF16)

    def chunk(j):
        return jnp.dot(xn, w_ref[:, j * PROJ_TN:(j + 1) * PROJ_TN], preferred_element_type=F32)

    nmain = MAIN_W // PROJ_TN
    for j in range(nmain):
        main_ref[:, j * PROJ_TN:(j + 1) * PROJ_TN] = chunk(j).astype(BF16)
    for s in range(3):
        a0_ref[:, s * PROJ_TN:(s + 1) * PROJ_TN] = chunk(nmain + s).astype(BF16)
    for g, a_ref in ((1, a1_ref), (2, a2_ref)):
        d = ATT_GROUPS[g][1]
        for s in range(3):
            res = chunk(nmain + 3 * g + s)
            for c in range(PROJ_TN // 128):
                scr_ref[c] = res[:, c * 128:(c + 1) * 128]
            for r in range(d):
                piece = jnp.concatenate(
                    [scr_ref[c, pl.ds(r, PROJ_TM // d, stride=d), :] for c in range(PROJ_TN // 128)], axis=1)
                a_ref[0, r, :, s * PROJ_TN:(s + 1) * PROJ_TN] = piece.astype(BF16)


def _proj(x2, g, w, B, S):
    T = x2.shape[0]
    tiles = S // PROJ_TM
    d1, d2 = ATT_GROUPS[1][1], ATT_GROUPS[2][1]
    return pl.pallas_call(
        _proj_kernel,
        grid=(T // PROJ_TM,),
        in_specs=[
            pl.BlockSpec((PROJ_TM, D_MODEL), lambda i: (i, 0)),
            _const_spec((1, D_MODEL)),
            pl.BlockSpec((D_MODEL, IN_W), lambda i: (0, 0), pipeline_mode=pl.Buffered(1)),
        ],
        out_specs=[
            pl.BlockSpec((PROJ_TM, MAIN_W), lambda i: (i, 0)),
            pl.BlockSpec((PROJ_TM, ATT_SEC_W), lambda i: (i, 0)),
            pl.BlockSpec((1, d1, PROJ_TM // d1, ATT_SEC_W), lambda i: (i // tiles, 0, i % tiles, 0)),
            pl.BlockSpec((1, d2, PROJ_TM // d2, ATT_SEC_W), lambda i: (i // tiles, 0, i % tiles, 0)),
        ],
        out_shape=[
            jax.ShapeDtypeStruct((T, MAIN_W), BF16),
            jax.ShapeDtypeStruct((T, ATT_SEC_W), BF16),
            jax.ShapeDtypeStruct((B, d1, S // d1, ATT_SEC_W), BF16),
            jax.ShapeDtypeStruct((B, d2, S // d2, ATT_SEC_W), BF16),
        ],
        scratch_shapes=[pltpu.VMEM((PROJ_TN // 128, PROJ_TM, 128), F32)],
        compiler_params=_cparams(("parallel",)),
        name="proj",
    )(x2, g, w)


def _retention_kernel(q_ref, k_ref, v_ref, rg_ref, o_ref, state_ref):
    C = RET_CHUNK

    @pl.when(pl.program_id(1) == 0)
    def _():
        state_ref[...] = jnp.zeros_like(state_ref)

    pi = lax.broadcasted_iota(jnp.int32, (C, C), 0)
    pj = lax.broadcasted_iota(jnp.int32, (C, C), 1)
    diff = (pi - pj).astype(F32)
    causal = diff >= 0
    diffc = jnp.maximum(diff, 0.0)
    pos = lax.broadcasted_iota(jnp.int32, (C, 1), 0).astype(F32)
    q = q_ref[0]
    k = k_ref[0]
    v = v_ref[0]
    rg = rg_ref[0]
    for h in range(RET_HEADS):
        lg = math.log1p(-(2.0 ** (-5.0 - h)))
        decay = jnp.where(causal, jnp.exp(diffc * lg), 0.0)
        w_k = jnp.exp((C - 1 - pos) * lg)
        w_q = jnp.exp((pos + 1.0) * lg)
        qh = q[:, h * RET_DK:(h + 1) * RET_DK]
        kh = k[:, h * RET_DK:(h + 1) * RET_DK].astype(F32) * (RET_DK ** -0.5)
        vh = v[:, h * RET_DV:(h + 1) * RET_DV]
        s = lax.dot_general(qh, kh.astype(BF16), (((1,), (1,)), ((), ())), preferred_element_type=F32)
        p = (s * decay).astype(BF16)
        inner = jnp.dot(p, vh, preferred_element_type=F32)
        st = state_ref[h]
        cross = jnp.dot((qh.astype(F32) * w_q).astype(BF16), st.astype(BF16), preferred_element_type=F32)
        kw = (kh * w_k).astype(BF16)
        kv = lax.dot_general(kw, vh, (((0,), (0,)), ((), ())), preferred_element_type=F32)
        state_ref[h] = math.exp(C * lg) * st + kv
        ret = inner + cross
        rn = ret * lax.rsqrt(jnp.mean(ret * ret, axis=-1, keepdims=True) + EPS)
        g = rg[:, h * RET_DV:(h + 1) * RET_DV].astype(F32)
        o_ref[0, :, h * RET_DV:(h + 1) * RET_DV] = (g * jax.nn.sigmoid(g) * rn).astype(BF16)


def _retention(proj3):
    B, S, _ = proj3.shape
    n = S // RET_CHUNK
    return pl.pallas_call(
        _retention_kernel,
        grid=(B, n),
        in_specs=[
            pl.BlockSpec((1, RET_CHUNK, RET_QK_W), lambda b, c: (b, c, OFF_RQ // RET_QK_W)),
            pl.BlockSpec((1, RET_CHUNK, RET_QK_W), lambda b, c: (b, c, OFF_RK // RET_QK_W)),
            pl.BlockSpec((1, RET_CHUNK, RET_V_W), lambda b, c: (b, c, OFF_RV // RET_V_W)),
            pl.BlockSpec((1, RET_CHUNK, RET_V_W), lambda b, c: (b, c, OFF_RG // RET_V_W)),
        ],
        out_specs=pl.BlockSpec((1, RET_CHUNK, RET_V_W), lambda b, c: (b, c, 0)),
        out_shape=jax.ShapeDtypeStruct((B, S, RET_V_W), BF16),
        scratch_shapes=[pltpu.VMEM((RET_HEADS, RET_DK, RET_DV), F32)],
        compiler_params=_cparams(("parallel", "arbitrary")),
        name="retention",
    )(proj3, proj3, proj3, proj3)


def _att_kernel(q_ref, kp_ref, kc_ref, vp_ref, vc_ref, o_ref, lse_ref, *, group, dilation, span):
    n = pl.program_id(1)
    r = pl.program_id(2)
    iq = lax.broadcasted_iota(jnp.int32, (ATT_BLOCK, 2 * ATT_BLOCK), 0)
    jk = lax.broadcasted_iota(jnp.int32, (ATT_BLOCK, 2 * ATT_BLOCK), 1)
    dist = iq + ATT_BLOCK - jk
    valid = (dist >= 0) & (dist <= span) & ((jk >= ATT_BLOCK) | (n > 0))
    distf = (dilation * dist).astype(F32)
    q = q_ref[0, 0]
    kk = jnp.concatenate([kp_ref[0, 0], kc_ref[0, 0]], axis=0)
    vv = jnp.concatenate([vp_ref[0, 0], vc_ref[0, 0]], axis=0)
    rows = pl.ds(r, ATT_BLOCK, stride=dilation) if dilation > 1 else slice(None)
    for i in range(ATT_HPG):
        slope = 2.0 ** (-8.0 * (group * ATT_HPG + i + 1) / ATT_HEADS)
        sl = slice(i * ATT_DH, (i + 1) * ATT_DH)
        s = lax.dot_general(q[:, sl], kk[:, sl], (((1,), (1,)), ((), ())), preferred_element_type=F32)
        s = s * (ATT_DH ** -0.5)
        s = jnp.where(valid, s - slope * distf, -jnp.inf)
        m = jnp.max(s, axis=-1, keepdims=True)
        p = jnp.exp(s - m)
        den = jnp.sum(p, axis=-1, keepdims=True)
        o_ref[i, 0, rows, :] = jnp.dot(p.astype(BF16), vv[:, sl], preferred_element_type=F32) / den
        lse_ref[i, 0, rows, :] = jnp.broadcast_to(m + jnp.log(den), (ATT_BLOCK, ATT_DH))


def _att_group(qkv, group):
    window, d = ATT_GROUPS[group]
    B, _, L, _ = qkv.shape
    S = L * d
    nL = L // ATT_BLOCK

    def cur(sec):
        return pl.BlockSpec((1, 1, ATT_BLOCK, ATT_OUT_W), lambda b, n, r: (b, r, n, sec))

    def prev(sec):
        return pl.BlockSpec((1, 1, ATT_BLOCK, ATT_OUT_W), lambda b, n, r: (b, r, jnp.maximum(n - 1, 0), sec))

    out_spec = pl.BlockSpec((ATT_HPG, 1, ATT_BLOCK * d, ATT_DH), lambda b, n, r: (0, b, n, 0))
    out_shape = jax.ShapeDtypeStruct((ATT_HPG, B, S, ATT_DH), F32)
    o, lse = pl.pallas_call(
        functools.partial(_att_kernel, group=group, dilation=d, span=window // d),
        grid=(B, nL, d),
        in_specs=[cur(0), prev(1), cur(1), prev(2), cur(2)],
        out_specs=[out_spec, out_spec],
        out_shape=[out_shape, out_shape],
        compiler_params=_cparams(("parallel", "arbitrary", "arbitrary")),
        name=f"att_g{group}",
    )(qkv, qkv, qkv, qkv, qkv)
    return o.reshape(ATT_HPG, B * S, ATT_DH), lse.reshape(ATT_HPG, B * S, ATT_DH)


def _mix_kernel(retg_ref, o1_ref, o2_ref, o3_ref, l1_ref, l2_ref, l3_ref, gr_ref, ga_ref, x_ref,
                wro_ref, wao_ref, wo_ref, g2_ref, wq_ref, sk_ref, h_ref, xn_ref, sc_ref):
    heads = lambda ref: jnp.concatenate([ref[i] for i in range(ATT_HPG)], axis=1)
    l1, l2, l3 = heads(l1_ref), heads(l2_ref), heads(l3_ref)
    mx = jnp.maximum(jnp.maximum(l1, l2), l3)
    e1, e2, e3 = jnp.exp(l1 - mx), jnp.exp(l2 - mx), jnp.exp(l3 - mx)
    att = (e1 * heads(o1_ref) + e2 * heads(o2_ref) + e3 * heads(o3_ref)) / (e1 + e2 + e3)
    a_branch = jnp.dot(att.astype(BF16), wao_ref[...], preferred_element_type=F32)
    r_branch = jnp.dot(retg_ref[...], wro_ref[...], preferred_element_type=F32)
    merged = (jax.nn.sigmoid(gr_ref[...].astype(F32)) * r_branch
              + jax.nn.sigmoid(ga_ref[...].astype(F32)) * a_branch)
    h = x_ref[...] + jnp.dot(merged.astype(BF16), wo_ref[...], preferred_element_type=F32)
    h_ref[...] = h
    xn = h * lax.rsqrt(jnp.mean(h * h, axis=-1, keepdims=True) + EPS) * g2_ref[...]
    for c in range(D_MODEL // 128):
        xn_ref[pl.ds(c, h.shape[0], stride=D_MODEL // 128), :] = xn[:, c * 128:(c + 1) * 128]
    qp =jnp.dot(xn.astype(BF16), wq_ref[...], preferred_element_type=F32).astype(BF16)
    half = PEER_DKEY // 2
    for hp in range(2 * PEER_HEADS):
        sc_ref[hp] = lax.dot_general(sk_ref[hp], qp[:, hp * half:(hp + 1) * half],
                                     (((1,), (1,)), ((), ())), preferred_element_type=F32)


def _mix(retg, o1, o2, o3, l1, l2, l3, proj, x2, wro, wao, wo, g2, wq, sk):
    T = x2.shape[0]
    tm = MIX_TM
    row = lambda w: pl.BlockSpec((tm, w), lambda i: (i, 0))
    hd = pl.BlockSpec((ATT_HPG, tm, ATT_DH), lambda i: (0, i, 0))
    nhp = 2 * PEER_HEADS
    nt = D_MODEL // 128
    return pl.pallas_call(
        _mix_kernel,
        grid=(T // tm,),
        in_specs=[
            row(RET_V_W), hd, hd, hd, hd, hd, hd,
            pl.BlockSpec((tm, D_MODEL), lambda i: (i, OFF_GR // D_MODEL)),
            pl.BlockSpec((tm, D_MODEL), lambda i: (i, OFF_GA // D_MODEL)),
            row(D_MODEL),
            _const_spec(wro.shape), _const_spec(wao.shape), _const_spec(wo.shape),
            _const_spec(g2.shape), _const_spec(wq.shape), _const_spec(sk.shape),
        ],
        out_specs=[row(D_MODEL), pl.BlockSpec((tm * nt, 128), lambda i: (i, 0)),
                   pl.BlockSpec((nhp, PEER_NKEYS, tm), lambda i: (0, 0, i))],
        out_shape=[jax.ShapeDtypeStruct((T, D_MODEL), F32),
                   jax.ShapeDtypeStruct((T * nt, 128), F32),
                   jax.ShapeDtypeStruct((nhp, PEER_NKEYS, T), F32)],
        compiler_params=_cparams(("parallel",)),
        name="mix",
    )(retg, o1, o2, o3, l1, l2, l3, proj, proj, x2, wro, wao, wo, g2, wq, sk)


def _topk_rows(s, ids, k, fill):
    vals, idxs = [], []
    for _ in range(k):
        m = jnp.max(s, axis=0, keepdims=True)
        i = jnp.min(jnp.where(s == m, ids, fill), axis=0, keepdims=True)
        vals.append(m)
        idxs.append(i)
        s = jnp.where(ids == i, -jnp.inf, s)
    return jnp.concatenate(vals, axis=0), jnp.concatenate(idxs, axis=0)


_CAND_B = [PEER_TOPK // (a + 1) for a in range(PEER_TOPK)]
_CAND_ROWS = -(-sum(_CAND_B) // 8) * 8


def _topk_kernel(sc_ref, off_ref, par_ref, gate_ref, e_scr, g_scr):
    K = PEER_TOPK
    W = sc_ref.shape[-1]
    key_ids = lax.broadcasted_iota(jnp.int32, (PEER_NKEYS, W), 0)
    npad = _CAND_ROWS - sum(_CAND_B)
    cand_ids = jnp.concatenate(
        [jnp.full((1, W), a * K + b, jnp.int32) for a in range(K) for b in range(_CAND_B[a])]
        + [jnp.full((npad, W), K * K, jnp.int32)], axis=0)
    pad = jnp.full((npad, W), -jnp.inf, F32)

    def head(h, carry):
        v0, i0 = _topk_rows(sc_ref[2 * h], key_ids, K, PEER_NKEYS)
        v1, i1 = _topk_rows(sc_ref[2 * h + 1], key_ids, K, PEER_NKEYS)
        cand = jnp.concatenate([v0[a:a + 1] + v1[0:_CAND_B[a]] for a in range(K)] + [pad], axis=0)
        best_s, best_c = _topk_rows(cand, cand_ids, K, K * K)
        ca = best_c // K
        cb = best_c % K
        ia = jnp.zeros_like(best_c)
        ib = jnp.zeros_like(best_c)
        for a in range(K):
            ia = jnp.where(ca == a, i0[a:a + 1], ia)
            ib = jnp.where(cb == a, i1[a:a + 1], ib)
        e_scr[h] = (ia * PEER_NKEYS + ib).astype(F32)
        ex = jnp.exp(best_s - best_s[0:1])
        g_scr[h] = ex / jnp.sum(ex, axis=0, keepdims=True)
        return carry

    lax.fori_loop(0, PEER_HEADS, head, 0)
    e = e_scr[...].reshape(PEER_PICKS, W).T.astype(jnp.int32)
    off_ref[...] = lax.shift_right_logical(e, 1) * 8
    par_ref[...] = (e & 1).astype(F32)
    gate_ref[...] = g_scr[...].reshape(PEER_PICKS, W).T


def _topk(scT):
    T = scT.shape[-1]
    tb = TOPK_TB
    out_spec = pl.BlockSpec((tb, PEER_PICKS), lambda i: (i, 0))
    return pl.pallas_call(
        _topk_kernel,
        grid=(T // tb,),
        in_specs=[pl.BlockSpec((2 * PEER_HEADS, PEER_NKEYS, tb), lambda i: (0, 0, i))],
        out_specs=[out_spec, out_spec, out_spec],
        out_shape=[jax.ShapeDtypeStruct((T, PEER_PICKS), jnp.int32),
                   jax.ShapeDtypeStruct((T, PEER_PICKS), F32),
                   jax.ShapeDtypeStruct((T, PEER_PICKS), F32)],
        scratch_shapes=[pltpu.VMEM((PEER_HEADS, PEER_TOPK, tb), F32),
                        pltpu.VMEM((PEER_HEADS, PEER_TOPK, tb), F32)],
        compiler_params=_cparams(("parallel",)),
        name="topk",
    )(scT)


PAIRS = PEER_PICKS // 2
PAIR_K = 16 * PAIRS


def _pack_table(tab):
    E = tab.shape[0]
    b = lax.bitcast_convert_type(tab.astype(BF16), jnp.uint16).astype(jnp.uint32).reshape(E // 2, 2, 8, 128)
    return (b[:, 0] | (b[:, 1] << 16)).reshape(4 * E, 128)


def _split3(a):
    p0 = a.astype(BF16)
    r1 = a - p0.astype(F32)
    p1 = r1.astype(BF16)
    p2 = (r1 - p1.astype(F32)).astype(BF16)
    return p0, p1, p2


def _pair_expand(v, which):
    j = lax.broadcasted_iota(jnp.int32, (PEER_PICKS, PAIR_K), 0)
    k = lax.broadcasted_iota(jnp.int32, (PEER_PICKS, PAIR_K), 1)
    onehot = jnp.where(j == 2 * (k // 16) + which, 1.0, 0.0).astype(BF16)
    return sum(jnp.dot(p, onehot, preferred_element_type=F32) for p in _split3(v))


def _pair_targets(par, which):
    r = lax.broadcasted_iota(jnp.int32, (par.shape[0], PAIR_K), 1) % 16
    return 0.5 * (r.astype(F32) - _pair_expand(par, which))


def _gather_tiles(tab_ref, off_ref, base):
    tiles = []
    for j in range(PEER_PICKS):
        off = pl.multiple_of(off_ref[0, 0, base + j], 8)
        tiles.append(pltpu.bitcast(tab_ref[pl.ds(off, 8), :], BF16))
    return tiles


def _sublane_sums(ps):
    sub = lax.broadcasted_iota(jnp.int32, (8, 128), 0)
    lo4 = sub < 4
    t = [jnp.where(lo4, ps[j], ps[j + 4]) + pltpu.roll(jnp.where(lo4, ps[j + 4], ps[j]), 4, 0)
         for j in range(4)]
    m2 = (sub & 2) == 0
    u = [jnp.where(m2, t[j] + pltpu.roll(t[j], 6, 0), t[j + 2] + pltpu.roll(t[j + 2], 2, 0))
         for j in range(2)]
    m1 = (sub & 1) == 0
    return jnp.where(m1, u[0] + pltpu.roll(u[0], 7, 0), u[1] + pltpu.roll(u[1], 1, 0))


def _peer_u_kernel(off_ref, par_ref, x_ref, gate_ref, tab_ref, w_ref, tga_ref, tgb_ref, v3_ref):
    tb = x_ref.shape[0]
    par = par_ref[...]
    tga_ref[...] = _pair_targets(par,0)
    tgb_ref[...] = _pair_targets(par,1)
    m_iota = lax.broadcasted_iota(jnp.int32, (8, PAIR_K), 0).astype(F32)
    zero = jnp.zeros((8, 128), F32)
    nt = (((1,), (1,)), ((), ()))

    def body(t, carry):
        tiles = _gather_tiles(tab_ref, off_ref, t * PEER_PICKS)
        wmat = jnp.concatenate(
            [jnp.concatenate([tiles[2 * q], tiles[2 * q + 1]], axis=1) for q in range(PAIRS)], axis=0)
        xt = x_ref[t]
        xf = jnp.concatenate([jnp.concatenate([xt, zero], axis=1),
                              jnp.concatenate([zero, xt], axis=1)], axis=0)
        xh = xf.astype(BF16)
        xl = (xf - xh.astype(F32)).astype(BF16)
        o = lax.dot_general(jnp.concatenate([xh, xl], axis=0), wmat, nt, preferred_element_type=F32)
        o = o[0:16] + o[16:32]
        za = jnp.where(tga_ref[pl.ds(t, 1), :] == m_iota, o[0:8], 0.0)
        zb = jnp.where(tgb_ref[pl.ds(t, 1), :] == m_iota, o[8:16], 0.0)
        va = _sublane_sums([za[:, i * 128:(i + 1) * 128] for i in range(8)])
        vb = _sublane_sums([zb[:, i * 128:(i + 1) * 128] for i in range(8)])
        for i in range(8):
            v3_ref[i, pl.ds(t, 1), :] = va[i:i + 1]
            v3_ref[8 + i, pl.ds(t, 1), :] = vb[i:i + 1]
        return carry

    def group(i, carry):
        for s in range(PEER_UNROLL):
            body(i * PEER_UNROLL + s, carry)
        return carry

    lax.fori_loop(0, tb // PEER_UNROLL, group, 0)
    lane = lax.broadcasted_iota(jnp.int32, (128, PEER_PICKS), 0)
    pick = lax.broadcasted_iota(jnp.int32, (128, PEER_PICKS), 1)
    act = jnp.zeros((tb, PEER_PICKS), F32)
    for c in range(16):
        sel = jnp.where(pick == 2 * (8 * (c % 8) + lane // 16) + c // 8, 1.0, 0.0).astype(BF16)
        for p in _split3(v3_ref[c]):
            act = act + jnp.dot(p, sel, preferred_element_type=F32)
    gelu = 0.5 * act * (1.0 + lax.erf(act * (2.0 ** -0.5)))
    w_ref[...] = gate_ref[...] * gelu


def _smem_block(tb):
    return pl.BlockSpec((1, 1, tb * PEER_PICKS), lambda i: (i, 0, 0), memory_space=pltpu.SMEM)


def _peer_u(idx_blocks, idx, x3, gate, tab):
    T = x3.shape[0]
    tb = PEER_TB
    rows = pl.BlockSpec((tb, PEER_PICKS), lambda i: (i, 0))
    return pl.pallas_call(
        _peer_u_kernel,
        grid=(T // tb,),
        in_specs=[
            _smem_block(tb), rows,
            pl.BlockSpec((tb, 8, 128), lambda i: (i, 0, 0)),
            rows,
            pl.BlockSpec(tab.shape, lambda i: (0, 0), pipeline_mode=pl.Buffered(1)),
        ],
        out_specs=rows,
        out_shape=jax.ShapeDtypeStruct((T, PEER_PICKS), F32),
        scratch_shapes=[pltpu.VMEM((tb, PAIR_K), F32), pltpu.VMEM((tb, PAIR_K), F32),
                        pltpu.VMEM((16, tb, 128), F32)],
        compiler_params=_cparams(("parallel",)),
        name="peer_u",
    )(idx_blocks, idx, x3, gate, tab)


def _peer_v_kernel(off_ref, par_ref, w_ref, tab_ref, o_ref, wa_ref, wb_ref, tga_ref, tgb_ref):
    tb = o_ref.shape[0]
    par = par_ref[...]
    w = w_ref[...]
    wa_ref[...] = _pair_expand(w, 0)
    wb_ref[...] = _pair_expand(w, 1)
    tga_ref[...] = _pair_targets(par,0)
    tgb_ref[...] = _pair_targets(par,1)
    m_iota = lax.broadcasted_iota(jnp.int32, (8, PAIR_K), 0).astype(F32)

    def body(t, carry):
        tiles = _gather_tiles(tab_ref, off_ref, t * PEER_PICKS)
        vmat = jnp.concatenate([jnp.concatenate(tiles[0::2], axis=0),
                                jnp.concatenate(tiles[1::2], axis=0)], axis=1)
        row = pl.ds(t, 1)
        la = jnp.where(tga_ref[row, :] == m_iota, wa_ref[row, :], 0.0)
        lb = jnp.where(tgb_ref[row, :] == m_iota, wb_ref[row, :], 0.0)
        lf = jnp.concatenate([la, lb], axis=0)
        lh = lf.astype(BF16)
        ll = (lf - lh.astype(F32)).astype(BF16)
        o = jnp.dot(jnp.concatenate([lh, ll], axis=0), vmat, preferred_element_type=F32)
        o_ref[t] = (o[0:8, 0:128] + o[8:16, 128:256]) + (o[16:24, 0:128] + o[24:32, 128:256])
        return carry

    def group(i, carry):
        for s in range(PEER_UNROLL):
            body(i * PEER_UNROLL + s, carry)
        return carry

    lax.fori_loop(0, tb // PEER_UNROLL, group, 0)


def _peer_v(idx_blocks, idx, w, tab):
    T = idx.shape[0]
    tb = PEER_TB
    rows = pl.BlockSpec((tb, PEER_PICKS), lambda i: (i, 0))
    return pl.pallas_call(
        _peer_v_kernel,
        grid=(T // tb,),
        in_specs=[_smem_block(tb), rows, rows,
                  pl.BlockSpec(tab.shape, lambda i: (0, 0), pipeline_mode=pl.Buffered(1))],
        out_specs=pl.BlockSpec((tb, 8, 128), lambda i: (i, 0, 0)),
        out_shape=jax.ShapeDtypeStruct((T, 8, 128), F32),
        scratch_shapes=[pltpu.VMEM((tb, PAIR_K), F32) for _ in range(4)],
        compiler_params=_cparams(("parallel",)),
        name="peer_v",
    )(idx_blocks, idx, w, tab)


def _final_kernel(h_ref, p_ref, g_ref, o_ref):
    nt = D_MODEL // 128
    tm = h_ref.shape[0]
    p = jnp.concatenate([p_ref[pl.ds(c, tm, stride=nt), :] for c in range(nt)], axis=1)
    h = h_ref[...] + p
    o_ref[...] = h * lax.rsqrt(jnp.mean(h * h, axis=-1, keepdims=True) + EPS) * g_ref[...]


def _final(h, p, g):
    T = h.shape[0]
    row = pl.BlockSpec((FINAL_TM, D_MODEL), lambda i: (i, 0))
    return pl.pallas_call(
        _final_kernel,
        grid=(T // FINAL_TM,),
        in_specs=[row, pl.BlockSpec((FINAL_TM * (D_MODEL // 128), 128), lambda i: (i, 0)),
                  _const_spec((1, D_MODEL))],
        out_specs=row,
        out_shape=jax.ShapeDtypeStruct((T, D_MODEL), F32),
        compiler_params=_cparams(("parallel",)),
        name="final_norm",
    )(h, p, g)


def _layer(h2, B, S, norm1_g, w_in, w_ret_out, w_att_out, w_out, norm2_g, peer_wq, peer_subkeys, peer_u, peer_v):
    T = B * S
    aq, ak, av = 3072, 3072 + ATT_W, 3072 + 2 * ATT_W
    att_cols = [w_in[:, o + g * ATT_OUT_W:o + (g + 1) * ATT_OUT_W] for g in range(len(ATT_GROUPS)) for o in (aq, ak, av)]
    w_perm = jnp.concatenate([w_in[:, :3072], w_in[:, 7680:9728]] + att_cols, axis=1).astype(BF16)
    main, a0, a1, a2 = _proj(h2, norm1_g.reshape(1, D_MODEL), w_perm, B, S)
    retg = _retention(main.reshape(B, S, MAIN_W)).reshape(T, RET_V_W)
    (o1, l1), (o2, l2), (o3, l3) = (_att_group(a, g) for g, a in enumerate((a0.reshape(B, 1, S, ATT_SEC_W), a1, a2)))
    sk = peer_subkeys.reshape(2 * PEER_HEADS, PEER_NKEYS, PEER_DKEY // 2).astype(BF16)
    h_mid, xn2, scT = _mix(retg, o1, o2, o3, l1, l2, l3, main, h2,
                           w_ret_out.astype(BF16), w_att_out.astype(BF16), w_out.astype(BF16),
                           norm2_g.reshape(1, D_MODEL), peer_wq.astype(BF16), sk)
    off, par, gate = _topk(scT)
    off_blocks = off.reshape(T // PEER_TB, 1, PEER_TB * PEER_PICKS)
    w = _peer_u(off_blocks, par, xn2.reshape(T, 8, 128), gate, _pack_table(peer_u))
    pout = _peer_v(off_blocks, par, w, _pack_table(peer_v))
    return h_mid, pout.reshape(T * 8, 128)


def kernel(x, norm1_g, w_in, w_ret_out, w_att_out, w_out, norm2_g, peer_wq, peer_subkeys, peer_u, peer_v, normf_g):
    B, S, D = x.shape
    assert w_in.shape[0] == 1, "single-layer block"
    h, pout = _layer(x.reshape(B * S, D), B, S, norm1_g[0], w_in[0], w_ret_out[0], w_att_out[0], w_out[0],
                     norm2_g[0], peer_wq[0], peer_subkeys[0], peer_u[0], peer_v[0])
    return _final(h, pout, normf_g.reshape(1, D)).reshape(B, S, D)
```

```python
import functools
import math

import jax
import jax.numpy as jnp
from jax import lax
from jax.experimental import pallas as pl
from jax.experimental.pallas import tpu as pltpu

F32 = jnp.float32
BF16 = jnp.bfloat16

D_MODEL = 1024
RET_HEADS = 8
RET_DK = 64
RET_DV = 128
RET_CHUNK = 128
ATT_GROUPS = ((128, 1), (512, 4), (2048, 16))
ATT_HPG = 4
ATT_HEADS = 12
ATT_DH = 128
ATT_BLOCK = 128
PEER_HEADS = 8
PEER_NKEYS = 128
PEER_DKEY = 256
PEER_TOPK = 16
PEER_PICKS = PEER_HEADS * PEER_TOPK
EPS = 1e-6

RET_QK_W = RET_HEADS * RET_DK
RET_V_W = RET_HEADS * RET_DV
ATT_W = ATT_HEADS * ATT_DH
ATT_OUT_W = ATT_HPG * ATT_DH
OFF_RQ, OFF_RK, OFF_RV, OFF_RG = 0, 512, 1024, 2048
OFF_GR, OFF_GA = 3072, 4096
MAIN_W = 5120
ATT_SEC_W = 3 * ATT_OUT_W
IN_W = MAIN_W + len(ATT_GROUPS) * ATT_SEC_W

VMEM_LIMIT_BYTES = 60 * 1024 * 1024

PROJ_TM = 256
PROJ_TN = 512
MIX_TM = 256
TOPK_TB = 256
PEER_TB = 128
PEER_UNROLL = 8
FINAL_TM = 512


def _cparams(sem):
    return pltpu.CompilerParams(dimension_semantics=sem, vmem_limit_bytes=VMEM_LIMIT_BYTES)


def _const_spec(shape):
    nd = len(shape)
    return pl.BlockSpec(shape, lambda *_: (0,) * nd)


def _proj_kernel(x_ref, g_ref, w_ref, main_ref, a0_ref, a1_ref, a2_ref, scr_ref):
    x = x_ref[...]
    ms = jnp.mean(x * x, axis=-1, keepdims=True)
    xn = (x * lax.rsqrt(ms + EPS) * g_ref[...]).astype(BF16)

    def chunk(j):
        return jnp.dot(xn, w_ref[:, j * PROJ_TN:(j + 1) * PROJ_TN], preferred_element_type=F32)

    nmain = MAIN_W // PROJ_TN
    for j in range(nmain):
        main_ref[:, j * PROJ_TN:(j + 1) * PROJ_TN] = chunk(j).astype(BF16)
    for s in range(3):
        a0_ref[:, s * PROJ_TN:(s + 1) * PROJ_TN] = chunk(nmain + s).astype(BF16)
    for g, a_ref in ((1, a1_ref), (2, a2_ref)):
        d = ATT_GROUPS[g][1]
        for s in range(3):
            res = chunk(nmain + 3 * g + s)
            for c in range(PROJ_TN // 128):
                scr_ref[c] = res[:, c * 128:(c + 1) * 128]
            for r in range(d):
                piece = jnp.concatenate(
                    [scr_ref[c, pl.ds(r, PROJ_TM // d, stride=d), :] for c in range(PROJ_TN // 128)], axis=1)
                a_ref[0, r, :, s * PROJ_TN:(s + 1) * PROJ_TN] = piece.astype(BF16)


def _proj(x2, g, w, B, S):
    T = x2.shape[0]
    tiles = S // PROJ_TM
    d1, d2 = ATT_GROUPS[1][1], ATT_GROUPS[2][1]
    return pl.pallas_call(
        _proj_kernel,
        grid=(T // PROJ_TM,),
        in_specs=[
            pl.BlockSpec((PROJ_TM, D_MODEL), lambda i: (i, 0)),
            _const_spec((1, D_MODEL)),
            pl.BlockSpec((D_MODEL, IN_W), lambda i: (0, 0), pipeline_mode=pl.Buffered(1)),
        ],
        out_specs=[
            pl.BlockSpec((PROJ_TM, MAIN_W), lambda i: (i, 0)),
            pl.BlockSpec((PROJ_TM, ATT_SEC_W), lambda i: (i, 0)),
            pl.BlockSpec((1, d1, PROJ_TM // d1, ATT_SEC_W), lambda i: (i // tiles, 0, i % tiles, 0)),
            pl.BlockSpec((1, d2, PROJ_TM // d2, ATT_SEC_W), lambda i: (i // tiles, 0, i % tiles, 0)),
        ],
        out_shape=[
            jax.ShapeDtypeStruct((T, MAIN_W), BF16),
            jax.ShapeDtypeStruct((T, ATT_SEC_W), BF16),
            jax.ShapeDtypeStruct((B, d1, S // d1, ATT_SEC_W), BF16),
            jax.ShapeDtypeStruct((B, d2, S // d2, ATT_SEC_W), BF16),
        ],
        scratch_shapes=[pltpu.VMEM((PROJ_TN // 128, PROJ_TM, 128), F32)],
        compiler_params=_cparams(("parallel",)),
        name="proj",
    )(x2, g, w)


def _retention_kernel(q_ref, k_ref, v_ref, rg_ref, o_ref, state_ref):
    C = RET_CHUNK

    @pl.when(pl.program_id(1) == 0)
    def _():
        state_ref[...] = jnp.zeros_like(state_ref)

    pi = lax.broadcasted_iota(jnp.int32, (C, C), 0)
    pj = lax.broadcasted_iota(jnp.int32, (C, C), 1)
    diff = (pi - pj).astype(F32)
    causal = diff >= 0
    diffc = jnp.maximum(diff, 0.0)
    pos = lax.broadcasted_iota(jnp.int32, (C, 1), 0).astype(F32)
    q = q_ref[0]
    k = k_ref[0]
    v = v_ref[0]
    rg = rg_ref[0]
    for h in range(RET_HEADS):
        lg = math.log1p(-(2.0 ** (-5.0 - h)))
        decay = jnp.where(causal, jnp.exp(diffc * lg), 0.0)
        w_k = jnp.exp((C - 1 - pos) * lg)
        w_q = jnp.exp((pos + 1.0) * lg)
        qh = q[:, h * RET_DK:(h + 1) * RET_DK]
        kh = k[:, h * RET_DK:(h + 1) * RET_DK].astype(F32) * (RET_DK ** -0.5)
        vh = v[:, h * RET_DV:(h + 1) * RET_DV]
        s = lax.dot_general(qh, kh.astype(BF16), (((1,), (1,)), ((), ())), preferred_element_type=F32)
        p = (s * decay).astype(BF16)
        inner = jnp.dot(p, vh, preferred_element_type=F32)
        st = state_ref[h]
        cross = jnp.dot((qh.astype(F32) * w_q).astype(BF16), st.astype(BF16), preferred_element_type=F32)
        kw = (kh * w_k).astype(BF16)
        kv = lax.dot_general(kw, vh, (((0,), (0,)), ((), ())), preferred_element_type=F32)
        state_ref[h] = math.exp(C * lg) * st + kv
        ret = inner + cross
        rn = ret * lax.rsqrt(jnp.mean(ret * ret, axis=-1, keepdims=True) + EPS)
        g = rg[:, h * RET_DV:(h + 1) * RET_DV].astype(F32)
        o_ref[0, :, h * RET_DV:(h + 1) * RET_DV] = (g * jax.nn.sigmoid(g) * rn).astype(BF16)


def _retention(proj3):
    B, S, _ = proj3.shape
    n = S // RET_CHUNK
    return pl.pallas_call(
        _retention_kernel,
        grid=(B, n),
        in_specs=[
            pl.BlockSpec((1, RET_CHUNK, RET_QK_W), lambda b, c: (b, c, OFF_RQ // RET_QK_W)),
            pl.BlockSpec((1, RET_CHUNK, RET_QK_W), lambda b, c: (b, c, OFF_RK // RET_QK_W)),
            pl.BlockSpec((1, RET_CHUNK, RET_V_W), lambda b, c: (b, c, OFF_RV // RET_V_W)),
            pl.BlockSpec((1, RET_CHUNK, RET_V_W), lambda b, c: (b, c, OFF_RG // RET_V_W)),
        ],
        out_specs=pl.BlockSpec((1, RET_CHUNK, RET_V_W), lambda b, c: (b, c, 0)),
        out_shape=jax.ShapeDtypeStruct((B, S, RET_V_W), BF16),
        scratch_shapes=[pltpu.VMEM((RET_HEADS, RET_DK, RET_DV), F32)],
        compiler_params=_cparams(("parallel", "arbitrary")),
        name="retention",
    )(proj3, proj3, proj3, proj3)


def _att_kernel(q_ref, kp_ref, kc_ref, vp_ref, vc_ref, o_ref, lse_ref, *, group, dilation, span):
    n = pl.program_id(1)
    r = pl.program_id(2)
    iq = lax.broadcasted_iota(jnp.int32, (ATT_BLOCK, 2 * ATT_BLOCK), 0)
    jk = lax.broadcasted_iota(jnp.int32, (ATT_BLOCK, 2 * ATT_BLOCK), 1)
    dist = iq + ATT_BLOCK - jk
    valid = (dist >= 0) & (dist <= span) & ((jk >= ATT_BLOCK) | (n > 0))
    distf = (dilation * dist).astype(F32)
    q = q_ref[0, 0]
    kk = jnp.concatenate([kp_ref[0, 0], kc_ref[0, 0]], axis=0)
    vv = jnp.concatenate([vp_ref[0, 0], vc_ref[0, 0]], axis=0)
    rows = pl.ds(r, ATT_BLOCK, stride=dilation) if dilation > 1 else slice(None)
    for i in range(ATT_HPG):
        slope = 2.0 ** (-8.0 * (group * ATT_HPG + i + 1) / ATT_HEADS)
        sl = slice(i * ATT_DH, (i + 1) * ATT_DH)
        s = lax.dot_general(q[:, sl], kk[:, sl], (((1,), (1,)), ((), ())), preferred_element_type=F32)
        s = s * (ATT_DH ** -0.5)
        s = jnp.where(valid, s - slope * distf, -jnp.inf)
        m = jnp.max(s, axis=-1, keepdims=True)
        p = jnp.exp(s - m)
        den = jnp.sum(p, axis=-1, keepdims=True)
        o_ref[i, 0, rows, :] = jnp.dot(p.astype(BF16), vv[:, sl], preferred_element_type=F32) / den
        lse_ref[i, 0, rows, :] = jnp.broadcast_to(m + jnp.log(den), (ATT_BLOCK, ATT_DH))


def _att_group(qkv, group):
    window, d = ATT_GROUPS[group]
    B, _, L, _ = qkv.shape
    S = L * d
    nL = L // ATT_BLOCK

    def cur(sec):
        return pl.BlockSpec((1, 1, ATT_BLOCK, ATT_OUT_W), lambda b, n, r: (b, r, n, sec))

    def prev(sec):
        return pl.BlockSpec((1, 1, ATT_BLOCK, ATT_OUT_W), lambda b, n, r: (b, r, jnp.maximum(n - 1, 0), sec))

    out_spec = pl.BlockSpec((ATT_HPG, 1, ATT_BLOCK * d, ATT_DH), lambda b, n, r: (0, b, n, 0))
    out_shape = jax.ShapeDtypeStruct((ATT_HPG, B, S, ATT_DH), F32)
    o, lse = pl.pallas_call(
        functools.partial(_att_kernel, group=group, dilation=d, span=window // d),
        grid=(B, nL, d),
        in_specs=[cur(0), prev(1), cur(1), prev(2), cur(2)],
        out_specs=[out_spec, out_spec],
        out_shape=[out_shape, out_shape],
        compiler_params=_cparams(("parallel", "arbitrary", "arbitrary")),
        name=f"att_g{group}",
    )(qkv, qkv, qkv, qkv, qkv)
    return o.reshape(ATT_HPG, B * S, ATT_DH), lse.reshape(ATT_HPG, B * S, ATT_DH)


def _mix_kernel(retg_ref, o1_ref, o2_ref, o3_ref, l1_ref, l2_ref, l3_ref, gr_ref, ga_ref, x_ref,
                wro_ref, wao_ref, wo_ref, g2_ref, wq_ref, sk_ref, h_ref, xn_ref, sc_ref):
    heads = lambda ref: jnp.concatenate([ref[i] for i in range(ATT_HPG)], axis=1)
    l1, l2, l3 = heads(l1_ref), heads(l2_ref), heads(l3_ref)
    mx = jnp.maximum(jnp.maximum(l1, l2), l3)
    e1, e2, e3 = jnp.exp(l1 - mx), jnp.exp(l2 - mx), jnp.exp(l3 - mx)
    att = (e1 * heads(o1_ref) + e2 * heads(o2_ref) + e3 * heads(o3_ref)) / (e1 + e2 + e3)
    a_branch = jnp.dot(att.astype(BF16), wao_ref[...], preferred_element_type=F32)
    r_branch = jnp.dot(retg_ref[...], wro_ref[...], preferred_element_type=F32)
    merged = (jax.nn.sigmoid(gr_ref[...].astype(F32)) * r_branch
              + jax.nn.sigmoid(ga_ref[...].astype(F32)) * a_branch)
    h = x_ref[...] + jnp.dot(merged.astype(BF16), wo_ref[...], preferred_element_type=F32)
    h_ref[...] = h
    xn = h * lax.rsqrt(jnp.mean(h * h, axis=-1, keepdims=True) + EPS) * g2_ref[...]
    for c in range(D_MODEL // 128):
        xn_ref[pl.ds(c, h.shape[0], stride=D_MODEL // 128), :] = xn[:, c * 128:(c + 1) * 128]
    qp = jnp.dot(xn.astype(BF16), wq_ref[...], preferred_element_type=F32).astype(BF16)
    half = PEER_DKEY // 2
    for hp in range(2 * PEER_HEADS):
        sc_ref[hp] = lax.dot_general(sk_ref[hp], qp[:, hp * half:(hp + 1) * half],
                                     (((1,), (1,)), ((), ())), preferred_element_type=F32)


def _mix(retg, o1, o2, o3, l1, l2, l3, proj, x2, wro, wao, wo, g2, wq, sk):
    T = x2.shape[0]
    tm = MIX_TM
    row = lambda w: pl.BlockSpec((tm, w), lambda i: (i, 0))
    hd = pl.BlockSpec((ATT_HPG, tm, ATT_DH), lambda i: (0, i, 0))
    nhp = 2 * PEER_HEADS
    nt = D_MODEL // 128
    return pl.pallas_call(
        _mix_kernel,
        grid=(T // tm,),
        in_specs=[
            row(RET_V_W), hd, hd, hd, hd, hd, hd,
            pl.BlockSpec((tm, D_MODEL), lambda i: (i, OFF_GR // D_MODEL)),
            pl.BlockSpec((tm, D_MODEL), lambda i: (i, OFF_GA // D_MODEL)),
            row(D_MODEL),
            _const_spec(wro.shape), _const_spec(wao.shape), _const_spec(wo.shape),
            _const_spec(g2.shape), _const_spec(wq.shape), _const_spec(sk.shape),
        ],
        out_specs=[row(D_MODEL), pl.BlockSpec((tm * nt, 128), lambda i: (i, 0)),
                   pl.BlockSpec((nhp, PEER_NKEYS, tm), lambda i: (0, 0, i))],
        out_shape=[jax.ShapeDtypeStruct((T, D_MODEL), F32),
                   jax.ShapeDtypeStruct((T * nt, 128), F32),
                   jax.ShapeDtypeStruct((nhp, PEER_NKEYS, T), F32)],
        compiler_params=_cparams(("parallel",)),
        name="mix",
    )(retg, o1, o2, o3, l1, l2, l3, proj, proj, x2, wro, wao, wo, g2, wq, sk)


def _topk_rows(s, ids, k, fill):
    vals, idxs = [], []
    for _ in range(k):
        m = jnp.max(s, axis=0, keepdims=True)
        i = jnp.min(jnp.where(s == m, ids, fill), axis=0, keepdims=True)
        vals.append(m)
        idxs.append(i)
        s = jnp.where(ids == i, -jnp.inf, s)
    return jnp.concatenate(vals, axis=0), jnp.concatenate(idxs, axis=0)


_CAND_B = [PEER_TOPK // (a + 1) for a in range(PEER_TOPK)]
_CAND_ROWS = -(-sum(_CAND_B) // 8) * 8


def _topk_kernel(sc_ref, off_ref, par_ref, gate_ref, e_scr, g_scr):
    K = PEER_TOPK
    W = sc_ref.shape[-1]
    key_ids = lax.broadcasted_iota(jnp.int32, (PEER_NKEYS, W), 0)
    npad = _CAND_ROWS - sum(_CAND_B)
    cand_ids = jnp.concatenate(
        [jnp.full((1, W), a * K + b, jnp.int32) for a in range(K) for b in range(_CAND_B[a])]
        + [jnp.full((npad, W), K * K, jnp.int32)], axis=0)
    pad = jnp.full((npad, W), -jnp.inf, F32)

    def head(h, carry):
        v0, i0 = _topk_rows(sc_ref[2 * h], key_ids, K, PEER_NKEYS)
        v1, i1 = _topk_rows(sc_ref[2 * h + 1], key_ids, K, PEER_NKEYS)
        cand = jnp.concatenate([v0[a:a + 1] + v1[0:_CAND_B[a]] for a in range(K)] + [pad], axis=0)
        best_s, best_c = _topk_rows(cand, cand_ids, K, K * K)
        ca = best_c // K
        cb = best_c % K
        ia = jnp.zeros_like(best_c)
        ib = jnp.zeros_like(best_c)
        for a in range(K):
            ia = jnp.where(ca == a, i0[a:a + 1], ia)
            ib = jnp.where(cb == a, i1[a:a + 1], ib)
        e_scr[h] = (ia * PEER_NKEYS + ib).astype(F32)
        ex = jnp.exp(best_s - best_s[0:1])
        g_scr[h] = ex / jnp.sum(ex, axis=0, keepdims=True)
        return carry

    lax.fori_loop(0, PEER_HEADS, head, 0)
    e = e_scr[...].reshape(PEER_PICKS, W).T.astype(jnp.int32)
    off_ref[...] = lax.shift_right_logical(e, 1) * 8
    par_ref[...] = (e & 1).astype(F32)
    gate_ref[...] = g_scr[...].reshape(PEER_PICKS, W).T


def _topk(scT):
    T = scT.shape[-1]
    tb = TOPK_TB
    out_spec = pl.BlockSpec((tb, PEER_PICKS), lambda i: (i, 0))
    return pl.pallas_call(
        _topk_kernel,
        grid=(T // tb,),
        in_specs=[pl.BlockSpec((2 * PEER_HEADS, PEER_NKEYS, tb), lambda i: (0, 0, i))],
        out_specs=[out_spec, out_spec, out_spec],
        out_shape=[jax.ShapeDtypeStruct((T, PEER_PICKS), jnp.int32),
                   jax.ShapeDtypeStruct((T, PEER_PICKS), F32),
                   jax.ShapeDtypeStruct((T, PEER_PICKS), F32)],
        scratch_shapes=[pltpu.VMEM((PEER_HEADS, PEER_TOPK, tb), F32),
                        pltpu.VMEM((PEER_HEADS, PEER_TOPK, tb), F32)],
        compiler_params=_cparams(("parallel",)),
        name="topk",
    )(scT)


PEER_U_VALU = 80
PEER_V_VALU = 48


def _pack_table(tab):
    E = tab.shape[0]
    b = lax.bitcast_convert_type(tab.astype(BF16), jnp.uint16).astype(jnp.uint32).reshape(E // 2, 2, 8, 128)
    return (b[:, 0] | (b[:, 1] << 16)).reshape(4 * E, 128)


def _split3(a):
    p0 = a.astype(BF16)
    r1 = a - p0.astype(F32)
    p1 = r1.astype(BF16)
    p2 = (r1 - p1.astype(F32)).astype(BF16)
    return p0, p1, p2


def _pair_expand(v, which, base, pairs):
    j = lax.broadcasted_iota(jnp.int32, (PEER_PICKS, 16 * pairs), 0)
    k = lax.broadcasted_iota(jnp.int32, (PEER_PICKS, 16 * pairs), 1)
    onehot = jnp.where(j == base + 2 * (k // 16) + which, 1.0, 0.0).astype(BF16)
    return sum(jnp.dot(p, onehot, preferred_element_type=F32) for p in _split3(v))


def _pair_targets(par, which, base, pairs):
    r = lax.broadcasted_iota(jnp.int32, (par.shape[0], 16 * pairs), 1) % 16
    return 0.5 * (r.astype(F32) - _pair_expand(par, which, base, pairs))


def _tile_words(tab_ref, off_ref, t, j):
    off = pl.multiple_of(off_ref[t, j], 8)
    return tab_ref[pl.ds(off, 8), :]


def _fill_lane_broadcasts(dst_ref, src, n):
    for j in range(n):
        dst_ref[j] = jnp.broadcast_to(src[:, j:j + 1], dst_ref.shape[1:])


def _expert_rows(words, shv):
    return pltpu.bitcast((words << shv) & jnp.uint32(0xFFFF0000), F32)


def _sublane_sums(ps):
    sub = lax.broadcasted_iota(jnp.int32, (8, 128), 0)
    lo4 = sub < 4
    t = [jnp.where(lo4, ps[j], ps[j + 4]) + pltpu.roll(jnp.where(lo4, ps[j + 4], ps[j]), 4, 0)
         for j in range(4)]
    m2 = (sub & 2) == 0
    u = [jnp.where(m2, t[j] + pltpu.roll(t[j], 6, 0), t[j + 2] + pltpu.roll(t[j + 2], 2, 0))
         for j in range(2)]
    m1 = (sub & 1) == 0
    return jnp.where(m1, u[0] + pltpu.roll(u[0], 7, 0), u[1] + pltpu.roll(u[1], 1, 0))


def _token_loop(tb, body):
    def group(i, carry):
        for s in range(PEER_UNROLL):
            body(i * PEER_UNROLL + s)
        return carry

    lax.fori_loop(0, tb // PEER_UNROLL, group, 0)


def _peer_u_kernel(off_ref, par_ref, x_ref, gate_ref, tab_ref, w_ref, tga_ref, tgb_ref, v3_ref, shb_ref, actv_ref):
    tb = x_ref.shape[0]
    nv = PEER_U_VALU
    pairs = (PEER_PICKS - nv) // 2
    nblk = 16 * pairs // 128
    par = par_ref[...]
    tga_ref[...] = _pair_targets(par, 0, nv, pairs)
    tgb_ref[...] = _pair_targets(par, 1, nv, pairs)
    _fill_lane_broadcasts(shb_ref, ((1.0 - par) * 16.0).astype(jnp.uint32), nv)
    m_iota = lax.broadcasted_iota(jnp.int32, (8, 16 * pairs), 0).astype(F32)
    zero = jnp.zeros((8, 128), F32)
    ones = jnp.ones((8, 128), BF16)
    nt = (((1,), (1,)), ((), ()))

    def body(t):
        xt = x_ref[t]
        row = pl.ds(t, 1)
        rs = []
        for g in range(nv // 8):
            ps = []
            for jj in range(8):
                j = g * 8 + jj
                shv = jnp.broadcast_to(shb_ref[j, row, :], (8, 128))
                ps.append(_expert_rows(_tile_words(tab_ref, off_ref, t, j), shv) * xt)
            rs.append(_sublane_sums(ps))
        r = jnp.concatenate(rs, axis=0)
        hi = r.astype(BF16)
        lo = (r - hi.astype(F32)).astype(BF16)
        av = (lax.dot_general(ones, hi, nt, preferred_element_type=F32)
              + lax.dot_general(ones, lo, nt, preferred_element_type=F32))
        actv_ref[row, 0:nv] = av[0:1]
        tiles = [pltpu.bitcast(_tile_words(tab_ref, off_ref, t, j), BF16) for j in range(nv, PEER_PICKS)]
        wmat = jnp.concatenate(
            [jnp.concatenate([tiles[2 * q], tiles[2 * q + 1]], axis=1) for q in range(pairs)], axis=0)
        xf = jnp.concatenate([jnp.concatenate([xt, zero], axis=1),
                              jnp.concatenate([zero, xt], axis=1)], axis=0)
        xh = xf.astype(BF16)
        xl = (xf - xh.astype(F32)).astype(BF16)
        o = lax.dot_general(jnp.concatenate([xh, xl], axis=0), wmat, nt, preferred_element_type=F32)
        o = o[0:16] + o[16:32]
        za = jnp.where(tga_ref[row, :] == m_iota, o[0:8], 0.0)
        zb = jnp.where(tgb_ref[row, :] == m_iota, o[8:16], 0.0)
        blocks = ([za[:, i * 128:(i + 1) * 128] for i in range(nblk)]
                  + [zb[:, i * 128:(i + 1) * 128] for i in range(nblk)])
        v = _sublane_sums(blocks + [zero] * (8 - 2 * nblk))
        for i in range(2 * nblk):
            v3_ref[i, row, :] = v[i:i + 1]

    _token_loop(tb, body)
    lane = lax.broadcasted_iota(jnp.int32, (128, PEER_PICKS), 0)
    pick = lax.broadcasted_iota(jnp.int32, (128, PEER_PICKS), 1)
    act = jnp.zeros((tb, PEER_PICKS), F32)
    for c in range(2 * nblk):
        sel = jnp.where(pick == nv + 2 * (8 * (c % nblk) + lane // 16) + c // nblk, 1.0, 0.0).astype(BF16)
        for p in _split3(v3_ref[c]):
            act = act + jnp.dot(p, sel, preferred_element_type=F32)
    act = jnp.where(pick[0:1] < nv, actv_ref[...], act)
    gelu = 0.5 * act * (1.0 + lax.erf(act * (2.0 ** -0.5)))
    w_ref[...] = gate_ref[...] * gelu


def _peer_specs(tb, tab):
    rows = pl.BlockSpec((tb, PEER_PICKS), lambda i: (i, 0))
    smem = pl.BlockSpec((tb, PEER_PICKS), lambda i: (i, 0), memory_space=pltpu.SMEM)
    table = pl.BlockSpec(tab.shape, lambda i: (0, 0), pipeline_mode=pl.Buffered(1))
    return rows, smem, table


def _peer_u(off, par, x3, gate, tab):
    T = x3.shape[0]
    tb = PEER_TB
    rows, smem, table = _peer_specs(tb, tab)
    pairs = (PEER_PICKS - PEER_U_VALU) // 2
    return pl.pallas_call(
        _peer_u_kernel,
        grid=(T // tb,),
        in_specs=[smem, rows, pl.BlockSpec((tb, 8, 128), lambda i: (i, 0, 0)), rows, table],
        out_specs=rows,
        out_shape=jax.ShapeDtypeStruct((T, PEER_PICKS), F32),
        scratch_shapes=[pltpu.VMEM((tb, 16 * pairs), F32), pltpu.VMEM((tb, 16 * pairs), F32),
                        pltpu.VMEM((pairs // 4, tb, 128), F32),
                        pltpu.VMEM((PEER_U_VALU, tb, 128), jnp.uint32),
                        pltpu.VMEM((tb, PEER_PICKS), F32)],
        compiler_params=_cparams(("parallel",)),
        name="peer_u",
    )(off, par, x3, gate, tab)


def _peer_v_kernel(off_ref, par_ref, w_ref, tab_ref, o_ref, wa_ref, wb_ref, tga_ref, tgb_ref, shb_ref, wbc_ref):
    tb = o_ref.shape[0]
    nv = PEER_V_VALU
    pairs = (PEER_PICKS - nv) // 2
    par = par_ref[...]
    w = w_ref[...]
    wa_ref[...] = _pair_expand(w, 0, nv, pairs)
    wb_ref[...] = _pair_expand(w, 1, nv, pairs)
    tga_ref[...] = _pair_targets(par, 0, nv, pairs)
    tgb_ref[...] = _pair_targets(par, 1, nv, pairs)
    _fill_lane_broadcasts(shb_ref, ((1.0 - par) * 16.0).astype(jnp.uint32), nv)
    _fill_lane_broadcasts(wbc_ref, w, nv)
    m_iota = lax.broadcasted_iota(jnp.int32, (8, 16 * pairs), 0).astype(F32)
    nacc = 4

    def body(t):
        row = pl.ds(t, 1)
        acc = [jnp.zeros((8, 128), F32) for _ in range(nacc)]
        for j in range(nv):
            shv = jnp.broadcast_to(shb_ref[j, row, :], (8, 128))
            wv = jnp.broadcast_to(wbc_ref[j, row, :], (8, 128))
            acc[j % nacc] = acc[j % nacc] + wv * _expert_rows(_tile_words(tab_ref, off_ref, t, j), shv)
        tiles = [pltpu.bitcast(_tile_words(tab_ref, off_ref, t, j), BF16) for j in range(nv, PEER_PICKS)]
        vmat = jnp.concatenate([jnp.concatenate(tiles[0::2], axis=0),
                                jnp.concatenate(tiles[1::2], axis=0)], axis=1)
        la = jnp.where(tga_ref[row, :] == m_iota, wa_ref[row, :], 0.0)
        lb = jnp.where(tgb_ref[row, :] == m_iota, wb_ref[row, :], 0.0)
        lf = jnp.concatenate([la, lb], axis=0)
        lh = lf.astype(BF16)
        ll = (lf - lh.astype(F32)).astype(BF16)
        o = jnp.dot(jnp.concatenate([lh, ll], axis=0), vmat, preferred_element_type=F32)
        o_ref[t] = (((o[0:8, 0:128] + o[8:16, 128:256]) + (o[16:24, 0:128] + o[24:32, 128:256]))
                    + ((acc[0] + acc[1]) + (acc[2] + acc[3])))

    _token_loop(tb, body)


def _peer_v(off, par, w, tab):
    T = par.shape[0]
    tb = PEER_TB
    rows, smem, table = _peer_specs(tb, tab)
    pairs = (PEER_PICKS - PEER_V_VALU) // 2
    return pl.pallas_call(
        _peer_v_kernel,
        grid=(T // tb,),
        in_specs=[smem, rows, rows, table],
        out_specs=pl.BlockSpec((tb, 8, 128), lambda i: (i, 0, 0)),
        out_shape=jax.ShapeDtypeStruct((T, 8, 128), F32),
        scratch_shapes=[pltpu.VMEM((tb, 16 * pairs), F32) for _ in range(4)]
        + [pltpu.VMEM((PEER_V_VALU, tb, 128), jnp.uint32), pltpu.VMEM((PEER_V_VALU, tb, 128), F32)],
        compiler_params=_cparams(("parallel",)),
        name="peer_v",
    )(off, par, w, tab)


def _final_kernel(h_ref, p_ref, g_ref, o_ref):
    nt = D_MODEL // 128
    tm = h_ref.shape[0]
    p = jnp.concatenate([p_ref[pl.ds(c, tm, stride=nt), :] for c in range(nt)], axis=1)
    h = h_ref[...] + p
    o_ref[...] = h * lax.rsqrt(jnp.mean(h * h, axis=-1, keepdims=True) + EPS) * g_ref[...]


def _final(h, p, g):
    T = h.shape[0]
    row = pl.BlockSpec((FINAL_TM, D_MODEL), lambda i: (i, 0))
    return pl.pallas_call(
        _final_kernel,
        grid=(T // FINAL_TM,),
        in_specs=[row, pl.BlockSpec((FINAL_TM * (D_MODEL // 128), 128), lambda i: (i, 0)),
                  _const_spec((1, D_MODEL))],
        out_specs=row,
        out_shape=jax.ShapeDtypeStruct((T, D_MODEL), F32),
        compiler_params=_cparams(("parallel",)),
        name="final_norm",
    )(h, p, g)


def _layer(h2, B, S, norm1_g, w_in, w_ret_out, w_att_out, w_out, norm2_g, peer_wq, peer_subkeys, peer_u, peer_v):
    T = B * S
    aq, ak, av = 3072, 3072 + ATT_W, 3072 + 2 * ATT_W
    att_cols = [w_in[:, o + g * ATT_OUT_W:o + (g + 1) * ATT_OUT_W] for g in range(len(ATT_GROUPS)) for o in (aq, ak, av)]
    w_perm = jnp.concatenate([w_in[:, :3072], w_in[:, 7680:9728]] + att_cols, axis=1).astype(BF16)
    main, a0, a1, a2 = _proj(h2, norm1_g.reshape(1, D_MODEL), w_perm, B, S)
    retg = _retention(main.reshape(B, S, MAIN_W)).reshape(T, RET_V_W)
    (o1, l1), (o2, l2), (o3, l3) = (_att_group(a, g) for g, a in enumerate((a0.reshape(B, 1, S, ATT_SEC_W), a1, a2)))
    sk = peer_subkeys.reshape(2 * PEER_HEADS, PEER_NKEYS, PEER_DKEY // 2).astype(BF16)
    h_mid, xn2, scT = _mix(retg, o1, o2, o3, l1, l2, l3, main, h2,
                           w_ret_out.astype(BF16), w_att_out.astype(BF16), w_out.astype(BF16),
                           norm2_g.reshape(1, D_MODEL), peer_wq.astype(BF16), sk)
    off, par, gate = _topk(scT)
    w = _peer_u(off, par, xn2.reshape(T, 8, 128), gate, _pack_table(peer_u))
    pout = _peer_v(off, par, w, _pack_table(peer_v))
    return h_mid, pout.reshape(T * 8, 128)


def kernel(x, norm1_g, w_in, w_ret_out, w_att_out, w_out, norm2_g, peer_wq, peer_subkeys, peer_u, peer_v, normf_g):
    B, S, D = x.shape
    assert w_in.shape[0] == 1, "single-layer block"
    h, pout = _layer(x.reshape(B * S, D), B, S, norm1_g[0], w_in[0], w_ret_out[0], w_att_out[0], w_out[0],
                     norm2_g[0], peer_wq[0], peer_subkeys[0], peer_u[0], peer_v[0])
    return _final(h, pout, normf_g.reshape(1, D)).reshape(B, S, D)
```

```python
import functools
import math

import jax
import jax.numpy as jnp
from jax import lax
from jax.experimental import pallas as pl
from jax.experimental.pallas import tpu as pltpu

F32 = jnp.float32
BF16 = jnp.bfloat16

D_MODEL = 1024
RET_HEADS = 8
RET_DK = 64
RET_DV = 128
RET_CHUNK = 128
ATT_GROUPS = ((128, 1), (512, 4), (2048, 16))
ATT_HPG = 4
ATT_HEADS = 12
ATT_DH = 128
ATT_BLOCK = 128
ATT_BATCH = 2
PEER_HEADS = 8
PEER_NKEYS = 128
PEER_DKEY = 256
PEER_TOPK = 16
PEER_PICKS = PEER_HEADS * PEER_TOPK
EPS = 1e-6

RET_QK_W = RET_HEADS * RET_DK
RET_V_W = RET_HEADS * RET_DV
ATT_W = ATT_HEADS * ATT_DH
ATT_OUT_W = ATT_HPG * ATT_DH
OFF_RQ, OFF_RK, OFF_RV, OFF_RG = 0, 512, 1024, 2048
OFF_GR, OFF_GA = 3072, 4096
MAIN_W = 5120
ATT_SEC_W = 3 * ATT_OUT_W
IN_W = MAIN_W + len(ATT_GROUPS) * ATT_SEC_W

VMEM_LIMIT_BYTES = 60 * 1024 * 1024

PROJ_TM = 256
PROJ_TN = 512
MIX_TM = 256
TOPK_TB = 256
PEER_TB = 128
PEER_UNROLL = 8
FINAL_TM = 512


def _cparams(sem):
    return pltpu.CompilerParams(dimension_semantics=sem, vmem_limit_bytes=VMEM_LIMIT_BYTES)


def _const_spec(shape):
    nd = len(shape)
    return pl.BlockSpec(shape, lambda *_: (0,) * nd)


def _proj_kernel(x_ref, g_ref, w_ref, main_ref, a0_ref, a1_ref, a2_ref, scr_ref):
    x = x_ref[...]
    ms = jnp.mean(x * x, axis=-1, keepdims=True)
    xn = (x * lax.rsqrt(ms + EPS) * g_ref[...]).astype(BF16)

    def chunk(j):
        return jnp.dot(xn, w_ref[:, j * PROJ_TN:(j + 1) * PROJ_TN], preferred_element_type=F32)

    nmain = MAIN_W // PROJ_TN
    for j in range(nmain):
        main_ref[:, j * PROJ_TN:(j + 1) * PROJ_TN] = chunk(j).astype(BF16)
    for s in range(3):
        a0_ref[:, s * PROJ_TN:(s + 1) * PROJ_TN] = chunk(nmain + s).astype(BF16)
    for g, a_ref in ((1, a1_ref), (2, a2_ref)):
        d = ATT_GROUPS[g][1]
        for s in range(3):
            res = chunk(nmain + 3 * g + s)
            for c in range(PROJ_TN // 128):
                scr_ref[c] = res[:, c * 128:(c + 1) * 128]
            for r in range(d):
                piece = jnp.concatenate(
                    [scr_ref[c, pl.ds(r, PROJ_TM // d, stride=d), :] for c in range(PROJ_TN // 128)], axis=1)
                a_ref[0, r, :, s * PROJ_TN:(s + 1) * PROJ_TN] = piece.astype(BF16)


def _proj(x2, g, w, B, S):
    T = x2.shape[0]
    tiles = S // PROJ_TM
    d1, d2 = ATT_GROUPS[1][1], ATT_GROUPS[2][1]
    return pl.pallas_call(
        _proj_kernel,
        grid=(T // PROJ_TM,),
        in_specs=[
            pl.BlockSpec((PROJ_TM, D_MODEL), lambda i: (i, 0)),
            _const_spec((1, D_MODEL)),
            pl.BlockSpec((D_MODEL, IN_W), lambda i: (0, 0), pipeline_mode=pl.Buffered(1)),
        ],
        out_specs=[
            pl.BlockSpec((PROJ_TM, MAIN_W), lambda i: (i, 0)),
            pl.BlockSpec((PROJ_TM, ATT_SEC_W), lambda i: (i, 0)),
            pl.BlockSpec((1, d1, PROJ_TM // d1, ATT_SEC_W), lambda i: (i // tiles, 0, i % tiles, 0)),
            pl.BlockSpec((1, d2, PROJ_TM // d2, ATT_SEC_W), lambda i: (i // tiles, 0, i % tiles, 0)),
        ],
        out_shape=[
            jax.ShapeDtypeStruct((T, MAIN_W), BF16),
            jax.ShapeDtypeStruct((T, ATT_SEC_W), BF16),
            jax.ShapeDtypeStruct((B, d1, S // d1, ATT_SEC_W), BF16),
            jax.ShapeDtypeStruct((B, d2, S // d2, ATT_SEC_W), BF16),
        ],
        scratch_shapes=[pltpu.VMEM((PROJ_TN // 128, PROJ_TM, 128), F32)],
        compiler_params=_cparams(("parallel",)),
        name="proj",
    )(x2, g, w)


def _retention_kernel(q_ref, k_ref, v_ref, rg_ref, o_ref, state_ref):
    C = RET_CHUNK

    @pl.when(pl.program_id(1) == 0)
    def _():
        state_ref[...] = jnp.zeros_like(state_ref)

    pi = lax.broadcasted_iota(jnp.int32, (C, C), 0)
    pj = lax.broadcasted_iota(jnp.int32, (C, C), 1)
    diff = (pi - pj).astype(F32)
    causal = diff >= 0
    diffc = jnp.maximum(diff, 0.0)
    pos = lax.broadcasted_iota(jnp.int32, (C, 1), 0).astype(F32)
    q = q_ref[0]
    k = k_ref[0]
    v = v_ref[0]
    rg = rg_ref[0]
    for h in range(RET_HEADS):
        lg = math.log1p(-(2.0 ** (-5.0 - h)))
        decay = jnp.where(causal, jnp.exp(diffc * lg), 0.0)
        w_k = jnp.exp((C - 1 - pos) * lg)
        w_q = jnp.exp((pos + 1.0) * lg)
        qh = q[:, h * RET_DK:(h + 1) * RET_DK]
        kh = k[:, h * RET_DK:(h + 1) * RET_DK].astype(F32) * (RET_DK ** -0.5)
        vh = v[:, h * RET_DV:(h + 1) * RET_DV]
        s = lax.dot_general(qh, kh.astype(BF16), (((1,), (1,)), ((), ())), preferred_element_type=F32)
        p = (s * decay).astype(BF16)
        inner = jnp.dot(p, vh, preferred_element_type=F32)
        st = state_ref[h]
        cross = jnp.dot((qh.astype(F32) * w_q).astype(BF16), st.astype(BF16), preferred_element_type=F32)
        kw = (kh * w_k).astype(BF16)
        kv = lax.dot_general(kw, vh, (((0,), (0,)), ((), ())), preferred_element_type=F32)
        state_ref[h] = math.exp(C * lg) * st + kv
        ret = inner + cross
        rn = ret * lax.rsqrt(jnp.mean(ret * ret, axis=-1, keepdims=True) + EPS)
        g = rg[:, h * RET_DV:(h + 1) * RET_DV].astype(F32)
        o_ref[0, :, h * RET_DV:(h + 1) * RET_DV] = (g * jax.nn.sigmoid(g) * rn).astype(BF16)


def _retention(proj3):
    B, S, _ = proj3.shape
    n = S // RET_CHUNK
    return pl.pallas_call(
        _retention_kernel,
        grid=(B, n),
        in_specs=[
            pl.BlockSpec((1, RET_CHUNK, RET_QK_W), lambda b, c: (b, c, OFF_RQ // RET_QK_W)),
            pl.BlockSpec((1, RET_CHUNK, RET_QK_W), lambda b, c: (b, c, OFF_RK // RET_QK_W)),
            pl.BlockSpec((1, RET_CHUNK, RET_V_W), lambda b, c: (b, c, OFF_RV // RET_V_W)),
            pl.BlockSpec((1, RET_CHUNK, RET_V_W), lambda b, c: (b, c, OFF_RG // RET_V_W)),
        ],
        out_specs=pl.BlockSpec((1, RET_CHUNK, RET_V_W), lambda b, c: (b, c, 0)),
        out_shape=jax.ShapeDtypeStruct((B, S, RET_V_W), BF16),
        scratch_shapes=[pltpu.VMEM((RET_HEADS, RET_DK, RET_DV), F32)],
        compiler_params=_cparams(("parallel", "arbitrary")),
        name="retention",
    )(proj3, proj3, proj3, proj3)


def _att_kernel(q_ref, kp_ref, kc_ref, vp_ref, vc_ref, o_ref, lse_ref, *, group, dilation, span):
    n = pl.program_id(1)
    r = pl.program_id(2)
    iq = lax.broadcasted_iota(jnp.int32, (ATT_BLOCK, 2 * ATT_BLOCK), 0)
    jk = lax.broadcasted_iota(jnp.int32, (ATT_BLOCK, 2 * ATT_BLOCK), 1)
    dist = iq + ATT_BLOCK - jk
    valid = (dist >= 0) & (dist <= span) & ((jk >= ATT_BLOCK) | (n > 0))
    distf = (dilation * dist).astype(F32)
    rows = pl.ds(r, ATT_BLOCK, stride=dilation) if dilation > 1 else slice(None)
    for bb in range(ATT_BATCH):
        q = q_ref[bb, 0]
        kk = jnp.concatenate([kp_ref[bb, 0], kc_ref[bb, 0]], axis=0)
        vv = jnp.concatenate([vp_ref[bb, 0], vc_ref[bb, 0]], axis=0)
        for i in range(ATT_HPG):
            slope = 2.0 ** (-8.0 * (group * ATT_HPG + i + 1) / ATT_HEADS)
            sl = slice(i * ATT_DH, (i + 1) * ATT_DH)
            s = lax.dot_general(q[:, sl], kk[:, sl], (((1,), (1,)), ((), ())), preferred_element_type=F32)
            s = s * (ATT_DH ** -0.5)
            s = jnp.where(valid, s - slope * distf, -jnp.inf)
            m = jnp.max(s, axis=-1, keepdims=True)
            p = jnp.exp(s - m)
            den = jnp.sum(p, axis=-1, keepdims=True)
            o_ref[i, bb, rows, :] = jnp.dot(p.astype(BF16), vv[:, sl], preferred_element_type=F32) / den
            lse_ref[i, bb, rows, :] = jnp.broadcast_to(m + jnp.log(den), (ATT_BLOCK, ATT_DH))


def _att_group(qkv, group):
    window, d = ATT_GROUPS[group]
    B, _, L, _ = qkv.shape
    S = L * d
    nL = L // ATT_BLOCK

    def cur(sec):
        return pl.BlockSpec((ATT_BATCH, 1, ATT_BLOCK, ATT_OUT_W), lambda b, n, r: (b, r, n, sec))

    def prev(sec):
        return pl.BlockSpec((ATT_BATCH, 1, ATT_BLOCK, ATT_OUT_W), lambda b, n, r: (b, r, jnp.maximum(n - 1, 0), sec))

    out_spec = pl.BlockSpec((ATT_HPG, ATT_BATCH, ATT_BLOCK * d, ATT_DH), lambda b, n, r: (0, b, n, 0))
    out_shape = jax.ShapeDtypeStruct((ATT_HPG, B, S, ATT_DH), F32)
    o, lse = pl.pallas_call(
        functools.partial(_att_kernel, group=group, dilation=d, span=window // d),
        grid=(B // ATT_BATCH, nL, d),
        in_specs=[cur(0), prev(1), cur(1), prev(2), cur(2)],
        out_specs=[out_spec, out_spec],
        out_shape=[out_shape, out_shape],
        compiler_params=_cparams(("parallel", "arbitrary", "arbitrary")),
        name=f"att_g{group}",
    )(qkv, qkv, qkv, qkv, qkv)
    return o.reshape(ATT_HPG, B * S, ATT_DH), lse.reshape(ATT_HPG, B * S, ATT_DH)


def _mix_kernel(retg_ref, o1_ref, o2_ref, o3_ref, l1_ref, l2_ref, l3_ref, gr_ref, ga_ref, x_ref,
                wro_ref, wao_ref, wo_ref, g2_ref, wq_ref, sk_ref, h_ref, xn_ref, sc_ref):
    heads = lambda ref: jnp.concatenate([ref[i] for i in range(ATT_HPG)], axis=1)
    l1, l2, l3 = heads(l1_ref), heads(l2_ref), heads(l3_ref)
    mx = jnp.maximum(jnp.maximum(l1, l2), l3)
    e1, e2, e3 = jnp.exp(l1 - mx), jnp.exp(l2 - mx), jnp.exp(l3 - mx)
    att = (e1 * heads(o1_ref) + e2 * heads(o2_ref) + e3 * heads(o3_ref)) / (e1 + e2 + e3)
    a_branch = jnp.dot(att.astype(BF16), wao_ref[...], preferred_element_type=F32)
    r_branch = jnp.dot(retg_ref[...], wro_ref[...], preferred_element_type=F32)
    merged = (jax.nn.sigmoid(gr_ref[...].astype(F32)) * r_branch
              + jax.nn.sigmoid(ga_ref[...].astype(F32)) * a_branch)
    h = x_ref[...] + jnp.dot(merged.astype(BF16), wo_ref[...], preferred_element_type=F32)
    h_ref[...] = h
    xn = h * lax.rsqrt(jnp.mean(h * h, axis=-1, keepdims=True) + EPS) * g2_ref[...]
    for c in range(D_MODEL // 128):
        xn_ref[pl.ds(c, h.shape[0], stride=D_MODEL // 128), :] = xn[:, c * 128:(c + 1) * 128]
    qp = jnp.dot(xn.astype(BF16), wq_ref[...], preferred_element_type=F32).astype(BF16)
    half = PEER_DKEY // 2
    for hp in range(2 * PEER_HEADS):
        sc_ref[hp] = lax.dot_general(sk_ref[hp], qp[:, hp * half:(hp + 1) * half],
                                     (((1,), (1,)), ((), ())), preferred_element_type=F32)


def _mix(retg, o1, o2, o3, l1, l2, l3, proj, x2, wro, wao, wo, g2, wq, sk):
    T = x2.shape[0]
    tm = MIX_TM
    row = lambda w: pl.BlockSpec((tm, w), lambda i: (i, 0))
    hd = pl.BlockSpec((ATT_HPG, tm, ATT_DH), lambda i: (0, i, 0))
    nhp = 2 * PEER_HEADS
    nt = D_MODEL // 128
    return pl.pallas_call(
        _mix_kernel,
        grid=(T // tm,),
        in_specs=[
            row(RET_V_W), hd, hd, hd, hd, hd, hd,
            pl.BlockSpec((tm, D_MODEL), lambda i: (i, OFF_GR // D_MODEL)),
            pl.BlockSpec((tm, D_MODEL), lambda i: (i, OFF_GA // D_MODEL)),
            row(D_MODEL),
            _const_spec(wro.shape), _const_spec(wao.shape), _const_spec(wo.shape),
            _const_spec(g2.shape), _const_spec(wq.shape), _const_spec(sk.shape),
        ],
        out_specs=[row(D_MODEL), pl.BlockSpec((tm * nt, 128), lambda i: (i, 0)),
                   pl.BlockSpec((nhp, PEER_NKEYS, tm), lambda i: (0, 0, i))],
        out_shape=[jax.ShapeDtypeStruct((T, D_MODEL), F32),
                   jax.ShapeDtypeStruct((T * nt, 128), F32),
                   jax.ShapeDtypeStruct((nhp, PEER_NKEYS, T), F32)],
        compiler_params=_cparams(("parallel",)),
        name="mix",
    )(retg, o1, o2, o3, l1, l2, l3, proj, proj, x2, wro, wao, wo, g2, wq, sk)


def _topk_rows(s, ids, k, fill):
    vals, idxs = [], []
    for _ in range(k):
        m = jnp.max(s, axis=0, keepdims=True)
        i = jnp.min(jnp.where(s == m, ids, fill), axis=0, keepdims=True)
        vals.append(m)
        idxs.append(i)
        s = jnp.where(ids == i, -jnp.inf, s)
    return jnp.concatenate(vals, axis=0), jnp.concatenate(idxs, axis=0)


_CAND_B = [PEER_TOPK // (a + 1) for a in range(PEER_TOPK)]
_CAND_ROWS = -(-sum(_CAND_B) // 8) * 8


def _topk_kernel(sc_ref, off_ref, par_ref, gate_ref, e_scr, g_scr):
    K = PEER_TOPK
    W = sc_ref.shape[-1]
    key_ids = lax.broadcasted_iota(jnp.int32, (PEER_NKEYS, W), 0)
    npad = _CAND_ROWS - sum(_CAND_B)
    cand_ids = jnp.concatenate(
        [jnp.full((1, W), a * K + b, jnp.int32) for a in range(K) for b in range(_CAND_B[a])]
        + [jnp.full((npad, W), K * K, jnp.int32)], axis=0)
    pad = jnp.full((npad, W), -jnp.inf, F32)

    def head(h, carry):
        v0, i0 = _topk_rows(sc_ref[2 * h], key_ids, K, PEER_NKEYS)
        v1, i1 = _topk_rows(sc_ref[2 * h + 1], key_ids, K, PEER_NKEYS)
        cand = jnp.concatenate([v0[a:a + 1] + v1[0:_CAND_B[a]] for a in range(K)] + [pad], axis=0)
        best_s, best_c = _topk_rows(cand, cand_ids, K, K * K)
        ca = best_c // K
        cb = best_c % K
        ia = jnp.zeros_like(best_c)
        ib = jnp.zeros_like(best_c)
        for a in range(K):
            ia = jnp.where(ca == a, i0[a:a + 1], ia)
            ib = jnp.where(cb == a, i1[a:a + 1], ib)
        e_scr[h] = (ia * PEER_NKEYS + ib).astype(F32)
        ex = jnp.exp(best_s - best_s[0:1])
        g_scr[h] = ex / jnp.sum(ex, axis=0, keepdims=True)
        return carry

    lax.fori_loop(0, PEER_HEADS, head, 0)
    e = e_scr[...].reshape(PEER_PICKS, W).T.astype(jnp.int32)
    off_ref[...] = lax.shift_right_logical(e, 1) * 8
    par_ref[...] = (e & 1).astype(F32)
    gate_ref[...] = g_scr[...].reshape(PEER_PICKS, W).T


def _topk(scT):
    T = scT.shape[-1]
    tb = TOPK_TB
    out_spec = pl.BlockSpec((tb, PEER_PICKS), lambda i: (i, 0))
    return pl.pallas_call(
        _topk_kernel,
        grid=(T // tb,),
        in_specs=[pl.BlockSpec((2 * PEER_HEADS, PEER_NKEYS, tb), lambda i: (0, 0, i))],
        out_specs=[out_spec, out_spec, out_spec],
        out_shape=[jax.ShapeDtypeStruct((T, PEER_PICKS), jnp.int32),
                   jax.ShapeDtypeStruct((T, PEER_PICKS), F32),
                   jax.ShapeDtypeStruct((T, PEER_PICKS), F32)],
        scratch_shapes=[pltpu.VMEM((PEER_HEADS, PEER_TOPK, tb), F32),
                        pltpu.VMEM((PEER_HEADS, PEER_TOPK, tb), F32)],
        compiler_params=_cparams(("parallel",)),
        name="topk",
    )(scT)


PEER_U_VALU = 64
PEER_V_VALU = 16


def _pack_table(tab):
    E = tab.shape[0]
    b = lax.bitcast_convert_type(tab.astype(BF16), jnp.uint16).astype(jnp.uint32).reshape(E // 2, 2, 8, 128)
    return (b[:, 0] | (b[:, 1] << 16)).reshape(4 * E, 128)


def _split3(a):
    p0 = a.astype(BF16)
    r1 = a - p0.astype(F32)
    p1 = r1.astype(BF16)
    p2 = (r1 - p1.astype(F32)).astype(BF16)
    return p0, p1, p2


def _pair_expand(v, which, base, pairs):
    j = lax.broadcasted_iota(jnp.int32, (PEER_PICKS, 16 * pairs), 0)
    k = lax.broadcasted_iota(jnp.int32, (PEER_PICKS, 16 * pairs), 1)
    onehot = jnp.where(j == base + 2 * (k // 16) + which, 1.0, 0.0).astype(BF16)
    return sum(jnp.dot(p, onehot, preferred_element_type=F32) for p in _split3(v))


def _pair_targets(par, which, base, pairs):
    r = lax.broadcasted_iota(jnp.int32, (par.shape[0], 16 * pairs), 1) % 16
    return 0.5 * (r.astype(F32) - _pair_expand(par, which, base, pairs))


def _tile_words(tab_ref, off_ref, t, j):
    off = pl.multiple_of(off_ref[t, j], 8)
    return tab_ref[pl.ds(off, 8), :]


def _fill_lane_broadcasts(dst_ref, src, n):
    for j in range(n):
        dst_ref[j] = jnp.broadcast_to(src[:, j:j + 1], dst_ref.shape[1:])


def _expert_rows(words, shv):
    return pltpu.bitcast((words << shv) & jnp.uint32(0xFFFF0000), F32)


def _sublane_sums(ps):
    sub = lax.broadcasted_iota(jnp.int32, (8, 128), 0)
    lo4 = sub < 4
    t = [jnp.where(lo4, ps[j], ps[j + 4]) + pltpu.roll(jnp.where(lo4, ps[j + 4], ps[j]), 4, 0)
         for j in range(4)]
    m2 = (sub & 2) == 0
    u = [jnp.where(m2, t[j] + pltpu.roll(t[j], 6, 0), t[j + 2] + pltpu.roll(t[j + 2], 2, 0))
         for j in range(2)]
    m1 = (sub & 1) == 0
    return jnp.where(m1, u[0] + pltpu.roll(u[0], 7, 0), u[1] + pltpu.roll(u[1], 1, 0))


def _token_loop(tb, body):
    def group(i, carry):
        for s in range(PEER_UNROLL):
            body(i * PEER_UNROLL + s)
        return carry

    lax.fori_loop(0, tb // PEER_UNROLL, group, 0)


def _peer_u_kernel(off_ref, par_ref, x_ref, gate_ref, tab_ref, w_ref, tga_ref, tgb_ref, v3_ref, shb_ref, actv_ref):
    tb = x_ref.shape[0]
    nv = PEER_U_VALU
    pairs = (PEER_PICKS - nv) // 2
    nblk = 16 * pairs // 128
    par = par_ref[...]
    tga_ref[...] = _pair_targets(par, 0, nv, pairs)
    tgb_ref[...] = _pair_targets(par, 1, nv, pairs)
    _fill_lane_broadcasts(shb_ref, ((1.0 - par) * 16.0).astype(jnp.uint32), nv)
    m_iota = lax.broadcasted_iota(jnp.int32, (8, 16 * pairs), 0).astype(F32)
    zero = jnp.zeros((8, 128), F32)
    ones = jnp.ones((8, 128), BF16)
    nt = (((1,), (1,)), ((), ()))

    def body(t):
        xt = x_ref[t]
        row = pl.ds(t, 1)
        rs = []
        for g in range(nv // 8):
            ps = []
            for jj in range(8):
                j = g * 8 + jj
                shv = jnp.broadcast_to(shb_ref[j, row, :], (8, 128))
                ps.append(_expert_rows(_tile_words(tab_ref, off_ref, t, j), shv) * xt)
            rs.append(_sublane_sums(ps))
        r = jnp.concatenate(rs, axis=0)
        hi = r.astype(BF16)
        lo = (r - hi.astype(F32)).astype(BF16)
        av = (lax.dot_general(ones, hi, nt, preferred_element_type=F32)
              + lax.dot_general(ones, lo, nt, preferred_element_type=F32))
        actv_ref[row, 0:nv] = av[0:1]
        tiles = [pltpu.bitcast(_tile_words(tab_ref, off_ref, t, j), BF16) for j in range(nv, PEER_PICKS)]
        wmat = jnp.concatenate(
            [jnp.concatenate([tiles[2 * q], tiles[2 * q + 1]], axis=1) for q in range(pairs)], axis=0)
        xf = jnp.concatenate([jnp.concatenate([xt, zero], axis=1),
                              jnp.concatenate([zero, xt], axis=1)], axis=0)
        xh = xf.astype(BF16)
        xl = (xf - xh.astype(F32)).astype(BF16)
        o = lax.dot_general(jnp.concatenate([xh, xl], axis=0), wmat, nt, preferred_element_type=F32)
        o = o[0:16] + o[16:32]
        za = jnp.where(tga_ref[row, :] == m_iota, o[0:8], 0.0)
        zb = jnp.where(tgb_ref[row, :] == m_iota, o[8:16], 0.0)
        blocks = ([za[:, i * 128:(i + 1) * 128] for i in range(nblk)]
                  + [zb[:, i * 128:(i + 1) * 128] for i in range(nblk)])
        v = _sublane_sums(blocks + [zero] * (8 - 2 * nblk))
        for i in range(2 * nblk):
            v3_ref[i, row, :] = v[i:i + 1]

    _token_loop(tb, body)
    lane = lax.broadcasted_iota(jnp.int32, (128, PEER_PICKS), 0)
    pick = lax.broadcasted_iota(jnp.int32, (128, PEER_PICKS), 1)
    act = jnp.zeros((tb, PEER_PICKS), F32)
    for c in range(2 * nblk):
        sel = jnp.where(pick == nv + 2 * (8 * (c % nblk) + lane // 16) + c // nblk, 1.0, 0.0).astype(BF16)
        for p in _split3(v3_ref[c]):
            act = act + jnp.dot(p, sel, preferred_element_type=F32)
    act = jnp.where(pick[0:1] < nv, actv_ref[...], act)
    gelu = 0.5 * act * (1.0 + lax.erf(act * (2.0 ** -0.5)))
    w_ref[...] = gate_ref[...] * gelu


def _peer_specs(tb, tab):
    rows = pl.BlockSpec((tb, PEER_PICKS), lambda i: (i, 0))
    smem = pl.BlockSpec((tb, PEER_PICKS), lambda i: (i, 0), memory_space=pltpu.SMEM)
    table = pl.BlockSpec(tab.shape, lambda i: (0, 0), pipeline_mode=pl.Buffered(1))
    return rows, smem, table


def _peer_u(off, par, x3, gate, tab):
    T = x3.shape[0]
    tb = PEER_TB
    rows, smem, table = _peer_specs(tb, tab)
    pairs = (PEER_PICKS - PEER_U_VALU) // 2
    return pl.pallas_call(
        _peer_u_kernel,
        grid=(T // tb,),
        in_specs=[smem, rows, pl.BlockSpec((tb, 8, 128), lambda i: (i, 0, 0)), rows, table],
        out_specs=rows,
        out_shape=jax.ShapeDtypeStruct((T, PEER_PICKS), F32),
        scratch_shapes=[pltpu.VMEM((tb, 16 * pairs), F32), pltpu.VMEM((tb, 16 * pairs), F32),
                        pltpu.VMEM((pairs // 4, tb, 128), F32),
                        pltpu.VMEM((PEER_U_VALU, tb, 128), jnp.uint32),
                        pltpu.VMEM((tb, PEER_PICKS), F32)],
        compiler_params=_cparams(("parallel",)),
        name="peer_u",
    )(off, par, x3, gate, tab)


def _peer_v_kernel(off_ref, par_ref, w_ref, tab_ref, o_ref, wa_ref, wb_ref, tga_ref, tgb_ref, shb_ref, wbc_ref):
    tb = o_ref.shape[0]
    nv = PEER_V_VALU
    pairs = (PEER_PICKS - nv) // 2
    par = par_ref[...]
    w = w_ref[...]
    wa_ref[...] = _pair_expand(w, 0, nv, pairs)
    wb_ref[...] = _pair_expand(w, 1, nv, pairs)
    tga_ref[...] = _pair_targets(par, 0, nv, pairs)
    tgb_ref[...] = _pair_targets(par, 1, nv, pairs)
    _fill_lane_broadcasts(shb_ref, ((1.0 - par) * 16.0).astype(jnp.uint32), nv)
    _fill_lane_broadcasts(wbc_ref, w, nv)
    m_iota = lax.broadcasted_iota(jnp.int32, (8, 16 * pairs), 0).astype(F32)
    nacc = 4

    def body(t):
        row = pl.ds(t, 1)
        acc = [jnp.zeros((8, 128), F32) for _ in range(nacc)]
        for j in range(nv):
            shv = jnp.broadcast_to(shb_ref[j, row, :], (8, 128))
            wv = jnp.broadcast_to(wbc_ref[j, row, :], (8, 128))
            acc[j % nacc] = acc[j % nacc] + wv * _expert_rows(_tile_words(tab_ref, off_ref, t, j), shv)
        tiles = [pltpu.bitcast(_tile_words(tab_ref, off_ref, t, j), BF16) for j in range(nv, PEER_PICKS)]
        vmat = jnp.concatenate([jnp.concatenate(tiles[0::2], axis=0),
                                jnp.concatenate(tiles[1::2], axis=0)], axis=1)
        la = jnp.where(tga_ref[row, :] == m_iota, wa_ref[row, :], 0.0)
        lb = jnp.where(tgb_ref[row, :] == m_iota, wb_ref[row, :], 0.0)
        lf = jnp.concatenate([la, lb], axis=0)
        lh = lf.astype(BF16)
        ll = (lf - lh.astype(F32)).astype(BF16)
        o = jnp.dot(jnp.concatenate([lh, ll], axis=0), vmat, preferred_element_type=F32)
        o_ref[t] = (((o[0:8, 0:128] + o[8:16, 128:256]) + (o[16:24, 0:128] + o[24:32, 128:256]))
                    + ((acc[0] + acc[1]) + (acc[2] + acc[3])))

    _token_loop(tb, body)


def _peer_v(off, par, w, tab):
    T = par.shape[0]
    tb = PEER_TB
    rows, smem, table = _peer_specs(tb, tab)
    pairs = (PEER_PICKS - PEER_V_VALU) // 2
    return pl.pallas_call(
        _peer_v_kernel,
        grid=(T // tb,),
        in_specs=[smem, rows, rows, table],
        out_specs=pl.BlockSpec((tb, 8, 128), lambda i: (i, 0, 0)),
        out_shape=jax.ShapeDtypeStruct((T, 8, 128), F32),
        scratch_shapes=[pltpu.VMEM((tb, 16 * pairs), F32) for _ in range(4)]
        + [pltpu.VMEM((PEER_V_VALU, tb, 128), jnp.uint32), pltpu.VMEM((PEER_V_VALU, tb, 128), F32)],
        compiler_params=_cparams(("parallel",)),
        name="peer_v",
    )(off, par, w, tab)


def _final_kernel(h_ref, p_ref, g_ref, o_ref):
    nt = D_MODEL // 128
    tm = h_ref.shape[0]
    p = jnp.concatenate([p_ref[pl.ds(c, tm, stride=nt), :] for c in range(nt)], axis=1)
    h = h_ref[...] + p
    o_ref[...] = h * lax.rsqrt(jnp.mean(h * h, axis=-1, keepdims=True) + EPS) * g_ref[...]


def _final(h, p, g):
    T = h.shape[0]
    row = pl.BlockSpec((FINAL_TM, D_MODEL), lambda i: (i, 0))
    return pl.pallas_call(
        _final_kernel,
        grid=(T // FINAL_TM,),
        in_specs=[row, pl.BlockSpec((FINAL_TM * (D_MODEL // 128), 128), lambda i: (i, 0)),
                  _const_spec((1, D_MODEL))],
        out_specs=row,
        out_shape=jax.ShapeDtypeStruct((T, D_MODEL), F32),
        compiler_params=_cparams(("parallel",)),
        name="final_norm",
    )(h, p, g)


def _layer(h2, B, S, norm1_g, w_in, w_ret_out, w_att_out, w_out, norm2_g, peer_wq, peer_subkeys, peer_u, peer_v):
    T = B * S
    aq, ak, av = 3072, 3072 + ATT_W, 3072 + 2 * ATT_W
    att_cols = [w_in[:, o + g * ATT_OUT_W:o + (g + 1) * ATT_OUT_W] for g in range(len(ATT_GROUPS)) for o in (aq, ak, av)]
    w_perm = jnp.concatenate([w_in[:, :3072], w_in[:, 7680:9728]] + att_cols, axis=1).astype(BF16)
    main, a0, a1, a2 = _proj(h2, norm1_g.reshape(1, D_MODEL), w_perm, B, S)
    retg = _retention(main.reshape(B, S, MAIN_W)).reshape(T, RET_V_W)
    (o1, l1), (o2, l2), (o3, l3) = (_att_group(a, g) for g, a in enumerate((a0.reshape(B, 1, S, ATT_SEC_W), a1, a2)))
    sk = peer_subkeys.reshape(2 * PEER_HEADS, PEER_NKEYS, PEER_DKEY // 2).astype(BF16)
    h_mid, xn2, scT = _mix(retg, o1, o2, o3, l1, l2, l3, main, h2,
                           w_ret_out.astype(BF16), w_att_out.astype(BF16), w_out.astype(BF16),
                           norm2_g.reshape(1, D_MODEL), peer_wq.astype(BF16), sk)
    off, par, gate = _topk(scT)
    w = _peer_u(off, par, xn2.reshape(T, 8, 128), gate, _pack_table(peer_u))
    pout = _peer_v(off, par, w, _pack_table(peer_v))
    return h_mid, pout.reshape(T * 8, 128)


def kernel(x, norm1_g, w_in, w_ret_out, w_att_out, w_out, norm2_g, peer_wq, peer_subkeys, peer_u, peer_v, normf_g):
    B, S, D = x.shape
    assert w_in.shape[0] == 1, "single-layer block"
    h, pout = _layer(x.reshape(B * S, D), B, S, norm1_g[0], w_in[0], w_ret_out[0], w_att_out[0], w_out[0],
                     norm2_g[0], peer_wq[0], peer_subkeys[0], peer_u[0], peer_v[0])
    return _final(h, pout, normf_g.reshape(1, D)).reshape(B, S, D)
```

```python
import functools
import math

import jax
import jax.numpy as jnp
from jax import lax
from jax.experimental import pallas as pl
from jax.experimental.pallas import tpu as pltpu

F32 = jnp.float32
BF16 = jnp.bfloat16

D_MODEL = 1024
RET_HEADS = 8
RET_DK = 64
RET_DV = 128
RET_CHUNK = 128
ATT_GROUPS = ((128, 1), (512, 4), (2048, 16))
ATT_HPG = 4
ATT_HEADS = 12
ATT_DH = 128
ATT_BLOCK = 128
ATT_BATCH = 2
PEER_HEADS = 8
PEER_NKEYS = 128
PEER_DKEY = 256
PEER_TOPK = 16
PEER_PICKS = PEER_HEADS * PEER_TOPK
EPS = 1e-6

RET_QK_W = RET_HEADS * RET_DK
RET_V_W = RET_HEADS * RET_DV
ATT_W = ATT_HEADS * ATT_DH
ATT_OUT_W = ATT_HPG * ATT_DH
OFF_RQ, OFF_RK, OFF_RV, OFF_RG = 0, 512, 1024, 2048
OFF_GR, OFF_GA = 3072, 4096
MAIN_W = 5120
ATT_SEC_W = 3 * ATT_OUT_W
IN_W = MAIN_W + len(ATT_GROUPS) * ATT_SEC_W

VMEM_LIMIT_BYTES = 60 * 1024 * 1024

PROJ_TM = 256
PROJ_TN = 512
MIX_TM = 256
TOPK_TB = 512
PEER_TB = 128
PEER_UNROLL = 16
FINAL_TM = 512


def _cparams(sem):
    return pltpu.CompilerParams(dimension_semantics=sem, vmem_limit_bytes=VMEM_LIMIT_BYTES)


def _const_spec(shape):
    nd = len(shape)
    return pl.BlockSpec(shape, lambda *_: (0,) * nd)


def _proj_kernel(x_ref, g_ref, w_ref, main_ref, a0_ref, a1_ref, a2_ref, scr_ref):
    x = x_ref[...]
    ms = jnp.mean(x * x, axis=-1, keepdims=True)
    xn = (x * lax.rsqrt(ms + EPS) * g_ref[...]).astype(BF16)

    def chunk(j):
        return jnp.dot(xn, w_ref[:, j * PROJ_TN:(j + 1) * PROJ_TN], preferred_element_type=F32)

    nmain = MAIN_W // PROJ_TN
    for j in range(nmain):
        main_ref[:, j * PROJ_TN:(j + 1) * PROJ_TN] = chunk(j).astype(BF16)
    for s in range(3):
        a0_ref[:, s * PROJ_TN:(s + 1) * PROJ_TN] = chunk(nmain + s).astype(BF16)
    for g, a_ref in ((1, a1_ref), (2, a2_ref)):
        d = ATT_GROUPS[g][1]
        for s in range(3):
            res = chunk(nmain + 3 * g + s)
            for c in range(PROJ_TN // 128):
                scr_ref[c] = res[:, c * 128:(c + 1) * 128]
            for r in range(d):
                piece = jnp.concatenate(
                    [scr_ref[c, pl.ds(r, PROJ_TM // d, stride=d), :] for c in range(PROJ_TN // 128)], axis=1)
                a_ref[0, r, :, s * PROJ_TN:(s + 1) * PROJ_TN] = piece.astype(BF16)


def _proj(x2, g, w, B, S):
    T = x2.shape[0]
    tiles = S // PROJ_TM
    d1, d2 = ATT_GROUPS[1][1], ATT_GROUPS[2][1]
    return pl.pallas_call(
        _proj_kernel,
        grid=(T // PROJ_TM,),
        in_specs=[
            pl.BlockSpec((PROJ_TM, D_MODEL), lambda i: (i, 0)),
            _const_spec((1, D_MODEL)),
            pl.BlockSpec((D_MODEL, IN_W), lambda i: (0, 0), pipeline_mode=pl.Buffered(1)),
        ],
        out_specs=[
            pl.BlockSpec((PROJ_TM, MAIN_W), lambda i: (i, 0)),
            pl.BlockSpec((PROJ_TM, ATT_SEC_W), lambda i: (i, 0)),
            pl.BlockSpec((1, d1, PROJ_TM // d1, ATT_SEC_W), lambda i: (i // tiles, 0, i % tiles, 0)),
            pl.BlockSpec((1, d2, PROJ_TM // d2, ATT_SEC_W), lambda i: (i // tiles, 0, i % tiles, 0)),
        ],
        out_shape=[
            jax.ShapeDtypeStruct((T, MAIN_W), BF16),
            jax.ShapeDtypeStruct((T, ATT_SEC_W), BF16),
            jax.ShapeDtypeStruct((B, d1, S // d1, ATT_SEC_W), BF16),
            jax.ShapeDtypeStruct((B, d2, S // d2, ATT_SEC_W), BF16),
        ],
        scratch_shapes=[pltpu.VMEM((PROJ_TN // 128, PROJ_TM, 128), F32)],
        compiler_params=_cparams(("parallel",)),
        name="proj",
    )(x2, g, w)


def _retention_kernel(q_ref, k_ref, v_ref, rg_ref, o_ref, state_ref, decay_ref):
    C = RET_CHUNK

    @pl.when(pl.program_id(1) == 0)
    def _():
        state_ref[...] = jnp.zeros_like(state_ref)
        pi = lax.broadcasted_iota(jnp.int32, (C, C), 0)
        pj = lax.broadcasted_iota(jnp.int32, (C, C), 1)
        diff = (pi - pj).astype(F32)
        for h in range(RET_HEADS):
            lg = math.log1p(-(2.0 ** (-5.0 - h)))
            decay_ref[h] = jnp.where(diff >= 0, jnp.exp(jnp.maximum(diff, 0.0) * lg), 0.0)

    pos = lax.broadcasted_iota(jnp.int32, (C, 1), 0).astype(F32)
    q = q_ref[0]
    k = k_ref[0]
    v = v_ref[0]
    rg = rg_ref[0]
    for h in range(RET_HEADS):
        lg = math.log1p(-(2.0 ** (-5.0 - h)))
        decay = decay_ref[h]
        w_k = jnp.exp((C - 1 - pos) * lg)
        w_q = jnp.exp((pos + 1.0) * lg)
        qh = q[:, h * RET_DK:(h + 1) * RET_DK]
        kh = k[:, h * RET_DK:(h + 1) * RET_DK].astype(F32) * (RET_DK ** -0.5)
        vh = v[:, h * RET_DV:(h + 1) * RET_DV]
        s = lax.dot_general(qh, kh.astype(BF16), (((1,), (1,)), ((), ())), preferred_element_type=F32)
        p = (s * decay).astype(BF16)
        inner = jnp.dot(p, vh, preferred_element_type=F32)
        st = state_ref[h]
        cross = jnp.dot((qh.astype(F32) * w_q).astype(BF16), st.astype(BF16), preferred_element_type=F32)
        kw = (kh * w_k).astype(BF16)
        kv = lax.dot_general(kw, vh, (((0,), (0,)), ((), ())), preferred_element_type=F32)
        state_ref[h] = math.exp(C * lg) * st + kv
        ret = inner + cross
        rn = ret * lax.rsqrt(jnp.mean(ret * ret, axis=-1, keepdims=True) + EPS)
        g = rg[:, h * RET_DV:(h + 1) * RET_DV].astype(F32)
        o_ref[0, :, h * RET_DV:(h + 1) * RET_DV] = (g * jax.nn.sigmoid(g) * rn).astype(BF16)


def _retention(proj3):
    B, S, _ = proj3.shape
    n = S // RET_CHUNK
    return pl.pallas_call(
        _retention_kernel,
        grid=(B, n),
        in_specs=[
            pl.BlockSpec((1, RET_CHUNK, RET_QK_W), lambda b, c: (b, c, OFF_RQ // RET_QK_W)),
            pl.BlockSpec((1, RET_CHUNK, RET_QK_W), lambda b, c: (b, c, OFF_RK // RET_QK_W)),
            pl.BlockSpec((1, RET_CHUNK, RET_V_W), lambda b, c: (b, c, OFF_RV // RET_V_W)),
            pl.BlockSpec((1, RET_CHUNK, RET_V_W), lambda b, c: (b, c, OFF_RG // RET_V_W)),
        ],
        out_specs=pl.BlockSpec((1, RET_CHUNK, RET_V_W), lambda b, c: (b, c, 0)),
        out_shape=jax.ShapeDtypeStruct((B, S, RET_V_W), BF16),
        scratch_shapes=[pltpu.VMEM((RET_HEADS, RET_DK, RET_DV), F32),
                        pltpu.VMEM((RET_HEADS, RET_CHUNK, RET_CHUNK), F32)],
        compiler_params=_cparams(("parallel", "arbitrary")),
        name="retention",
    )(proj3, proj3, proj3, proj3)


def _att_kernel(q_ref, kp_ref, kc_ref, vp_ref, vc_ref, o_ref, lse_ref, *, group, dilation, span):
    n = pl.program_id(1)
    r = pl.program_id(2)
    iq = lax.broadcasted_iota(jnp.int32, (ATT_BLOCK, 2 * ATT_BLOCK), 0)
    jk = lax.broadcasted_iota(jnp.int32, (ATT_BLOCK, 2 * ATT_BLOCK), 1)
    dist = iq + ATT_BLOCK - jk
    valid = (dist >= 0) & (dist <= span) & ((jk >= ATT_BLOCK) | (n > 0))
    distf = (dilation * dist).astype(F32)
    rows = pl.ds(r, ATT_BLOCK, stride=dilation) if dilation > 1 else slice(None)
    for bb in range(ATT_BATCH):
        q = q_ref[bb, 0]
        kk = jnp.concatenate([kp_ref[bb, 0], kc_ref[bb, 0]], axis=0)
        vv = jnp.concatenate([vp_ref[bb, 0], vc_ref[bb, 0]], axis=0)
        for i in range(ATT_HPG):
            slope = 2.0 ** (-8.0 * (group * ATT_HPG + i + 1) / ATT_HEADS)
            sl = slice(i * ATT_DH, (i + 1) * ATT_DH)
            s = lax.dot_general(q[:, sl], kk[:, sl], (((1,), (1,)), ((), ())), preferred_element_type=F32)
            s = s * (ATT_DH ** -0.5)
            s = jnp.where(valid, s - slope * distf, -jnp.inf)
            m = jnp.max(s, axis=-1, keepdims=True)
            p = jnp.exp(s - m)
            den = jnp.sum(p, axis=-1, keepdims=True)
            o_ref[i, bb, rows, :] = jnp.dot(p.astype(BF16), vv[:, sl], preferred_element_type=F32) / den
            lse_ref[i, bb, rows, :] = jnp.broadcast_to(m + jnp.log(den), (ATT_BLOCK, ATT_DH))


def _att_group(qkv, group):
    window, d = ATT_GROUPS[group]
    B, _, L, _ = qkv.shape
    S = L * d
    nL = L // ATT_BLOCK

    def cur(sec):
        return pl.BlockSpec((ATT_BATCH, 1, ATT_BLOCK, ATT_OUT_W), lambda b, n, r: (b, r, n, sec))

    def prev(sec):
        return pl.BlockSpec((ATT_BATCH, 1, ATT_BLOCK, ATT_OUT_W), lambda b, n, r: (b, r, jnp.maximum(n - 1, 0), sec))

    out_spec = pl.BlockSpec((ATT_HPG, ATT_BATCH, ATT_BLOCK * d, ATT_DH), lambda b, n, r: (0, b, n, 0))
    out_shape = jax.ShapeDtypeStruct((ATT_HPG, B, S, ATT_DH), F32)
    o, lse = pl.pallas_call(
        functools.partial(_att_kernel, group=group, dilation=d, span=window // d),
        grid=(B // ATT_BATCH, nL, d),
        in_specs=[cur(0), prev(1), cur(1), prev(2), cur(2)],
        out_specs=[out_spec, out_spec],
        out_shape=[out_shape, out_shape],
        compiler_params=_cparams(("parallel", "arbitrary", "arbitrary")),
        name=f"att_g{group}",
    )(qkv, qkv, qkv, qkv, qkv)
    return o.reshape(ATT_HPG, B * S, ATT_DH), lse.reshape(ATT_HPG, B * S, ATT_DH)


def _mix_kernel(retg_ref, o1_ref, o2_ref, o3_ref, l1_ref, l2_ref, l3_ref, gr_ref, ga_ref, x_ref,
                wro_ref, wao_ref, wo_ref, g2_ref, wq_ref, sk_ref, h_ref, xn_ref, sc_ref):
    heads = lambda ref: jnp.concatenate([ref[i] for i in range(ATT_HPG)], axis=1)
    l1, l2, l3 = heads(l1_ref), heads(l2_ref), heads(l3_ref)
    mx = jnp.maximum(jnp.maximum(l1, l2), l3)
    e1, e2, e3 = jnp.exp(l1 - mx), jnp.exp(l2 - mx), jnp.exp(l3 - mx)
    att = (e1 * heads(o1_ref) + e2 * heads(o2_ref) + e3 * heads(o3_ref)) / (e1 + e2 + e3)
    a_branch = jnp.dot(att.astype(BF16), wao_ref[...], preferred_element_type=F32)
    r_branch = jnp.dot(retg_ref[...], wro_ref[...], preferred_element_type=F32)
    merged = (jax.nn.sigmoid(gr_ref[...].astype(F32)) * r_branch
              + jax.nn.sigmoid(ga_ref[...].astype(F32)) * a_branch)
    h = x_ref[...] + jnp.dot(merged.astype(BF16), wo_ref[...], preferred_element_type=F32)
    h_ref[...] = h
    xn = h * lax.rsqrt(jnp.mean(h * h, axis=-1, keepdims=True) + EPS) * g2_ref[...]
    for c in range(D_MODEL // 128):
        xn_ref[pl.ds(c, h.shape[0], stride=D_MODEL // 128), :] = xn[:, c * 128:(c + 1) * 128]
    qp = jnp.dot(xn.astype(BF16), wq_ref[...], preferred_element_type=F32).astype(BF16)
    half = PEER_DKEY // 2
    for hp in range(2 * PEER_HEADS):
        sc_ref[hp] = lax.dot_general(sk_ref[hp], qp[:, hp * half:(hp + 1) * half],
                                     (((1,), (1,)), ((), ())), preferred_element_type=F32)


def _mix(retg, o1, o2, o3, l1, l2, l3, proj, x2, wro, wao, wo, g2, wq, sk):
    T = x2.shape[0]
    tm = MIX_TM
    row = lambda w: pl.BlockSpec((tm, w), lambda i: (i, 0))
    hd = pl.BlockSpec((ATT_HPG, tm, ATT_DH), lambda i: (0, i, 0))
    nhp = 2 * PEER_HEADS
    nt = D_MODEL // 128
    return pl.pallas_call(
        _mix_kernel,
        grid=(T // tm,),
        in_specs=[
            row(RET_V_W), hd, hd, hd, hd, hd, hd,
            pl.BlockSpec((tm, D_MODEL), lambda i: (i, OFF_GR // D_MODEL)),
            pl.BlockSpec((tm, D_MODEL), lambda i: (i, OFF_GA // D_MODEL)),
            row(D_MODEL),
            _const_spec(wro.shape), _const_spec(wao.shape), _const_spec(wo.shape),
            _const_spec(g2.shape), _const_spec(wq.shape), _const_spec(sk.shape),
        ],
        out_specs=[row(D_MODEL), pl.BlockSpec((tm * nt, 128), lambda i: (i, 0)),
                   pl.BlockSpec((nhp, PEER_NKEYS, tm), lambda i: (0, 0, i))],
        out_shape=[jax.ShapeDtypeStruct((T, D_MODEL), F32),
                   jax.ShapeDtypeStruct((T * nt, 128), F32),
                   jax.ShapeDtypeStruct((nhp, PEER_NKEYS, T), F32)],
        compiler_params=_cparams(("parallel",)),
        name="mix",
    )(retg, o1, o2, o3, l1, l2, l3, proj, proj, x2, wro, wao, wo, g2, wq, sk)


def _topk_rows(s, ids, k, fill):
    vals, idxs = [], []
    for _ in range(k):
        m = jnp.max(s, axis=0, keepdims=True)
        i = jnp.min(jnp.where(s == m, ids, fill), axis=0, keepdims=True)
        vals.append(m)
        idxs.append(i)
        s = jnp.where(ids == i, -jnp.inf, s)
    return jnp.concatenate(vals, axis=0), jnp.concatenate(idxs, axis=0)


_CAND_B = [PEER_TOPK // (a + 1) for a in range(PEER_TOPK)]
_CAND_ROWS = -(-sum(_CAND_B) // 8) * 8


def _topk_kernel(sc_ref, off_ref, par_ref, gate_ref, e_scr, g_scr):
    K = PEER_TOPK
    W = sc_ref.shape[-1]
    key_ids = lax.broadcasted_iota(jnp.int32, (PEER_NKEYS, W), 0)
    npad = _CAND_ROWS - sum(_CAND_B)
    cand_ids = jnp.concatenate(
        [jnp.full((1, W), a * K + b, jnp.int32) for a in range(K) for b in range(_CAND_B[a])]
        + [jnp.full((npad, W), K * K, jnp.int32)], axis=0)
    pad = jnp.full((npad, W), -jnp.inf, F32)

    def head(h, carry):
        v0, i0 = _topk_rows(sc_ref[2 * h], key_ids, K, PEER_NKEYS)
        v1, i1 = _topk_rows(sc_ref[2 * h + 1], key_ids, K, PEER_NKEYS)
        cand = jnp.concatenate([v0[a:a + 1] + v1[0:_CAND_B[a]] for a in range(K)] + [pad], axis=0)
        best_s, best_c = _topk_rows(cand, cand_ids, K, K * K)
        ca = best_c // K
        cb = best_c % K
        ia = jnp.zeros_like(best_c)
        ib = jnp.zeros_like(best_c)
        for a in range(K):
            ia = jnp.where(ca == a, i0[a:a + 1], ia)
            ib = jnp.where(cb == a, i1[a:a + 1], ib)
        e_scr[h] = (ia * PEER_NKEYS + ib).astype(F32)
        ex = jnp.exp(best_s - best_s[0:1])
        g_scr[h] = ex / jnp.sum(ex, axis=0, keepdims=True)
        return carry

    lax.fori_loop(0, PEER_HEADS, head, 0)
    e = e_scr[...].reshape(PEER_PICKS, W).T.astype(jnp.int32)
    off_ref[...] = lax.shift_right_logical(e, 1) * 8
    par_ref[...] = (e & 1).astype(F32)
    gate_ref[...] = g_scr[...].reshape(PEER_PICKS, W).T


def _topk(scT):
    T = scT.shape[-1]
    tb = TOPK_TB
    out_spec = pl.BlockSpec((tb, PEER_PICKS), lambda i: (i, 0))
    return pl.pallas_call(
        _topk_kernel,
        grid=(T // tb,),
        in_specs=[pl.BlockSpec((2 * PEER_HEADS, PEER_NKEYS, tb), lambda i: (0, 0, i))],
        out_specs=[out_spec, out_spec, out_spec],
        out_shape=[jax.ShapeDtypeStruct((T, PEER_PICKS), jnp.int32),
                   jax.ShapeDtypeStruct((T, PEER_PICKS), F32),
                   jax.ShapeDtypeStruct((T, PEER_PICKS), F32)],
        scratch_shapes=[pltpu.VMEM((PEER_HEADS, PEER_TOPK, tb), F32),
                        pltpu.VMEM((PEER_HEADS, PEER_TOPK, tb), F32)],
        compiler_params=_cparams(("parallel",)),
        name="topk",
    )(scT)


PEER_U_VALU = 64
PEER_V_VALU = 16


def _pack_table(tab):
    E = tab.shape[0]
    b = lax.bitcast_convert_type(tab.astype(BF16), jnp.uint16).astype(jnp.uint32).reshape(E // 2, 2, 8, 128)
    return (b[:, 0] | (b[:, 1] << 16)).reshape(4 * E, 128)


def _split3(a):
    p0 = a.astype(BF16)
    r1 = a - p0.astype(F32)
    p1 = r1.astype(BF16)
    p2 = (r1 - p1.astype(F32)).astype(BF16)
    return p0, p1, p2


def _pair_expand(v, which, base, pairs):
    j = lax.broadcasted_iota(jnp.int32, (PEER_PICKS, 16 * pairs), 0)
    k = lax.broadcasted_iota(jnp.int32, (PEER_PICKS, 16 * pairs), 1)
    onehot = jnp.where(j == base + 2 * (k // 16) + which, 1.0, 0.0).astype(BF16)
    return sum(jnp.dot(p, onehot, preferred_element_type=F32) for p in _split3(v))


def _pair_targets(par, which, base, pairs):
    r = lax.broadcasted_iota(jnp.int32, (par.shape[0], 16 * pairs), 1) % 16
    return 0.5 * (r.astype(F32) - _pair_expand(par, which, base, pairs))


def _tile_words(tab_ref, off_ref, t, j):
    off = pl.multiple_of(off_ref[t, j], 8)
    return tab_ref[pl.ds(off, 8), :]


def _fill_lane_broadcasts(dst_ref, src, n):
    for j in range(n):
        dst_ref[j] = jnp.broadcast_to(src[:, j:j + 1], dst_ref.shape[1:])


def _expert_rows(words, shv):
    return pltpu.bitcast((words << shv) & jnp.uint32(0xFFFF0000), F32)


def _sublane_sums(ps):
    sub = lax.broadcasted_iota(jnp.int32, (8, 128), 0)
    lo4 = sub < 4
    t = [jnp.where(lo4, ps[j], ps[j + 4]) + pltpu.roll(jnp.where(lo4, ps[j + 4], ps[j]), 4, 0)
         for j in range(4)]
    m2 = (sub & 2) == 0
    u = [jnp.where(m2, t[j] + pltpu.roll(t[j], 6, 0), t[j + 2] + pltpu.roll(t[j + 2], 2, 0))
         for j in range(2)]
    m1 = (sub & 1) == 0
    return jnp.where(m1, u[0] + pltpu.roll(u[0], 7, 0), u[1] + pltpu.roll(u[1], 1, 0))


def _token_loop(tb, body):
    def group(i, carry):
        for s in range(PEER_UNROLL):
            body(i * PEER_UNROLL + s)
        return carry

    lax.fori_loop(0, tb // PEER_UNROLL, group, 0)


def _peer_u_kernel(off_ref, par_ref, x_ref, gate_ref, tab_ref, w_ref, tga_ref, tgb_ref, v3_ref, shb_ref, actv_ref):
    tb = x_ref.shape[0]
    nv = PEER_U_VALU
    pairs = (PEER_PICKS - nv) // 2
    nblk = 16 * pairs // 128
    par = par_ref[...]
    tga_ref[...] = _pair_targets(par, 0, nv, pairs)
    tgb_ref[...] = _pair_targets(par, 1, nv, pairs)
    _fill_lane_broadcasts(shb_ref, ((1.0 - par) * 16.0).astype(jnp.uint32), nv)
    m_iota = lax.broadcasted_iota(jnp.int32, (8, 16 * pairs), 0).astype(F32)
    zero = jnp.zeros((8, 128), F32)
    ones = jnp.ones((8, 128), BF16)
    nt = (((1,), (1,)), ((), ()))

    def body(t):
        xt = x_ref[t]
        row = pl.ds(t, 1)
        rs = []
        for g in range(nv // 8):
            ps = []
            for jj in range(8):
                j = g * 8 + jj
                shv = jnp.broadcast_to(shb_ref[j, row, :], (8, 128))
                ps.append(_expert_rows(_tile_words(tab_ref, off_ref, t, j), shv) * xt)
            rs.append(_sublane_sums(ps))
        r = jnp.concatenate(rs, axis=0)
        hi = r.astype(BF16)
        lo = (r - hi.astype(F32)).astype(BF16)
        av = (lax.dot_general(ones, hi, nt, preferred_element_type=F32)
              + lax.dot_general(ones, lo, nt, preferred_element_type=F32))
        actv_ref[row, 0:nv] = av[0:1]
        tiles = [pltpu.bitcast(_tile_words(tab_ref, off_ref, t, j), BF16) for j in range(nv, PEER_PICKS)]
        wmat = jnp.concatenate(
            [jnp.concatenate([tiles[2 * q], tiles[2 * q + 1]], axis=1) for q in range(pairs)], axis=0)
        xf = jnp.concatenate([jnp.concatenate([xt, zero], axis=1),
                              jnp.concatenate([zero, xt], axis=1)], axis=0)
        xh = xf.astype(BF16)
        xl = (xf - xh.astype(F32)).astype(BF16)
        o = lax.dot_general(jnp.concatenate([xh, xl], axis=0), wmat, nt, preferred_element_type=F32)
        o = o[0:16] + o[16:32]
        za = jnp.where(tga_ref[row, :] == m_iota, o[0:8], 0.0)
        zb = jnp.where(tgb_ref[row, :] == m_iota, o[8:16], 0.0)
        for half, z in enumerate((za, zb)):
            v = _sublane_sums([z[:, i * 128:(i + 1) * 128] for i in range(nblk)] + [zero] * (8 - nblk))
            for i in range(nblk):
                v3_ref[half * nblk + i, row, :] = v[i:i + 1]

    _token_loop(tb, body)
    lane = lax.broadcasted_iota(jnp.int32, (128, PEER_PICKS), 0)
    pick = lax.broadcasted_iota(jnp.int32, (128, PEER_PICKS), 1)
    act = jnp.zeros((tb, PEER_PICKS), F32)
    for c in range(2 * nblk):
        sel = jnp.where(pick == nv + 2 * (8 * (c % nblk) + lane // 16) + c // nblk, 1.0, 0.0).astype(BF16)
        for p in _split3(v3_ref[c]):
            act = act + jnp.dot(p, sel, preferred_element_type=F32)
    act = jnp.where(pick[0:1] < nv, actv_ref[...], act)
    gelu = 0.5 * act * (1.0 + lax.erf(act * (2.0 ** -0.5)))
    w_ref[...] = gate_ref[...] * gelu


def _peer_specs(tb, tab):
    rows = pl.BlockSpec((tb, PEER_PICKS), lambda i: (i, 0))
    smem = pl.BlockSpec((tb, PEER_PICKS), lambda i: (i, 0), memory_space=pltpu.SMEM)
    table = pl.BlockSpec(tab.shape, lambda i: (0, 0), pipeline_mode=pl.Buffered(1))
    return rows, smem, table


def _peer_u(off, par, x3, gate, tab):
    T = x3.shape[0]
    tb = PEER_TB
    rows, smem, table = _peer_specs(tb, tab)
    pairs = (PEER_PICKS - PEER_U_VALU) // 2
    return pl.pallas_call(
        _peer_u_kernel,
        grid=(T // tb,),
        in_specs=[smem, rows, pl.BlockSpec((tb, 8, 128), lambda i: (i, 0, 0)), rows, table],
        out_specs=rows,
        out_shape=jax.ShapeDtypeStruct((T, PEER_PICKS), F32),
        scratch_shapes=[pltpu.VMEM((tb, 16 * pairs), F32), pltpu.VMEM((tb, 16 * pairs), F32),
                        pltpu.VMEM((pairs // 4, tb, 128), F32),
                        pltpu.VMEM((PEER_U_VALU, tb, 128), jnp.uint32),
                        pltpu.VMEM((tb, PEER_PICKS), F32)],
        compiler_params=_cparams(("parallel",)),
        name="peer_u",
    )(off, par, x3, gate, tab)


def _peer_v_kernel(off_ref, par_ref, w_ref, tab_ref, o_ref, wa_ref, wb_ref, tga_ref, tgb_ref, shb_ref, wbc_ref):
    tb = o_ref.shape[0]
    nv = PEER_V_VALU
    pairs = (PEER_PICKS - nv) // 2
    par = par_ref[...]
    w = w_ref[...]
    wa_ref[...] = _pair_expand(w, 0, nv, pairs)
    wb_ref[...] = _pair_expand(w, 1, nv, pairs)
    tga_ref[...] = _pair_targets(par, 0, nv, pairs)
    tgb_ref[...] = _pair_targets(par, 1, nv, pairs)
    _fill_lane_broadcasts(shb_ref, ((1.0 - par) * 16.0).astype(jnp.uint32), nv)
    _fill_lane_broadcasts(wbc_ref, w, nv)
    m_iota = lax.broadcasted_iota(jnp.int32, (8, 16 * pairs), 0).astype(F32)
    nacc = 4

    def body(t):
        row = pl.ds(t, 1)
        acc = [jnp.zeros((8, 128), F32) for _ in range(nacc)]
        for j in range(nv):
            shv = jnp.broadcast_to(shb_ref[j, row, :], (8, 128))
            wv = jnp.broadcast_to(wbc_ref[j, row, :], (8, 128))
            acc[j % nacc] = acc[j % nacc] + wv * _expert_rows(_tile_words(tab_ref, off_ref, t, j), shv)
        tiles = [pltpu.bitcast(_tile_words(tab_ref, off_ref, t, j), BF16) for j in range(nv, PEER_PICKS)]
        vmat = jnp.concatenate([jnp.concatenate(tiles[0::2], axis=0),
                                jnp.concatenate(tiles[1::2], axis=0)], axis=1)
        la = jnp.where(tga_ref[row, :] == m_iota, wa_ref[row, :], 0.0)
        lb = jnp.where(tgb_ref[row, :] == m_iota, wb_ref[row, :], 0.0)
        lf = jnp.concatenate([la, lb], axis=0)
        lh = lf.astype(BF16)
        ll = (lf - lh.astype(F32)).astype(BF16)
        o = jnp.dot(jnp.concatenate([lh, ll], axis=0), vmat, preferred_element_type=F32)
        o_ref[t] = (((o[0:8, 0:128] + o[8:16, 128:256]) + (o[16:24, 0:128] + o[24:32, 128:256]))
                    + ((acc[0] + acc[1]) + (acc[2] + acc[3])))

    _token_loop(tb, body)


def _peer_v(off, par, w, tab):
    T = par.shape[0]
    tb = PEER_TB
    rows, smem, table = _peer_specs(tb, tab)
    pairs = (PEER_PICKS - PEER_V_VALU) // 2
    return pl.pallas_call(
        _peer_v_kernel,
        grid=(T // tb,),
        in_specs=[smem, rows, rows, table],
        out_specs=pl.BlockSpec((tb, 8, 128), lambda i: (i, 0, 0)),
        out_shape=jax.ShapeDtypeStruct((T, 8, 128), F32),
        scratch_shapes=[pltpu.VMEM((tb, 16 * pairs), F32) for _ in range(4)]
        + [pltpu.VMEM((PEER_V_VALU, tb, 128), jnp.uint32), pltpu.VMEM((PEER_V_VALU, tb, 128), F32)],
        compiler_params=_cparams(("parallel",)),
        name="peer_v",
    )(off, par, w, tab)


def _final_kernel(h_ref, p_ref, g_ref, o_ref):
    nt = D_MODEL // 128
    tm = h_ref.shape[0]
    p = jnp.concatenate([p_ref[pl.ds(c, tm, stride=nt), :] for c in range(nt)], axis=1)
    h = h_ref[...] + p
    o_ref[...] = h * lax.rsqrt(jnp.mean(h * h, axis=-1, keepdims=True) + EPS) * g_ref[...]


def _final(h, p, g):
    T = h.shape[0]
    row = pl.BlockSpec((FINAL_TM, D_MODEL), lambda i: (i, 0))
    return pl.pallas_call(
        _final_kernel,
        grid=(T // FINAL_TM,),
        in_specs=[row, pl.BlockSpec((FINAL_TM * (D_MODEL // 128), 128), lambda i: (i, 0)),
                  _const_spec((1, D_MODEL))],
        out_specs=row,
        out_shape=jax.ShapeDtypeStruct((T, D_MODEL), F32),
        compiler_params=_cparams(("parallel",)),
        name="final_norm",
    )(h, p, g)


def _layer(h2, B, S, norm1_g, w_in, w_ret_out, w_att_out, w_out, norm2_g, peer_wq, peer_subkeys, peer_u, peer_v):
    T = B * S
    aq, ak, av = 3072, 3072 + ATT_W, 3072 + 2 * ATT_W
    att_cols = [w_in[:, o + g * ATT_OUT_W:o + (g + 1) * ATT_OUT_W] for g in range(len(ATT_GROUPS)) for o in (aq, ak, av)]
    w_perm = jnp.concatenate([w_in[:, :3072], w_in[:, 7680:9728]] + att_cols, axis=1).astype(BF16)
    main, a0, a1, a2 = _proj(h2, norm1_g.reshape(1, D_MODEL), w_perm, B, S)
    retg = _retention(main.reshape(B, S, MAIN_W)).reshape(T, RET_V_W)
    (o1, l1), (o2, l2), (o3, l3) = (_att_group(a, g) for g, a in enumerate((a0.reshape(B, 1, S, ATT_SEC_W), a1, a2)))
    sk = peer_subkeys.reshape(2 * PEER_HEADS, PEER_NKEYS, PEER_DKEY // 2).astype(BF16)
    h_mid, xn2, scT = _mix(retg, o1, o2, o3, l1, l2, l3, main, h2,
                           w_ret_out.astype(BF16), w_att_out.astype(BF16), w_out.astype(BF16),
                           norm2_g.reshape(1, D_MODEL), peer_wq.astype(BF16), sk)
    off, par, gate = _topk(scT)
    w = _peer_u(off, par, xn2.reshape(T, 8, 128), gate, _pack_table(peer_u))
    pout = _peer_v(off, par, w, _pack_table(peer_v))
    return h_mid, pout.reshape(T * 8, 128)


def kernel(x, norm1_g, w_in, w_ret_out, w_att_out, w_out, norm2_g, peer_wq, peer_subkeys, peer_u, peer_v, normf_g):
    B, S, D = x.shape
    assert w_in.shape[0] == 1, "single-layer block"
    h, pout = _layer(x.reshape(B * S, D), B, S, norm1_g[0], w_in[0], w_ret_out[0], w_att_out[0], w_out[0],
                     norm2_g[0], peer_wq[0], peer_subkeys[0], peer_u[0], peer_v[0])
    return _final(h, pout, normf_g.reshape(1, D)).reshape(B, S, D)
```

```python
import functools
import math

import jax
import jax.numpy as jnp
from jax import lax
from jax.experimental import pallas as pl
from jax.experimental.pallas import tpu as pltpu

F32 = jnp.float32
BF16 = jnp.bfloat16

D_MODEL = 1024
RET_HEADS = 8
RET_DK = 64
RET_DV = 128
RET_CHUNK = 128
ATT_GROUPS = ((128, 1), (512, 4), (2048, 16))
ATT_HPG = 4
ATT_HEADS = 12
ATT_DH = 128
ATT_BLOCK = 128
ATT_BATCH = 2
PEER_HEADS = 8
PEER_NKEYS = 128
PEER_DKEY = 256
PEER_TOPK = 16
PEER_PICKS = PEER_HEADS * PEER_TOPK
EPS = 1e-6

RET_QK_W = RET_HEADS * RET_DK
RET_V_W = RET_HEADS * RET_DV
ATT_W = ATT_HEADS * ATT_DH
ATT_OUT_W = ATT_HPG * ATT_DH
OFF_RQ, OFF_RK, OFF_RV, OFF_RG = 0, 512, 1024, 2048
OFF_GR, OFF_GA = 3072, 4096
MAIN_W = 5120
ATT_SEC_W = 3 * ATT_OUT_W
IN_W = MAIN_W + len(ATT_GROUPS) * ATT_SEC_W

VMEM_LIMIT_BYTES = 60 * 1024 * 1024

PROJ_TM = 256
PROJ_TN = 512
MIX_TM = 256
TOPK_TB = 512
PEER_TB = 256
PEER_UNROLL = 32
FINAL_TM = 512


def _cparams(sem):
    return pltpu.CompilerParams(dimension_semantics=sem, vmem_limit_bytes=VMEM_LIMIT_BYTES)


def _const_spec(shape):
    nd = len(shape)
    return pl.BlockSpec(shape, lambda *_: (0,) * nd)


def _proj_kernel(x_ref, g_ref, w_ref, main_ref, a0_ref, a1_ref, a2_ref, scr_ref):
    x = x_ref[...]
    ms = jnp.mean(x * x, axis=-1, keepdims=True)
    xn = (x * lax.rsqrt(ms + EPS) * g_ref[...]).astype(BF16)

    def chunk(j):
        return jnp.dot(xn, w_ref[:, j * PROJ_TN:(j + 1) * PROJ_TN], preferred_element_type=F32)

    nmain = MAIN_W // PROJ_TN
    for j in range(nmain):
        main_ref[:, j * PROJ_TN:(j + 1) * PROJ_TN] = chunk(j).astype(BF16)
    for s in range(3):
        a0_ref[:, s * PROJ_TN:(s + 1) * PROJ_TN] = chunk(nmain + s).astype(BF16)
    for g, a_ref in ((1, a1_ref), (2, a2_ref)):
        d = ATT_GROUPS[g][1]
        for s in range(3):
            res = chunk(nmain + 3 * g + s)
            for c in range(PROJ_TN // 128):
                scr_ref[c] = res[:, c * 128:(c + 1) * 128]
            for r in range(d):
                piece = jnp.concatenate(
                    [scr_ref[c, pl.ds(r, PROJ_TM // d, stride=d), :] for c in range(PROJ_TN // 128)], axis=1)
                a_ref[0, r, :, s * PROJ_TN:(s + 1) * PROJ_TN] = piece.astype(BF16)


def _proj(x2, g, w, B, S):
    T = x2.shape[0]
    tiles = S // PROJ_TM
    d1, d2 = ATT_GROUPS[1][1], ATT_GROUPS[2][1]
    return pl.pallas_call(
        _proj_kernel,
        grid=(T // PROJ_TM,),
        in_specs=[
            pl.BlockSpec((PROJ_TM, D_MODEL), lambda i: (i, 0)),
            _const_spec((1, D_MODEL)),
            pl.BlockSpec((D_MODEL, IN_W), lambda i: (0, 0), pipeline_mode=pl.Buffered(1)),
        ],
        out_specs=[
            pl.BlockSpec((PROJ_TM, MAIN_W), lambda i: (i, 0)),
            pl.BlockSpec((PROJ_TM, ATT_SEC_W), lambda i: (i, 0)),
            pl.BlockSpec((1, d1, PROJ_TM // d1, ATT_SEC_W), lambda i: (i // tiles, 0, i % tiles, 0)),
            pl.BlockSpec((1, d2, PROJ_TM // d2, ATT_SEC_W), lambda i: (i // tiles, 0, i % tiles, 0)),
        ],
        out_shape=[
            jax.ShapeDtypeStruct((T, MAIN_W), BF16),
            jax.ShapeDtypeStruct((T, ATT_SEC_W), BF16),
            jax.ShapeDtypeStruct((B, d1, S // d1, ATT_SEC_W), BF16),
            jax.ShapeDtypeStruct((B, d2, S // d2, ATT_SEC_W), BF16),
        ],
        scratch_shapes=[pltpu.VMEM((PROJ_TN // 128, PROJ_TM, 128), F32)],
        compiler_params=_cparams(("parallel",)),
        name="proj",
    )(x2, g, w)


def _retention_kernel(q_ref, k_ref, v_ref, rg_ref, o_ref, state_ref, decay_ref):
    C = RET_CHUNK

    @pl.when(pl.program_id(1) == 0)
    def _():
        state_ref[...] = jnp.zeros_like(state_ref)
        pi = lax.broadcasted_iota(jnp.int32, (C, C), 0)
        pj = lax.broadcasted_iota(jnp.int32, (C, C), 1)
        diff = (pi - pj).astype(F32)
        for h in range(RET_HEADS):
            lg = math.log1p(-(2.0 ** (-5.0 - h)))
            decay_ref[h] = jnp.where(diff >= 0, jnp.exp(jnp.maximum(diff, 0.0) * lg), 0.0)

    pos = lax.broadcasted_iota(jnp.int32, (C, 1), 0).astype(F32)
    q = q_ref[0]
    k = k_ref[0]
    v = v_ref[0]
    rg = rg_ref[0]
    for h in range(RET_HEADS):
        lg = math.log1p(-(2.0 ** (-5.0 - h)))
        decay = decay_ref[h]
        w_k = jnp.exp((C - 1 - pos) * lg)
        w_q = jnp.exp((pos + 1.0) * lg)
        qh = q[:, h * RET_DK:(h + 1) * RET_DK]
        kh = k[:, h * RET_DK:(h + 1) * RET_DK].astype(F32) * (RET_DK ** -0.5)
        vh = v[:, h * RET_DV:(h + 1) * RET_DV]
        s = lax.dot_general(qh, kh.astype(BF16), (((1,), (1,)), ((), ())), preferred_element_type=F32)
        p = (s * decay).astype(BF16)
        inner = jnp.dot(p, vh, preferred_element_type=F32)
        st = state_ref[h]
        cross = jnp.dot((qh.astype(F32) * w_q).astype(BF16), st.astype(BF16), preferred_element_type=F32)
        kw = (kh * w_k).astype(BF16)
        kv = lax.dot_general(kw, vh, (((0,), (0,)), ((), ())), preferred_element_type=F32)
        state_ref[h] = math.exp(C * lg) * st + kv
        ret = inner + cross
        rn = ret * lax.rsqrt(jnp.mean(ret * ret, axis=-1, keepdims=True) + EPS)
        g = rg[:, h * RET_DV:(h + 1) * RET_DV].astype(F32)
        o_ref[0, :, h * RET_DV:(h + 1) * RET_DV] = (g * jax.nn.sigmoid(g) * rn).astype(BF16)


def _retention(proj3):
    B, S, _ = proj3.shape
    n = S // RET_CHUNK
    return pl.pallas_call(
        _retention_kernel,
        grid=(B, n),
        in_specs=[
            pl.BlockSpec((1, RET_CHUNK, RET_QK_W), lambda b, c: (b, c, OFF_RQ // RET_QK_W)),
            pl.BlockSpec((1, RET_CHUNK, RET_QK_W), lambda b, c: (b, c, OFF_RK // RET_QK_W)),
            pl.BlockSpec((1, RET_CHUNK, RET_V_W), lambda b, c: (b, c, OFF_RV // RET_V_W)),
            pl.BlockSpec((1, RET_CHUNK, RET_V_W), lambda b, c: (b, c, OFF_RG // RET_V_W)),
        ],
        out_specs=pl.BlockSpec((1, RET_CHUNK, RET_V_W), lambda b, c: (b, c, 0)),
        out_shape=jax.ShapeDtypeStruct((B, S, RET_V_W), BF16),
        scratch_shapes=[pltpu.VMEM((RET_HEADS, RET_DK, RET_DV), F32),
                        pltpu.VMEM((RET_HEADS, RET_CHUNK, RET_CHUNK), F32)],
        compiler_params=_cparams(("parallel", "arbitrary")),
        name="retention",
    )(proj3, proj3, proj3, proj3)


def _att_kernel(q_ref, kp_ref, kc_ref, vp_ref, vc_ref, o_ref, lse_ref, *, group, dilation, span):
    n = pl.program_id(1)
    r = pl.program_id(2)
    iq = lax.broadcasted_iota(jnp.int32, (ATT_BLOCK, 2 * ATT_BLOCK), 0)
    jk = lax.broadcasted_iota(jnp.int32, (ATT_BLOCK, 2 * ATT_BLOCK), 1)
    dist = iq + ATT_BLOCK - jk
    valid = (dist >= 0) & (dist <= span) & ((jk >= ATT_BLOCK) | (n > 0))
    distf = (dilation * dist).astype(F32)
    rows = pl.ds(r, ATT_BLOCK, stride=dilation) if dilation > 1 else slice(None)
    for bb in range(ATT_BATCH):
        q = q_ref[bb, 0]
        kk = jnp.concatenate([kp_ref[bb, 0], kc_ref[bb, 0]], axis=0)
        vv = jnp.concatenate([vp_ref[bb, 0], vc_ref[bb, 0]], axis=0)
        for i in range(ATT_HPG):
            slope = 2.0 ** (-8.0 * (group * ATT_HPG + i + 1) / ATT_HEADS)
            sl = slice(i * ATT_DH, (i + 1) * ATT_DH)
            s = lax.dot_general(q[:, sl], kk[:, sl], (((1,), (1,)), ((), ())), preferred_element_type=F32)
            s = s * (ATT_DH ** -0.5)
            s = jnp.where(valid, s - slope * distf, -jnp.inf)
            m = jnp.max(s, axis=-1, keepdims=True)
            p = jnp.exp(s - m)
            den = jnp.sum(p, axis=-1, keepdims=True)
            o_ref[i, bb, rows, :] = jnp.dot(p.astype(BF16), vv[:, sl], preferred_element_type=F32) / den
            lse_ref[i, bb, rows, :] = jnp.broadcast_to(m + jnp.log(den), (ATT_BLOCK, ATT_DH))


def _att_group(qkv, group):
    window, d = ATT_GROUPS[group]
    B, _, L, _ = qkv.shape
    S = L * d
    nL = L // ATT_BLOCK

    def cur(sec):
        return pl.BlockSpec((ATT_BATCH, 1, ATT_BLOCK, ATT_OUT_W), lambda b, n, r: (b, r, n, sec))

    def prev(sec):
        return pl.BlockSpec((ATT_BATCH, 1, ATT_BLOCK, ATT_OUT_W), lambda b, n, r: (b, r, jnp.maximum(n - 1, 0), sec))

    out_spec = pl.BlockSpec((ATT_HPG, ATT_BATCH, ATT_BLOCK * d, ATT_DH), lambda b, n, r: (0, b, n, 0))
    out_shape = jax.ShapeDtypeStruct((ATT_HPG, B, S, ATT_DH), F32)
    o, lse = pl.pallas_call(
        functools.partial(_att_kernel, group=group, dilation=d, span=window // d),
        grid=(B // ATT_BATCH, nL, d),
        in_specs=[cur(0), prev(1), cur(1), prev(2), cur(2)],
        out_specs=[out_spec, out_spec],
        out_shape=[out_shape, out_shape],
        compiler_params=_cparams(("parallel", "arbitrary", "arbitrary")),
        name=f"att_g{group}",
    )(qkv, qkv, qkv, qkv, qkv)
    return o.reshape(ATT_HPG, B * S, ATT_DH), lse.reshape(ATT_HPG, B * S, ATT_DH)


def _mix_kernel(retg_ref, o1_ref, o2_ref, o3_ref, l1_ref, l2_ref, l3_ref, gr_ref, ga_ref, x_ref,
                wro_ref, wao_ref, wo_ref, g2_ref, wq_ref, sk_ref, h_ref, xn_ref, sc_ref):
    heads = lambda ref: jnp.concatenate([ref[i] for i in range(ATT_HPG)], axis=1)
    l1, l2, l3 = heads(l1_ref), heads(l2_ref), heads(l3_ref)
    mx = jnp.maximum(jnp.maximum(l1, l2), l3)
    e1, e2, e3 = jnp.exp(l1 - mx), jnp.exp(l2 - mx), jnp.exp(l3 - mx)
    att = (e1 * heads(o1_ref) + e2 * heads(o2_ref) + e3 * heads(o3_ref)) / (e1 + e2 + e3)
    a_branch = jnp.dot(att.astype(BF16), wao_ref[...], preferred_element_type=F32)
    r_branch = jnp.dot(retg_ref[...], wro_ref[...], preferred_element_type=F32)
    merged = (jax.nn.sigmoid(gr_ref[...].astype(F32)) * r_branch
              + jax.nn.sigmoid(ga_ref[...].astype(F32)) * a_branch)
    h = x_ref[...] + jnp.dot(merged.astype(BF16), wo_ref[...], preferred_element_type=F32)
    h_ref[...] = h
    xn = h * lax.rsqrt(jnp.mean(h * h, axis=-1, keepdims=True) + EPS) * g2_ref[...]
    for c in range(D_MODEL // 128):
        xn_ref[pl.ds(c, h.shape[0], stride=D_MODEL // 128), :] = xn[:, c * 128:(c + 1) * 128]
    qp = jnp.dot(xn.astype(BF16), wq_ref[...], preferred_element_type=F32).astype(BF16)
    half = PEER_DKEY // 2
    for hp in range(2 * PEER_HEADS):
        sc_ref[hp] = lax.dot_general(sk_ref[hp], qp[:, hp * half:(hp + 1) * half],
                                     (((1,), (1,)), ((), ())), preferred_element_type=F32)


def _mix(retg, o1, o2, o3, l1, l2, l3, proj, x2, wro, wao, wo, g2, wq, sk):
    T = x2.shape[0]
    tm = MIX_TM
    row = lambda w: pl.BlockSpec((tm, w), lambda i: (i, 0))
    hd = pl.BlockSpec((ATT_HPG, tm, ATT_DH), lambda i: (0, i, 0))
    nhp = 2 * PEER_HEADS
    nt = D_MODEL // 128
    return pl.pallas_call(
        _mix_kernel,
        grid=(T // tm,),
        in_specs=[
            row(RET_V_W), hd, hd, hd, hd, hd, hd,
            pl.BlockSpec((tm, D_MODEL), lambda i: (i, OFF_GR // D_MODEL)),
            pl.BlockSpec((tm, D_MODEL), lambda i: (i, OFF_GA // D_MODEL)),
            row(D_MODEL),
            _const_spec(wro.shape), _const_spec(wao.shape), _const_spec(wo.shape),
            _const_spec(g2.shape), _const_spec(wq.shape), _const_spec(sk.shape),
        ],
        out_specs=[row(D_MODEL), pl.BlockSpec((tm * nt, 128), lambda i: (i, 0)),
                   pl.BlockSpec((nhp, PEER_NKEYS, tm), lambda i: (0, 0, i))],
        out_shape=[jax.ShapeDtypeStruct((T, D_MODEL), F32),
                   jax.ShapeDtypeStruct((T * nt, 128), F32),
                   jax.ShapeDtypeStruct((nhp, PEER_NKEYS, T), F32)],
        compiler_params=_cparams(("parallel",)),
        name="mix",
    )(retg, o1, o2, o3, l1, l2, l3, proj, proj, x2, wro, wao, wo, g2, wq, sk)


def _topk_rows(s, ids, k, fill):
    vals, idxs = [], []
    for _ in range(k):
        m = jnp.max(s, axis=0, keepdims=True)
        i = jnp.min(jnp.where(s == m, ids, fill), axis=0, keepdims=True)
        vals.append(m)
        idxs.append(i)
        s = jnp.where(ids == i, -jnp.inf, s)
    return jnp.concatenate(vals, axis=0), jnp.concatenate(idxs, axis=0)


_CAND_B = [PEER_TOPK // (a + 1) for a in range(PEER_TOPK)]
_CAND_ROWS = -(-sum(_CAND_B) // 8) * 8


def _topk_kernel(sc_ref, off_ref, par_ref, gate_ref, e_scr, g_scr):
    K = PEER_TOPK
    W = sc_ref.shape[-1]
    key_ids = lax.broadcasted_iota(jnp.int32, (PEER_NKEYS, W), 0)
    npad = _CAND_ROWS - sum(_CAND_B)
    cand_ids = jnp.concatenate(
        [jnp.full((1, W), a * K + b, jnp.int32) for a in range(K) for b in range(_CAND_B[a])]
        + [jnp.full((npad, W), K * K, jnp.int32)], axis=0)
    pad = jnp.full((npad, W), -jnp.inf, F32)

    def head(h, carry):
        v0, i0 = _topk_rows(sc_ref[2 * h], key_ids, K, PEER_NKEYS)
        v1, i1 = _topk_rows(sc_ref[2 * h + 1], key_ids, K, PEER_NKEYS)
        cand = jnp.concatenate([v0[a:a + 1] + v1[0:_CAND_B[a]] for a in range(K)] + [pad], axis=0)
        best_s, best_c = _topk_rows(cand, cand_ids, K, K * K)
        ca = best_c // K
        cb = best_c % K
        ia = jnp.zeros_like(best_c)
        ib = jnp.zeros_like(best_c)
        for a in range(K):
            ia = jnp.where(ca == a, i0[a:a + 1], ia)
            ib = jnp.where(cb == a, i1[a:a + 1], ib)
        e_scr[h] = (ia * PEER_NKEYS + ib).astype(F32)
        ex = jnp.exp(best_s - best_s[0:1])
        g_scr[h] = ex / jnp.sum(ex, axis=0, keepdims=True)
        return carry

    lax.fori_loop(0, PEER_HEADS, head, 0)
    e = e_scr[...].reshape(PEER_PICKS, W).T.astype(jnp.int32)
    off_ref[...] = lax.shift_right_logical(e, 1) * 8
    par_ref[...] = (e & 1).astype(F32)
    gate_ref[...] = g_scr[...].reshape(PEER_PICKS, W).T


def _topk(scT):
    T = scT.shape[-1]
    tb = TOPK_TB
    out_spec = pl.BlockSpec((tb, PEER_PICKS), lambda i: (i, 0))
    return pl.pallas_call(
        _topk_kernel,
        grid=(T // tb,),
        in_specs=[pl.BlockSpec((2 * PEER_HEADS, PEER_NKEYS, tb), lambda i: (0, 0, i))],
        out_specs=[out_spec, out_spec, out_spec],
        out_shape=[jax.ShapeDtypeStruct((T, PEER_PICKS), jnp.int32),
                   jax.ShapeDtypeStruct((T, PEER_PICKS), F32),
                   jax.ShapeDtypeStruct((T, PEER_PICKS), F32)],
        scratch_shapes=[pltpu.VMEM((PEER_HEADS, PEER_TOPK, tb), F32),
                        pltpu.VMEM((PEER_HEADS, PEER_TOPK, tb), F32)],
        compiler_params=_cparams(("parallel",)),
        name="topk",
    )(scT)


PEER_U_VALU = 64
PEER_V_VALU = 16


PACK_TE = 1024


def _pack_kernel(x_ref, o_ref):
    s = pl.program_id(1)
    half = PACK_TE // 2
    bits = lambda v: pltpu.bitcast(v.astype(BF16).astype(F32), jnp.uint32)
    even = bits(x_ref[pl.ds(0, half, stride=2), :])
    odd = bits(x_ref[pl.ds(1, half, stride=2), :])
    o_ref[pl.ds(s, half, stride=D_MODEL // 128), :] = (even >> 16) | (odd & jnp.uint32(0xFFFF0000))


def _pack_table(tab):
    E = tab.shape[0]
    return pl.pallas_call(
        _pack_kernel,
        grid=(E // PACK_TE, D_MODEL // 128),
        in_specs=[pl.BlockSpec((PACK_TE, 128), lambda i, s: (i, s))],
        out_specs=pl.BlockSpec((4 * PACK_TE, 128), lambda i, s: (i, 0)),
        out_shape=jax.ShapeDtypeStruct((4 * E, 128), jnp.uint32),
        compiler_params=_cparams(("parallel", "arbitrary")),
        name="pack_table",
    )(tab)


def _split3(a):
    p0 = a.astype(BF16)
    r1 = a - p0.astype(F32)
    p1 = r1.astype(BF16)
    p2 = (r1 - p1.astype(F32)).astype(BF16)
    return p0, p1, p2


def _pair_expand(v, which, base, pairs):
    j = lax.broadcasted_iota(jnp.int32, (PEER_PICKS, 16 * pairs), 0)
    k = lax.broadcasted_iota(jnp.int32, (PEER_PICKS, 16 * pairs), 1)
    onehot = jnp.where(j == base + 2 * (k // 16) + which, 1.0, 0.0).astype(BF16)
    return sum(jnp.dot(p, onehot, preferred_element_type=F32) for p in _split3(v))


def _pair_targets(par, which, base, pairs):
    r = lax.broadcasted_iota(jnp.int32, (par.shape[0], 16 * pairs), 1) % 16
    return 0.5 * (r.astype(F32) - _pair_expand(par, which, base, pairs))


def _tile_words(tab_ref, off_ref, t, j):
    off = pl.multiple_of(off_ref[t, j], 8)
    return tab_ref[pl.ds(off, 8), :]


def _fill_lane_broadcasts(dst_ref, src, n):
    for j in range(n):
        dst_ref[j] = jnp.broadcast_to(src[:, j:j + 1], dst_ref.shape[1:])


def _expert_rows(words, shv):
    return pltpu.bitcast((words << shv) & jnp.uint32(0xFFFF0000), F32)


def _sublane_sums(ps):
    sub = lax.broadcasted_iota(jnp.int32, (8, 128), 0)
    lo4 = sub < 4
    t = [jnp.where(lo4, ps[j], ps[j + 4]) + pltpu.roll(jnp.where(lo4, ps[j + 4], ps[j]), 4, 0)
         for j in range(4)]
    m2 = (sub & 2) == 0
    u = [jnp.where(m2, t[j] + pltpu.roll(t[j], 6, 0), t[j + 2] + pltpu.roll(t[j + 2], 2, 0))
         for j in range(2)]
    m1 = (sub & 1) == 0
    return jnp.where(m1, u[0] + pltpu.roll(u[0], 7, 0), u[1] + pltpu.roll(u[1], 1, 0))


def _token_loop(tb, body):
    def group(i, carry):
        for s in range(PEER_UNROLL):
            body(i * PEER_UNROLL + s)
        return carry

    lax.fori_loop(0, tb // PEER_UNROLL, group, 0)


def _peer_u_kernel(off_ref, par_ref, x_ref, gate_ref, tab_ref, w_ref, tga_ref, tgb_ref, v3_ref, shb_ref, actv_ref):
    tb = x_ref.shape[0]
    nv = PEER_U_VALU
    pairs = (PEER_PICKS - nv) // 2
    nblk = 16 * pairs // 128
    par = par_ref[...]
    tga_ref[...] = _pair_targets(par, 0, nv, pairs)
    tgb_ref[...] = _pair_targets(par, 1, nv, pairs)
    _fill_lane_broadcasts(shb_ref, ((1.0 - par) * 16.0).astype(jnp.uint32), nv)
    m_iota = lax.broadcasted_iota(jnp.int32, (8, 16 * pairs), 0).astype(F32)
    zero = jnp.zeros((8, 128), F32)
    ones = jnp.ones((8, 128), BF16)
    nt = (((1,), (1,)), ((), ()))

    def body(t):
        xt = x_ref[t]
        row = pl.ds(t, 1)
        rs = []
        for g in range(nv // 8):
            ps = []
            for jj in range(8):
                j = g * 8 + jj
                shv = jnp.broadcast_to(shb_ref[j, row, :], (8, 128))
                ps.append(_expert_rows(_tile_words(tab_ref, off_ref, t, j), shv) * xt)
            rs.append(_sublane_sums(ps))
        r = jnp.concatenate(rs, axis=0)
        hi = r.astype(BF16)
        lo = (r - hi.astype(F32)).astype(BF16)
        av = (lax.dot_general(ones, hi, nt, preferred_element_type=F32)
              + lax.dot_general(ones, lo, nt, preferred_element_type=F32))
        actv_ref[row, 0:nv] = av[0:1]
        tiles = [pltpu.bitcast(_tile_words(tab_ref, off_ref, t, j), BF16) for j in range(nv, PEER_PICKS)]
        wmat = jnp.concatenate(
            [jnp.concatenate([tiles[2 * q], tiles[2 * q + 1]], axis=1) for q in range(pairs)], axis=0)
        xf = jnp.concatenate([jnp.concatenate([xt, zero], axis=1),
                              jnp.concatenate([zero, xt], axis=1)], axis=0)
        xh = xf.astype(BF16)
        xl = (xf - xh.astype(F32)).astype(BF16)
        o = lax.dot_general(jnp.concatenate([xh, xl], axis=0), wmat, nt, preferred_element_type=F32)
        o = o[0:16] + o[16:32]
        za = jnp.where(tga_ref[row, :] == m_iota, o[0:8], 0.0)
        zb = jnp.where(tgb_ref[row, :] == m_iota, o[8:16], 0.0)
        for half, z in enumerate((za, zb)):
            v = _sublane_sums([z[:, i * 128:(i + 1) * 128] for i in range(nblk)] + [zero] * (8 - nblk))
            for i in range(nblk):
                v3_ref[half * nblk + i, row, :] = v[i:i + 1]

    _token_loop(tb, body)
    lane = lax.broadcasted_iota(jnp.int32, (128, PEER_PICKS), 0)
    pick = lax.broadcasted_iota(jnp.int32, (128, PEER_PICKS), 1)
    act = jnp.zeros((tb, PEER_PICKS), F32)
    for c in range(2 * nblk):
        sel = jnp.where(pick == nv + 2 * (8 * (c % nblk) + lane // 16) + c // nblk, 1.0, 0.0).astype(BF16)
        for p in _split3(v3_ref[c]):
            act = act + jnp.dot(p, sel, preferred_element_type=F32)
    act = jnp.where(pick[0:1] < nv, actv_ref[...], act)
    gelu = 0.5 * act * (1.0 + lax.erf(act * (2.0 ** -0.5)))
    w_ref[...] = gate_ref[...] * gelu


def _peer_specs(tb, tab):
    rows = pl.BlockSpec((tb, PEER_PICKS), lambda i: (i, 0))
    smem = pl.BlockSpec((tb, PEER_PICKS), lambda i: (i, 0), memory_space=pltpu.SMEM)
    table = pl.BlockSpec(tab.shape, lambda i: (0, 0), pipeline_mode=pl.Buffered(1))
    return rows, smem, table


def _peer_u(off, par, x3, gate, tab):
    T = x3.shape[0]
    tb = PEER_TB
    rows, smem, table = _peer_specs(tb, tab)
    pairs = (PEER_PICKS - PEER_U_VALU) // 2
    return pl.pallas_call(
        _peer_u_kernel,
        grid=(T // tb,),
        in_specs=[smem, rows, pl.BlockSpec((tb, 8, 128), lambda i: (i, 0, 0)), rows, table],
        out_specs=rows,
        out_shape=jax.ShapeDtypeStruct((T, PEER_PICKS), F32),
        scratch_shapes=[pltpu.VMEM((tb, 16 * pairs), F32), pltpu.VMEM((tb, 16 * pairs), F32),
                        pltpu.VMEM((pairs // 4, tb, 128), F32),
                        pltpu.VMEM((PEER_U_VALU, tb, 128), jnp.uint32),
                        pltpu.VMEM((tb, PEER_PICKS), F32)],
        compiler_params=_cparams(("parallel",)),
        name="peer_u",
    )(off, par, x3, gate, tab)


def _peer_v_kernel(off_ref, par_ref, w_ref, tab_ref, o_ref, wa_ref, wb_ref, tga_ref, tgb_ref, shb_ref, wbc_ref):
    tb = o_ref.shape[0]
    nv = PEER_V_VALU
    pairs = (PEER_PICKS - nv) // 2
    par = par_ref[...]
    w = w_ref[...]
    wa_ref[...] = _pair_expand(w, 0, nv, pairs)
    wb_ref[...] = _pair_expand(w, 1, nv, pairs)
    tga_ref[...] = _pair_targets(par, 0, nv, pairs)
    tgb_ref[...] = _pair_targets(par, 1, nv, pairs)
    _fill_lane_broadcasts(shb_ref, ((1.0 - par) * 16.0).astype(jnp.uint32), nv)
    _fill_lane_broadcasts(wbc_ref, w, nv)
    m_iota = lax.broadcasted_iota(jnp.int32, (8, 16 * pairs), 0).astype(F32)
    nacc = 4

    def body(t):
        row = pl.ds(t, 1)
        acc = [jnp.zeros((8, 128), F32) for _ in range(nacc)]
        for j in range(nv):
            shv = jnp.broadcast_to(shb_ref[j, row, :], (8, 128))
            wv = jnp.broadcast_to(wbc_ref[j, row, :], (8, 128))
            acc[j % nacc] = acc[j % nacc] + wv * _expert_rows(_tile_words(tab_ref, off_ref, t, j), shv)
        tiles = [pltpu.bitcast(_tile_words(tab_ref, off_ref, t, j), BF16) for j in range(nv, PEER_PICKS)]
        vmat = jnp.concatenate([jnp.concatenate(tiles[0::2], axis=0),
                                jnp.concatenate(tiles[1::2], axis=0)], axis=1)
        la = jnp.where(tga_ref[row, :] == m_iota, wa_ref[row, :], 0.0)
        lb = jnp.where(tgb_ref[row, :] == m_iota, wb_ref[row, :], 0.0)
        lf = jnp.concatenate([la, lb], axis=0)
        lh = lf.astype(BF16)
        ll = (lf - lh.astype(F32)).astype(BF16)
        o = jnp.dot(jnp.concatenate([lh, ll], axis=0), vmat, preferred_element_type=F32)
        o_ref[t] = (((o[0:8, 0:128] + o[8:16, 128:256]) + (o[16:24, 0:128] + o[24:32, 128:256]))
                    + ((acc[0] + acc[1]) + (acc[2] + acc[3])))

    _token_loop(tb, body)


def _peer_v(off, par, w, tab):
    T = par.shape[0]
    tb = PEER_TB
    rows, smem, table = _peer_specs(tb, tab)
    pairs = (PEER_PICKS - PEER_V_VALU) // 2
    return pl.pallas_call(
        _peer_v_kernel,
        grid=(T // tb,),
        in_specs=[smem, rows, rows, table],
        out_specs=pl.BlockSpec((tb, 8, 128), lambda i: (i, 0, 0)),
        out_shape=jax.ShapeDtypeStruct((T, 8, 128), F32),
        scratch_shapes=[pltpu.VMEM((tb, 16 * pairs), F32) for _ in range(4)]
        + [pltpu.VMEM((PEER_V_VALU, tb, 128), jnp.uint32), pltpu.VMEM((PEER_V_VALU, tb, 128), F32)],
        compiler_params=_cparams(("parallel",)),
        name="peer_v",
    )(off, par, w, tab)


def _final_kernel(h_ref, p_ref, g_ref, o_ref):
    nt = D_MODEL // 128
    tm = h_ref.shape[0]
    p = jnp.concatenate([p_ref[pl.ds(c, tm, stride=nt), :] for c in range(nt)], axis=1)
    h = h_ref[...] + p
    o_ref[...] = h * lax.rsqrt(jnp.mean(h * h, axis=-1, keepdims=True) + EPS) * g_ref[...]


def _final(h, p, g):
    T = h.shape[0]
    row = pl.BlockSpec((FINAL_TM, D_MODEL), lambda i: (i, 0))
    return pl.pallas_call(
        _final_kernel,
        grid=(T // FINAL_TM,),
        in_specs=[row, pl.BlockSpec((FINAL_TM * (D_MODEL // 128), 128), lambda i: (i, 0)),
                  _const_spec((1, D_MODEL))],
        out_specs=row,
        out_shape=jax.ShapeDtypeStruct((T, D_MODEL), F32),
        compiler_params=_cparams(("parallel",)),
        name="final_norm",
    )(h, p, g)


def _layer(h2, B, S, norm1_g, w_in, w_ret_out, w_att_out, w_out, norm2_g, peer_wq, peer_subkeys, peer_u, peer_v):
    T = B * S
    aq, ak, av = 3072, 3072 + ATT_W, 3072 + 2 * ATT_W
    att_cols = [w_in[:, o + g * ATT_OUT_W:o + (g + 1) * ATT_OUT_W] for g in range(len(ATT_GROUPS)) for o in (aq, ak, av)]
    w_perm = jnp.concatenate([w_in[:, :3072], w_in[:, 7680:9728]] + att_cols, axis=1).astype(BF16)
    main, a0, a1, a2 = _proj(h2, norm1_g.reshape(1, D_MODEL), w_perm, B, S)
    retg = _retention(main.reshape(B, S, MAIN_W)).reshape(T, RET_V_W)
    (o1, l1), (o2, l2), (o3, l3) = (_att_group(a, g) for g, a in enumerate((a0.reshape(B, 1, S, ATT_SEC_W), a1, a2)))
    sk = peer_subkeys.reshape(2 * PEER_HEADS, PEER_NKEYS, PEER_DKEY // 2).astype(BF16)
    h_mid, xn2, scT = _mix(retg, o1, o2, o3, l1, l2, l3, main, h2,
                           w_ret_out.astype(BF16), w_att_out.astype(BF16), w_out.astype(BF16),
                           norm2_g.reshape(1, D_MODEL), peer_wq.astype(BF16), sk)
    off, par, gate = _topk(scT)
    w = _peer_u(off, par, xn2.reshape(T, 8, 128), gate, _pack_table(peer_u))
    pout = _peer_v(off, par, w, _pack_table(peer_v))
    return h_mid, pout.reshape(T * 8, 128)


def kernel(x, norm1_g, w_in, w_ret_out, w_att_out, w_out, norm2_g, peer_wq, peer_subkeys, peer_u, peer_v, normf_g):
    B, S, D = x.shape
    assert w_in.shape[0] == 1, "single-layer block"
    h, pout = _layer(x.reshape(B * S, D), B, S, norm1_g[0], w_in[0], w_ret_out[0], w_att_out[0], w_out[0],
                     norm2_g[0], peer_wq[0], peer_subkeys[0], peer_u[0], peer_v[0])
    return _final(h, pout, normf_g.reshape(1, D)).reshape(B, S, D)
```

```python
import functools
import math

import jax
import jax.numpy as jnp
from jax import lax
from jax.experimental import pallas as pl
from jax.experimental.pallas import tpu as pltpu

F32 = jnp.float32
BF16 = jnp.bfloat16

D_MODEL = 1024
RET_HEADS = 8
RET_DK = 64
RET_DV = 128
RET_CHUNK = 128
ATT_GROUPS = ((128, 1), (512, 4), (2048, 16))
ATT_HPG = 4
ATT_HEADS = 12
ATT_DH = 128
ATT_BLOCK = 128
ATT_BATCH = 2
PEER_HEADS = 8
PEER_NKEYS = 128
PEER_DKEY = 256
PEER_TOPK = 16
PEER_PICKS = PEER_HEADS * PEER_TOPK
EPS = 1e-6

RET_QK_W = RET_HEADS * RET_DK
RET_V_W = RET_HEADS * RET_DV
ATT_W = ATT_HEADS * ATT_DH
ATT_OUT_W = ATT_HPG * ATT_DH
OFF_RQ, OFF_RK, OFF_RV, OFF_RG = 0, 512, 1024, 2048
OFF_GR, OFF_GA = 3072, 4096
MAIN_W = 5120
ATT_SEC_W = 3 * ATT_OUT_W
IN_W = MAIN_W + len(ATT_GROUPS) * ATT_SEC_W

VMEM_LIMIT_BYTES = 60 * 1024 * 1024

PROJ_TM = 256
PROJ_TN = 512
MIX_TM = 256
TOPK_TB = 512
PEER_TB = 256
PEER_UNROLL = 32
FINAL_TM = 512


def _cparams(sem):
    return pltpu.CompilerParams(dimension_semantics=sem, vmem_limit_bytes=VMEM_LIMIT_BYTES)


def _const_spec(shape):
    nd = len(shape)
    return pl.BlockSpec(shape, lambda *_: (0,) * nd)


def _proj_kernel(x_ref, g_ref, w_ref, main_ref, a0_ref, a1_ref, a2_ref, scr_ref):
    x = x_ref[...]
    ms = jnp.mean(x * x, axis=-1, keepdims=True)
    xn = (x * lax.rsqrt(ms + EPS) * g_ref[...]).astype(BF16)

    def chunk(j):
        return jnp.dot(xn, w_ref[:, j * PROJ_TN:(j + 1) * PROJ_TN], preferred_element_type=F32)

    nmain = MAIN_W // PROJ_TN
    for j in range(nmain):
        main_ref[:, j * PROJ_TN:(j + 1) * PROJ_TN] = chunk(j).astype(BF16)
    for s in range(3):
        a0_ref[:, s * PROJ_TN:(s + 1) * PROJ_TN] = chunk(nmain + s).astype(BF16)
    for g, a_ref in ((1, a1_ref), (2, a2_ref)):
        d = ATT_GROUPS[g][1]
        for s in range(3):
            res = chunk(nmain + 3 * g + s)
            for c in range(PROJ_TN // 128):
                scr_ref[c] = res[:, c * 128:(c + 1) * 128]
            for r in range(d):
                piece = jnp.concatenate(
                    [scr_ref[c, pl.ds(r, PROJ_TM // d, stride=d), :] for c in range(PROJ_TN // 128)], axis=1)
                a_ref[0, r, :, s * PROJ_TN:(s + 1) * PROJ_TN] = piece.astype(BF16)


def _proj(x2, g, w, B, S):
    T = x2.shape[0]
    tiles = S // PROJ_TM
    d1, d2 = ATT_GROUPS[1][1], ATT_GROUPS[2][1]
    return pl.pallas_call(
        _proj_kernel,
        grid=(T // PROJ_TM,),
        in_specs=[
            pl.BlockSpec((PROJ_TM, D_MODEL), lambda i: (i, 0)),
            _const_spec((1, D_MODEL)),
            pl.BlockSpec((D_MODEL, IN_W), lambda i: (0, 0), pipeline_mode=pl.Buffered(1)),
        ],
        out_specs=[
            pl.BlockSpec((PROJ_TM, MAIN_W), lambda i: (i, 0)),
            pl.BlockSpec((PROJ_TM, ATT_SEC_W), lambda i: (i, 0)),
            pl.BlockSpec((1, d1, PROJ_TM // d1, ATT_SEC_W), lambda i: (i // tiles, 0, i % tiles, 0)),
            pl.BlockSpec((1, d2, PROJ_TM // d2, ATT_SEC_W), lambda i: (i // tiles, 0, i % tiles, 0)),
        ],
        out_shape=[
            jax.ShapeDtypeStruct((T, MAIN_W), BF16),
            jax.ShapeDtypeStruct((T, ATT_SEC_W), BF16),
            jax.ShapeDtypeStruct((B, d1, S // d1, ATT_SEC_W), BF16),
            jax.ShapeDtypeStruct((B, d2, S // d2, ATT_SEC_W), BF16),
        ],
        scratch_shapes=[pltpu.VMEM((PROJ_TN // 128, PROJ_TM, 128), F32)],
        compiler_params=_cparams(("parallel",)),
        name="proj",
    )(x2, g, w)


def _retention_kernel(q_ref, k_ref, v_ref, rg_ref, o_ref, state_ref, decay_ref):
    C = RET_CHUNK

    @pl.when(pl.program_id(1) == 0)
    def _():
        state_ref[...] = jnp.zeros_like(state_ref)
        pi = lax.broadcasted_iota(jnp.int32, (C, C), 0)
        pj = lax.broadcasted_iota(jnp.int32, (C, C), 1)
        diff = (pi - pj).astype(F32)
        for h in range(RET_HEADS):
            lg = math.log1p(-(2.0 ** (-5.0 - h)))
            decay_ref[h] = jnp.where(diff >= 0, jnp.exp(jnp.maximum(diff, 0.0) * lg), 0.0)

    pos = lax.broadcasted_iota(jnp.int32, (C, 1), 0).astype(F32)
    q = q_ref[0]
    k = k_ref[0]
    v = v_ref[0]
    rg = rg_ref[0]
    for h in range(RET_HEADS):
        lg = math.log1p(-(2.0 ** (-5.0 - h)))
        decay = decay_ref[h]
        w_k = jnp.exp((C - 1 - pos) * lg)
        w_q = jnp.exp((pos + 1.0) * lg)
        qh = q[:, h * RET_DK:(h + 1) * RET_DK]
        kh = k[:, h * RET_DK:(h + 1) * RET_DK].astype(F32) * (RET_DK ** -0.5)
        vh = v[:, h * RET_DV:(h + 1) * RET_DV]
        s = lax.dot_general(qh, kh.astype(BF16), (((1,), (1,)), ((), ())), preferred_element_type=F32)
        p = (s * decay).astype(BF16)
        inner = jnp.dot(p, vh, preferred_element_type=F32)
        st = state_ref[h]
        cross = jnp.dot((qh.astype(F32) * w_q).astype(BF16), st.astype(BF16), preferred_element_type=F32)
        kw = (kh * w_k).astype(BF16)
        kv = lax.dot_general(kw, vh, (((0,), (0,)), ((), ())), preferred_element_type=F32)
        state_ref[h] = math.exp(C * lg) * st + kv
        ret = inner + cross
        rn = ret * lax.rsqrt(jnp.mean(ret * ret, axis=-1, keepdims=True) + EPS)
        g = rg[:, h * RET_DV:(h + 1) * RET_DV].astype(F32)
        o_ref[0, :, h * RET_DV:(h + 1) * RET_DV] = (g * jax.nn.sigmoid(g) * rn).astype(BF16)


def _retention(proj3):
    B, S, _ = proj3.shape
    n = S // RET_CHUNK
    return pl.pallas_call(
        _retention_kernel,
        grid=(B, n),
        in_specs=[
            pl.BlockSpec((1, RET_CHUNK, RET_QK_W), lambda b, c: (b, c, OFF_RQ // RET_QK_W)),
            pl.BlockSpec((1, RET_CHUNK, RET_QK_W), lambda b, c: (b, c, OFF_RK // RET_QK_W)),
            pl.BlockSpec((1, RET_CHUNK, RET_V_W), lambda b, c: (b, c, OFF_RV // RET_V_W)),
            pl.BlockSpec((1, RET_CHUNK, RET_V_W), lambda b, c: (b, c, OFF_RG // RET_V_W)),
        ],
        out_specs=pl.BlockSpec((1, RET_CHUNK, RET_V_W), lambda b, c: (b, c, 0)),
        out_shape=jax.ShapeDtypeStruct((B, S, RET_V_W), BF16),
        scratch_shapes=[pltpu.VMEM((RET_HEADS, RET_DK, RET_DV), F32),
                        pltpu.VMEM((RET_HEADS, RET_CHUNK, RET_CHUNK), F32)],
        compiler_params=_cparams(("parallel", "arbitrary")),
        name="retention",
    )(proj3, proj3, proj3, proj3)


def _att_kernel(q_ref, kp_ref, kc_ref, vp_ref, vc_ref, o_ref, lse_ref, *, group, dilation, span):
    n = pl.program_id(1)
    r = pl.program_id(2)
    iq = lax.broadcasted_iota(jnp.int32, (ATT_BLOCK, 2 * ATT_BLOCK), 0)
    jk = lax.broadcasted_iota(jnp.int32, (ATT_BLOCK, 2 * ATT_BLOCK), 1)
    dist = iq + ATT_BLOCK - jk
    valid = (dist >= 0) & (dist <= span) & ((jk >= ATT_BLOCK) | (n > 0))
    distf = (dilation * dist).astype(F32)
    rows = pl.ds(r, ATT_BLOCK, stride=dilation) if dilation > 1 else slice(None)
    for bb in range(ATT_BATCH):
        q = q_ref[bb, 0]
        kk = jnp.concatenate([kp_ref[bb, 0], kc_ref[bb, 0]], axis=0)
        vv = jnp.concatenate([vp_ref[bb, 0], vc_ref[bb, 0]], axis=0)
        for i in range(ATT_HPG):
            slope = 2.0 ** (-8.0 * (group * ATT_HPG + i + 1) / ATT_HEADS)
            sl = slice(i * ATT_DH, (i + 1) * ATT_DH)
            s = lax.dot_general(q[:, sl], kk[:, sl], (((1,), (1,)), ((), ())), preferred_element_type=F32)
            s = s * (ATT_DH ** -0.5)
            s = jnp.where(valid, s - slope * distf, -jnp.inf)
            m = jnp.max(s, axis=-1, keepdims=True)
            p = jnp.exp(s - m)
            den = jnp.sum(p, axis=-1, keepdims=True)
            o_ref[i, bb, rows, :] = jnp.dot(p.astype(BF16), vv[:, sl], preferred_element_type=F32) / den
            lse_ref[i, bb, rows, :] = jnp.broadcast_to(m + jnp.log(den), (ATT_BLOCK, ATT_DH))


def _att_group(qkv, group):
    window, d = ATT_GROUPS[group]
    B, _, L, _ = qkv.shape
    S = L * d
    nL = L // ATT_BLOCK

    def cur(sec):
        return pl.BlockSpec((ATT_BATCH, 1, ATT_BLOCK, ATT_OUT_W), lambda b, n, r: (b, r, n, sec))

    def prev(sec):
        return pl.BlockSpec((ATT_BATCH, 1, ATT_BLOCK, ATT_OUT_W), lambda b, n, r: (b, r, jnp.maximum(n - 1, 0), sec))

    out_spec = pl.BlockSpec((ATT_HPG, ATT_BATCH, ATT_BLOCK * d, ATT_DH), lambda b, n, r: (0, b, n, 0))
    out_shape = jax.ShapeDtypeStruct((ATT_HPG, B, S, ATT_DH), F32)
    o, lse = pl.pallas_call(
        functools.partial(_att_kernel, group=group, dilation=d, span=window // d),
        grid=(B // ATT_BATCH, nL, d),
        in_specs=[cur(0), prev(1), cur(1), prev(2), cur(2)],
        out_specs=[out_spec, out_spec],
        out_shape=[out_shape, out_shape],
        compiler_params=_cparams(("parallel", "arbitrary", "arbitrary")),
        name=f"att_g{group}",
    )(qkv, qkv, qkv, qkv, qkv)
    return o.reshape(ATT_HPG, B * S, ATT_DH), lse.reshape(ATT_HPG, B * S, ATT_DH)


def _mix_kernel(retg_ref, o1_ref, o2_ref, o3_ref, l1_ref, l2_ref, l3_ref, gr_ref, ga_ref, x_ref,
                wro_ref, wao_ref, wo_ref, g2_ref, wq_ref, sk_ref, h_ref, xn_ref, sc_ref):
    heads = lambda ref: jnp.concatenate([ref[i] for i in range(ATT_HPG)], axis=1)
    l1, l2, l3 = heads(l1_ref), heads(l2_ref), heads(l3_ref)
    mx = jnp.maximum(jnp.maximum(l1, l2), l3)
    e1, e2, e3 = jnp.exp(l1 - mx), jnp.exp(l2 - mx), jnp.exp(l3 - mx)
    att = (e1 * heads(o1_ref) + e2 * heads(o2_ref) + e3 * heads(o3_ref)) / (e1 + e2 + e3)
    a_branch = jnp.dot(att.astype(BF16), wao_ref[...], preferred_element_type=F32)
    r_branch = jnp.dot(retg_ref[...], wro_ref[...], preferred_element_type=F32)
    merged = (jax.nn.sigmoid(gr_ref[...].astype(F32)) * r_branch
              + jax.nn.sigmoid(ga_ref[...].astype(F32)) * a_branch)
    h = x_ref[...] + jnp.dot(merged.astype(BF16), wo_ref[...], preferred_element_type=F32)
    h_ref[...] = h
    xn = h * lax.rsqrt(jnp.mean(h * h, axis=-1, keepdims=True) + EPS) * g2_ref[...]
    for c in range(D_MODEL // 128):
        xn_ref[pl.ds(c, h.shape[0], stride=D_MODEL // 128), :] = xn[:, c * 128:(c + 1) * 128]
    qp = jnp.dot(xn.astype(BF16), wq_ref[...], preferred_element_type=F32).astype(BF16)
    half = PEER_DKEY // 2
    for hp in range(2 * PEER_HEADS):
        sc_ref[hp] = lax.dot_general(sk_ref[hp], qp[:, hp * half:(hp + 1) * half],
                                     (((1,), (1,)), ((), ())), preferred_element_type=F32)


def _mix(retg, o1, o2, o3, l1, l2, l3, proj, x2, wro, wao, wo, g2, wq, sk):
    T = x2.shape[0]
    tm = MIX_TM
    row = lambda w: pl.BlockSpec((tm, w), lambda i: (i, 0))
    hd = pl.BlockSpec((ATT_HPG, tm, ATT_DH), lambda i: (0, i, 0))
    nhp = 2 * PEER_HEADS
    nt = D_MODEL // 128
    return pl.pallas_call(
        _mix_kernel,
        grid=(T // tm,),
        in_specs=[
            row(RET_V_W), hd, hd, hd, hd, hd, hd,
            pl.BlockSpec((tm, D_MODEL), lambda i: (i, OFF_GR // D_MODEL)),
            pl.BlockSpec((tm, D_MODEL), lambda i: (i, OFF_GA // D_MODEL)),
            row(D_MODEL),
            _const_spec(wro.shape), _const_spec(wao.shape), _const_spec(wo.shape),
            _const_spec(g2.shape), _const_spec(wq.shape), _const_spec(sk.shape),
        ],
        out_specs=[row(D_MODEL), pl.BlockSpec((tm * nt, 128), lambda i: (i, 0)),
                   pl.BlockSpec((nhp, PEER_NKEYS, tm), lambda i: (0, 0, i))],
        out_shape=[jax.ShapeDtypeStruct((T, D_MODEL), F32),
                   jax.ShapeDtypeStruct((T * nt, 128), F32),
                   jax.ShapeDtypeStruct((nhp, PEER_NKEYS, T), F32)],
        compiler_params=_cparams(("parallel",)),
        name="mix",
    )(retg, o1, o2, o3, l1, l2, l3, proj, proj, x2, wro, wao, wo, g2, wq, sk)


def _topk_rows(s, ids, k, fill):
    vals, idxs = [], []
    for _ in range(k):
        m = jnp.max(s, axis=0, keepdims=True)
        i = jnp.min(jnp.where(s == m, ids, fill), axis=0, keepdims=True)
        vals.append(m)
        idxs.append(i)
        s = jnp.where(ids == i, -jnp.inf, s)
    return jnp.concatenate(vals, axis=0), jnp.concatenate(idxs, axis=0)


_CAND_B = [PEER_TOPK // (a + 1) for a in range(PEER_TOPK)]
_CAND_ROWS = -(-sum(_CAND_B) // 8) * 8


def _topk_consts(W):
    K = PEER_TOPK
    key_ids = lax.broadcasted_iota(jnp.int32, (PEER_NKEYS, W), 0)
    npad = _CAND_ROWS - sum(_CAND_B)
    cand_ids = jnp.concatenate(
        [jnp.full((1, W), a * K + b, jnp.int32) for a in range(K) for b in range(_CAND_B[a])]
        + [jnp.full((npad, W), K * K, jnp.int32)], axis=0)
    pad = jnp.full((npad, W), -jnp.inf, F32)
    return key_ids, cand_ids, pad


def _topk_head(sc_ref, h, consts):
    K = PEER_TOPK
    key_ids, cand_ids, pad = consts
    v0, i0 = _topk_rows(sc_ref[2 * h], key_ids, K, PEER_NKEYS)
    v1, i1 = _topk_rows(sc_ref[2 * h + 1], key_ids, K, PEER_NKEYS)
    cand = jnp.concatenate([v0[a:a + 1] + v1[0:_CAND_B[a]] for a in range(K)] + [pad], axis=0)
    best_s, best_c = _topk_rows(cand, cand_ids, K, K * K)
    ca = best_c // K
    cb = best_c % K
    ia = jnp.zeros_like(best_c)
    ib = jnp.zeros_like(best_c)
    for a in range(K):
        ia = jnp.where(ca == a, i0[a:a + 1], ia)
        ib = jnp.where(cb == a, i1[a:a + 1], ib)
    ex = jnp.exp(best_s - best_s[0:1])
    return (ia * PEER_NKEYS + ib).astype(F32), ex / jnp.sum(ex, axis=0, keepdims=True)


def _topk_finish(e_scr, g_scr, off_ref, par_ref, gate_ref):
    W = e_scr.shape[-1]
    e = e_scr[...].reshape(PEER_PICKS, W).T.astype(jnp.int32)
    off_ref[...] = lax.shift_right_logical(e, 1) * 8
    par_ref[...] = (e & 1).astype(F32)
    gate_ref[...] = g_scr[...].reshape(PEER_PICKS, W).T


def _topk_kernel(sc_ref, off_ref, par_ref, gate_ref, e_scr, g_scr):
    consts = _topk_consts(sc_ref.shape[-1])

    def head(h, carry):
        e_scr[h], g_scr[h] = _topk_head(sc_ref, h, consts)
        return carry

    lax.fori_loop(0, PEER_HEADS, head, 0)
    _topk_finish(e_scr, g_scr, off_ref, par_ref, gate_ref)


def _topk(scT):
    T = scT.shape[-1]
    tb = min(TOPK_TB, T)
    out_spec = pl.BlockSpec((tb, PEER_PICKS), lambda i: (i, 0))
    return pl.pallas_call(
        _topk_kernel,
        grid=(T // tb,),
        in_specs=[pl.BlockSpec((2 * PEER_HEADS, PEER_NKEYS, tb), lambda i: (0, 0, i))],
        out_specs=[out_spec, out_spec, out_spec],
        out_shape=[jax.ShapeDtypeStruct((T, PEER_PICKS), jnp.int32),
                   jax.ShapeDtypeStruct((T, PEER_PICKS), F32),
                   jax.ShapeDtypeStruct((T, PEER_PICKS), F32)],
        scratch_shapes=[pltpu.VMEM((PEER_HEADS, PEER_TOPK, tb), F32),
                        pltpu.VMEM((PEER_HEADS, PEER_TOPK, tb), F32)],
        compiler_params=_cparams(("parallel",)),
        name="topk",
    )(scT)


PEER_U_VALU = 16
PEER_V_VALU = 16


PACK_TE = 1024


def _pack_kernel(x_ref, o_ref):
    s = pl.program_id(1)
    half = PACK_TE // 2
    bits = lambda v: pltpu.bitcast(v.astype(BF16).astype(F32), jnp.uint32)
    even = bits(x_ref[pl.ds(0, half, stride=2), :])
    odd = bits(x_ref[pl.ds(1, half, stride=2), :])
    o_ref[pl.ds(s, half, stride=D_MODEL // 128), :] = (even >> 16) | (odd & jnp.uint32(0xFFFF0000))


def _pack_table(tab):
    E = tab.shape[0]
    return pl.pallas_call(
        _pack_kernel,
        grid=(E // PACK_TE, D_MODEL // 128),
        in_specs=[pl.BlockSpec((PACK_TE, 128), lambda i, s: (i, s))],
        out_specs=pl.BlockSpec((4 * PACK_TE, 128), lambda i, s: (i, 0)),
        out_shape=jax.ShapeDtypeStruct((4 * E, 128), jnp.uint32),
        compiler_params=_cparams(("parallel", "arbitrary")),
        name="pack_table",
    )(tab)


def _split3(a):
    p0 = a.astype(BF16)
    r1 = a - p0.astype(F32)
    p1 = r1.astype(BF16)
    p2 = (r1 - p1.astype(F32)).astype(BF16)
    return p0, p1, p2


def _pair_expand(v, which, base, pairs):
    j = lax.broadcasted_iota(jnp.int32, (PEER_PICKS, 16 * pairs), 0)
    k = lax.broadcasted_iota(jnp.int32, (PEER_PICKS, 16 * pairs), 1)
    onehot = jnp.where(j == base + 2 * (k // 16) + which, 1.0, 0.0).astype(BF16)
    return sum(jnp.dot(p, onehot, preferred_element_type=F32) for p in _split3(v))


def _pair_targets(par, which, base, pairs):
    r = lax.broadcasted_iota(jnp.int32, (par.shape[0], 16 * pairs), 1) % 16
    return 0.5 * (r.astype(F32) - _pair_expand(par, which, base, pairs))


def _tile_words(tab_ref, off_ref, t, j):
    off = pl.multiple_of(off_ref[t, j], 8)
    return tab_ref[pl.ds(off, 8), :]


def _fill_lane_broadcasts(dst_ref, src, n):
    for j in range(n):
        dst_ref[j] = jnp.broadcast_to(src[:, j:j + 1], dst_ref.shape[1:])


def _expert_rows(words, shv):
    return pltpu.bitcast((words << shv) & jnp.uint32(0xFFFF0000), F32)


def _sublane_sums(ps):
    sub = lax.broadcasted_iota(jnp.int32, (8, 128), 0)
    lo4 = sub < 4
    t = [jnp.where(lo4, ps[j], ps[j + 4]) + pltpu.roll(jnp.where(lo4, ps[j + 4], ps[j]), 4, 0)
         for j in range(4)]
    m2 = (sub & 2) == 0
    u = [jnp.where(m2, t[j] + pltpu.roll(t[j], 6, 0), t[j + 2] + pltpu.roll(t[j + 2], 2, 0))
         for j in range(2)]
    m1 = (sub & 1) == 0
    return jnp.where(m1, u[0] + pltpu.roll(u[0], 7, 0), u[1] + pltpu.roll(u[1], 1, 0))


def _token_loop(tb, body, per_group=None):
    def group(i, carry):
        if per_group is not None:
            per_group(i)
        for s in range(PEER_UNROLL):
            body(i * PEER_UNROLL + s)
        return carry

    lax.fori_loop(0, tb // PEER_UNROLL, group, 0)


def _peer_u_kernel(off0_ref, par0_ref, gate0_ref, x_ref, tab_ref, sc_ref, w_ref, offo_ref, paro_ref,
                   off_ref, offv_ref, par_ref, gate_ref, e_scr, g_scr, tga_ref, tgb_ref, v3_ref, shb_ref, actv_ref):
    tb = x_ref.shape[0]
    nv = PEER_U_VALU
    pairs = (PEER_PICKS - nv) // 2
    nblk = 16 * pairs // 128

    @pl.when(pl.program_id(0) == 0)
    def _():
        offv_ref[...] = off0_ref[...]
        par_ref[...] = par0_ref[...]
        gate_ref[...] = gate0_ref[...]

    pltpu.sync_copy(offv_ref, off_ref)
    offo_ref[...] = offv_ref[...]
    par = par_ref[...]
    paro_ref[...] = par
    tga_ref[...] = _pair_targets(par, 0, nv, pairs)
    tgb_ref[...] = _pair_targets(par, 1, nv, pairs)
    _fill_lane_broadcasts(shb_ref, ((1.0 - par) * 16.0).astype(jnp.uint32), nv)
    m_iota = lax.broadcasted_iota(jnp.int32, (8, 16 * pairs), 0).astype(F32)
    zero = jnp.zeros((8, 128), F32)
    ones = jnp.ones((8, 128), BF16)
    nt = (((1,), (1,)), ((), ()))

    def body(t):
        xt = x_ref[t]
        row = pl.ds(t, 1)
        rs = []
        for g in range(nv // 8):
            ps = []
            for jj in range(8):
                j = g * 8 + jj
                shv = jnp.broadcast_to(shb_ref[j, row, :], (8, 128))
                ps.append(_expert_rows(_tile_words(tab_ref, off_ref, t, j), shv) * xt)
            rs.append(_sublane_sums(ps))
        r = jnp.concatenate(rs, axis=0)
        hi = r.astype(BF16)
        lo = (r - hi.astype(F32)).astype(BF16)
        av = (lax.dot_general(ones, hi, nt, preferred_element_type=F32)
              + lax.dot_general(ones, lo, nt, preferred_element_type=F32))
        actv_ref[row, 0:nv] = av[0:1]
        tiles = [pltpu.bitcast(_tile_words(tab_ref, off_ref, t, j), BF16) for j in range(nv, PEER_PICKS)]
        wmat = jnp.concatenate(
            [jnp.concatenate([tiles[2 * q], tiles[2 * q + 1]], axis=1) for q in range(pairs)], axis=0)
        xf = jnp.concatenate([jnp.concatenate([xt, zero], axis=1),
                              jnp.concatenate([zero, xt], axis=1)], axis=0)
        xh = xf.astype(BF16)
        xl = (xf - xh.astype(F32)).astype(BF16)
        o = lax.dot_general(jnp.concatenate([xh, xl], axis=0), wmat, nt, preferred_element_type=F32)
        o = o[0:16] + o[16:32]
        za = jnp.where(tga_ref[row, :] == m_iota, o[0:8], 0.0)
        zb = jnp.where(tgb_ref[row, :] == m_iota, o[8:16], 0.0)
        for half, z in enumerate((za, zb)):
            v = _sublane_sums([z[:, i * 128:(i + 1) * 128] for i in range(nblk)] + [zero] * (8 - nblk))
            for i in range(nblk):
                v3_ref[half * nblk + i, row, :] = v[i:i + 1]

    consts = _topk_consts(tb)

    def next_block_head(h):
        e_scr[h], g_scr[h] = _topk_head(sc_ref, h, consts)

    _token_loop(tb, body, next_block_head)
    lane = lax.broadcasted_iota(jnp.int32, (128, PEER_PICKS), 0)
    pick = lax.broadcasted_iota(jnp.int32, (128, PEER_PICKS), 1)
    act = jnp.zeros((tb, PEER_PICKS), F32)
    for c in range(2 * nblk):
        sel = jnp.where(pick == nv + 2 * (8 * (c % nblk) + lane // 16) + c // nblk, 1.0, 0.0).astype(BF16)
        for p in _split3(v3_ref[c]):
            act = act + jnp.dot(p, sel, preferred_element_type=F32)
    act = jnp.where(pick[0:1] < nv, actv_ref[...], act)
    gelu = 0.5 * act * (1.0 + lax.erf(act * (2.0 ** -0.5)))
    w_ref[...] = gate_ref[...] * gelu
    _topk_finish(e_scr, g_scr, offv_ref, par_ref, gate_ref)


def _peer_specs(tb, tab):
    rows = pl.BlockSpec((tb, PEER_PICKS), lambda i: (i, 0))
    smem = pl.BlockSpec((tb, PEER_PICKS), lambda i: (i, 0), memory_space=pltpu.SMEM)
    table = pl.BlockSpec(tab.shape, lambda i: (0, 0), pipeline_mode=pl.Buffered(1))
    return rows, smem, table


def _peer_u(scT, x3, tab):
    T = x3.shape[0]
    tb = PEER_TB
    assert tb // PEER_UNROLL == PEER_HEADS
    nb = T // tb
    rows, _, table = _peer_specs(tb, tab)
    first = pl.BlockSpec((tb, PEER_PICKS), lambda i: (0, 0))
    pairs = (PEER_PICKS - PEER_U_VALU) // 2
    off0, par0, gate0 = _topk(scT[:, :, :tb])
    return pl.pallas_call(
        _peer_u_kernel,
        grid=(nb,),
        in_specs=[first, first, first, pl.BlockSpec((tb, 8, 128), lambda i: (i, 0, 0)), table,
                  pl.BlockSpec((2 * PEER_HEADS, PEER_NKEYS, tb), lambda i: (0, 0, jnp.minimum(i + 1, nb - 1)))],
        out_specs=[rows, rows, rows],
        out_shape=[jax.ShapeDtypeStruct((T, PEER_PICKS), F32),
                   jax.ShapeDtypeStruct((T, PEER_PICKS), jnp.int32),
                   jax.ShapeDtypeStruct((T, PEER_PICKS), F32)],
        scratch_shapes=[pltpu.SMEM((tb, PEER_PICKS), jnp.int32),
                        pltpu.VMEM((tb, PEER_PICKS), jnp.int32),
                        pltpu.VMEM((tb, PEER_PICKS), F32), pltpu.VMEM((tb, PEER_PICKS), F32),
                        pltpu.VMEM((PEER_HEADS, PEER_TOPK, tb), F32), pltpu.VMEM((PEER_HEADS, PEER_TOPK, tb), F32),
                        pltpu.VMEM((tb, 16 * pairs), F32), pltpu.VMEM((tb, 16 * pairs), F32),
                        pltpu.VMEM((pairs // 4, tb, 128), F32),
                        pltpu.VMEM((PEER_U_VALU, tb, 128), jnp.uint32),
                        pltpu.VMEM((tb, PEER_PICKS), F32)],
        compiler_params=_cparams(("arbitrary",)),
        name="peer_u",
    )(off0, par0, gate0, x3, tab, scT)


def _peer_v_kernel(off_ref, par_ref, w_ref, tab_ref, o_ref, wa_ref, wb_ref, tga_ref, tgb_ref, shb_ref, wbc_ref):
    tb = o_ref.shape[0]
    nv = PEER_V_VALU
    pairs = (PEER_PICKS - nv) // 2
    par = par_ref[...]
    w = w_ref[...]
    wa_ref[...] = _pair_expand(w, 0, nv, pairs)
    wb_ref[...] = _pair_expand(w, 1, nv, pairs)
    tga_ref[...] = _pair_targets(par, 0, nv, pairs)
    tgb_ref[...] = _pair_targets(par, 1, nv, pairs)
    _fill_lane_broadcasts(shb_ref, ((1.0 - par) * 16.0).astype(jnp.uint32), nv)
    _fill_lane_broadcasts(wbc_ref, w, nv)
    m_iota = lax.broadcasted_iota(jnp.int32, (8, 16 * pairs), 0).astype(F32)
    nacc = 4

    def body(t):
        row = pl.ds(t, 1)
        acc = [jnp.zeros((8, 128), F32) for _ in range(nacc)]
        for j in range(nv):
            shv = jnp.broadcast_to(shb_ref[j, row, :], (8, 128))
            wv = jnp.broadcast_to(wbc_ref[j, row, :], (8, 128))
            acc[j % nacc] = acc[j % nacc] + wv * _expert_rows(_tile_words(tab_ref, off_ref, t, j), shv)
        tiles = [pltpu.bitcast(_tile_words(tab_ref, off_ref, t, j), BF16) for j in range(nv, PEER_PICKS)]
        vmat = jnp.concatenate([jnp.concatenate(tiles[0::2], axis=0),
                                jnp.concatenate(tiles[1::2], axis=0)], axis=1)
        la = jnp.where(tga_ref[row, :] == m_iota, wa_ref[row, :], 0.0)
        lb = jnp.where(tgb_ref[row, :] == m_iota, wb_ref[row, :], 0.0)
        lf = jnp.concatenate([la, lb], axis=0)
        lh = lf.astype(BF16)
        ll = (lf - lh.astype(F32)).astype(BF16)
        o = jnp.dot(jnp.concatenate([lh, ll], axis=0), vmat, preferred_element_type=F32)
        o_ref[t] = (((o[0:8, 0:128] + o[8:16, 128:256]) + (o[16:24, 0:128] + o[24:32, 128:256]))
                    + ((acc[0] + acc[1]) + (acc[2] + acc[3])))

    _token_loop(tb, body)


def _peer_v(off, par, w, tab):
    T = par.shape[0]
    tb = PEER_TB
    rows, smem, table = _peer_specs(tb, tab)
    pairs = (PEER_PICKS - PEER_V_VALU) // 2
    return pl.pallas_call(
        _peer_v_kernel,
        grid=(T // tb,),
        in_specs=[smem, rows, rows, table],
        out_specs=pl.BlockSpec((tb, 8, 128), lambda i: (i, 0, 0)),
        out_shape=jax.ShapeDtypeStruct((T, 8, 128), F32),
        scratch_shapes=[pltpu.VMEM((tb, 16 * pairs), F32) for _ in range(4)]
        + [pltpu.VMEM((PEER_V_VALU, tb, 128), jnp.uint32), pltpu.VMEM((PEER_V_VALU, tb, 128), F32)],
        compiler_params=_cparams(("parallel",)),
        name="peer_v",
    )(off, par, w, tab)


def _final_kernel(h_ref, p_ref, g_ref, o_ref):
    nt = D_MODEL // 128
    tm = h_ref.shape[0]
    p = jnp.concatenate([p_ref[pl.ds(c, tm, stride=nt), :] for c in range(nt)], axis=1)
    h = h_ref[...] + p
    o_ref[...] = h * lax.rsqrt(jnp.mean(h * h, axis=-1, keepdims=True) + EPS) * g_ref[...]


def _final(h, p, g):
    T = h.shape[0]
    row = pl.BlockSpec((FINAL_TM, D_MODEL), lambda i: (i, 0))
    return pl.pallas_call(
        _final_kernel,
        grid=(T // FINAL_TM,),
        in_specs=[row, pl.BlockSpec((FINAL_TM * (D_MODEL // 128), 128), lambda i: (i, 0)),
                  _const_spec((1, D_MODEL))],
        out_specs=row,
        out_shape=jax.ShapeDtypeStruct((T, D_MODEL), F32),
        compiler_params=_cparams(("parallel",)),
        name="final_norm",
    )(h, p, g)


def _layer(h2, B, S, norm1_g, w_in, w_ret_out, w_att_out, w_out, norm2_g, peer_wq, peer_subkeys, peer_u, peer_v):
    T = B * S
    aq, ak, av = 3072, 3072 + ATT_W, 3072 + 2 * ATT_W
    att_cols = [w_in[:, o + g * ATT_OUT_W:o + (g + 1) * ATT_OUT_W] for g in range(len(ATT_GROUPS)) for o in (aq, ak, av)]
    w_perm = jnp.concatenate([w_in[:, :3072], w_in[:, 7680:9728]] + att_cols, axis=1).astype(BF16)
    main, a0, a1, a2 = _proj(h2, norm1_g.reshape(1, D_MODEL), w_perm, B, S)
    retg = _retention(main.reshape(B, S, MAIN_W)).reshape(T, RET_V_W)
    (o1, l1), (o2, l2), (o3, l3) = (_att_group(a, g) for g, a in enumerate((a0.reshape(B, 1, S, ATT_SEC_W), a1, a2)))
    sk = peer_subkeys.reshape(2 * PEER_HEADS, PEER_NKEYS, PEER_DKEY // 2).astype(BF16)
    h_mid, xn2, scT = _mix(retg, o1, o2, o3, l1, l2, l3, main, h2,
                           w_ret_out.astype(BF16), w_att_out.astype(BF16), w_out.astype(BF16),
                           norm2_g.reshape(1, D_MODEL), peer_wq.astype(BF16), sk)
    w, off, par = _peer_u(scT, xn2.reshape(T, 8, 128), _pack_table(peer_u))
    pout = _peer_v(off, par, w, _pack_table(peer_v))
    return h_mid, pout.reshape(T * 8, 128)


def kernel(x, norm1_g, w_in, w_ret_out, w_att_out, w_out, norm2_g, peer_wq, peer_subkeys, peer_u, peer_v, normf_g):
    B, S, D = x.shape
    assert w_in.shape[0] == 1, "single-layer block"
    h, pout = _layer(x.reshape(B * S, D), B, S, norm1_g[0], w_in[0], w_ret_out[0], w_att_out[0], w_out[0],
                     norm2_g[0], peer_wq[0], peer_subkeys[0], peer_u[0], peer_v[0])
    return _final(h, pout, normf_g.reshape(1, D)).reshape(B, S, D)
```

```python
import functools
import math

import jax
import jax.numpy as jnp
from jax import lax
from jax.experimental import pallas as pl
from jax.experimental.pallas import tpu as pltpu

F32 = jnp.float32
BF16 = jnp.bfloat16

D_MODEL = 1024
RET_HEADS = 8
RET_DK = 64
RET_DV = 128
RET_CHUNK = 128
ATT_GROUPS = ((128, 1), (512, 4), (2048, 16))
ATT_HPG = 4
ATT_HEADS = 12
ATT_DH = 128
ATT_BLOCK = 128
ATT_BATCH = 2
PEER_HEADS = 8
PEER_NKEYS = 128
PEER_DKEY = 256
PEER_TOPK = 16
PEER_PICKS = PEER_HEADS * PEER_TOPK
EPS = 1e-6

RET_QK_W = RET_HEADS * RET_DK
RET_V_W = RET_HEADS * RET_DV
ATT_W = ATT_HEADS * ATT_DH
ATT_OUT_W = ATT_HPG * ATT_DH
OFF_RQ, OFF_RK, OFF_RV, OFF_RG = 0, 512, 1024, 2048
OFF_GR, OFF_GA = 3072, 4096
MAIN_W = 5120
ATT_SEC_W = 3 * ATT_OUT_W
IN_W = MAIN_W + len(ATT_GROUPS) * ATT_SEC_W

VMEM_LIMIT_BYTES = 60 * 1024 * 1024

PROJ_TM = 256
PROJ_TN = 512
MIX_TM = 256
TOPK_TB = 512
PEER_TB = 256
PEER_UNROLL = 32
FINAL_TM = 512


def _cparams(sem):
    return pltpu.CompilerParams(dimension_semantics=sem, vmem_limit_bytes=VMEM_LIMIT_BYTES)


def _const_spec(shape):
    nd = len(shape)
    return pl.BlockSpec(shape, lambda *_: (0,) * nd)


def _proj_kernel(x_ref, g_ref, w_ref, main_ref, a0_ref, a1_ref, a2_ref, scr_ref):
    x = x_ref[...]
    ms = jnp.mean(x * x, axis=-1, keepdims=True)
    xn = (x * lax.rsqrt(ms + EPS) * g_ref[...]).astype(BF16)

    def chunk(j):
        return jnp.dot(xn, w_ref[:, j * PROJ_TN:(j + 1) * PROJ_TN], preferred_element_type=F32)

    nmain = MAIN_W // PROJ_TN
    for j in range(nmain):
        main_ref[:, j * PROJ_TN:(j + 1) * PROJ_TN] = chunk(j).astype(BF16)
    for s in range(3):
        a0_ref[:, s * PROJ_TN:(s + 1) * PROJ_TN] = chunk(nmain + s).astype(BF16)
    for g, a_ref in ((1, a1_ref), (2, a2_ref)):
        d = ATT_GROUPS[g][1]
        for s in range(3):
            res = chunk(nmain + 3 * g + s)
            for c in range(PROJ_TN // 128):
                scr_ref[c] = res[:, c * 128:(c + 1) * 128]
            for r in range(d):
                piece = jnp.concatenate(
                    [scr_ref[c, pl.ds(r, PROJ_TM // d, stride=d), :] for c in range(PROJ_TN // 128)], axis=1)
                a_ref[0, r, :, s * PROJ_TN:(s + 1) * PROJ_TN] = piece.astype(BF16)


def _proj(x2, g, w, B, S):
    T = x2.shape[0]
    tiles = S // PROJ_TM
    d1, d2 = ATT_GROUPS[1][1], ATT_GROUPS[2][1]
    return pl.pallas_call(
        _proj_kernel,
        grid=(T // PROJ_TM,),
        in_specs=[
            pl.BlockSpec((PROJ_TM, D_MODEL), lambda i: (i, 0)),
            _const_spec((1, D_MODEL)),
            pl.BlockSpec((D_MODEL, IN_W), lambda i: (0, 0), pipeline_mode=pl.Buffered(1)),
        ],
        out_specs=[
            pl.BlockSpec((PROJ_TM, MAIN_W), lambda i: (i, 0)),
            pl.BlockSpec((PROJ_TM, ATT_SEC_W), lambda i: (i, 0)),
            pl.BlockSpec((1, d1, PROJ_TM // d1, ATT_SEC_W), lambda i: (i // tiles, 0, i % tiles, 0)),
            pl.BlockSpec((1, d2, PROJ_TM // d2, ATT_SEC_W), lambda i: (i // tiles, 0, i % tiles, 0)),
        ],
        out_shape=[
            jax.ShapeDtypeStruct((T, MAIN_W), BF16),
            jax.ShapeDtypeStruct((T, ATT_SEC_W), BF16),
            jax.ShapeDtypeStruct((B, d1, S // d1, ATT_SEC_W), BF16),
            jax.ShapeDtypeStruct((B, d2, S // d2, ATT_SEC_W), BF16),
        ],
        scratch_shapes=[pltpu.VMEM((PROJ_TN // 128, PROJ_TM, 128), F32)],
        compiler_params=_cparams(("parallel",)),
        name="proj",
    )(x2, g, w)


def _retention_kernel(q_ref, k_ref, v_ref, rg_ref, o_ref, state_ref, decay_ref):
    C = RET_CHUNK

    @pl.when(pl.program_id(1) == 0)
    def _():
        state_ref[...] = jnp.zeros_like(state_ref)
        pi = lax.broadcasted_iota(jnp.int32, (C, C), 0)
        pj = lax.broadcasted_iota(jnp.int32, (C, C), 1)
        diff = (pi - pj).astype(F32)
        for h in range(RET_HEADS):
            lg = math.log1p(-(2.0 ** (-5.0 - h)))
            decay_ref[h] = jnp.where(diff >= 0, jnp.exp(jnp.maximum(diff, 0.0) * lg), 0.0)

    pos = lax.broadcasted_iota(jnp.int32, (C, 1), 0).astype(F32)
    q = q_ref[0]
    k = k_ref[0]
    v = v_ref[0]
    rg = rg_ref[0]
    for h in range(RET_HEADS):
        lg = math.log1p(-(2.0 ** (-5.0 - h)))
        decay = decay_ref[h]
        w_k = jnp.exp((C - 1 - pos) * lg)
        w_q = jnp.exp((pos + 1.0) * lg)
        qh = q[:, h * RET_DK:(h + 1) * RET_DK]
        kh = k[:, h * RET_DK:(h + 1) * RET_DK].astype(F32) * (RET_DK ** -0.5)
        vh = v[:, h * RET_DV:(h + 1) * RET_DV]
        s = lax.dot_general(qh, kh.astype(BF16), (((1,), (1,)), ((), ())), preferred_element_type=F32)
        p = (s * decay).astype(BF16)
        inner = jnp.dot(p, vh, preferred_element_type=F32)
        st = state_ref[h]
        cross = jnp.dot((qh.astype(F32) * w_q).astype(BF16), st.astype(BF16), preferred_element_type=F32)
        kw = (kh * w_k).astype(BF16)
        kv = lax.dot_general(kw, vh, (((0,), (0,)), ((), ())), preferred_element_type=F32)
        state_ref[h] = math.exp(C * lg) * st + kv
        ret = inner + cross
        rn = ret * lax.rsqrt(jnp.mean(ret * ret, axis=-1, keepdims=True) + EPS)
        g = rg[:, h * RET_DV:(h + 1) * RET_DV].astype(F32)
        o_ref[0, :, h * RET_DV:(h + 1) * RET_DV] = (g * jax.nn.sigmoid(g) * rn).astype(BF16)


def _retention(proj3):
    B, S, _ = proj3.shape
    n = S // RET_CHUNK
    return pl.pallas_call(
        _retention_kernel,
        grid=(B, n),
        in_specs=[
            pl.BlockSpec((1, RET_CHUNK, RET_QK_W), lambda b, c: (b, c, OFF_RQ // RET_QK_W)),
            pl.BlockSpec((1, RET_CHUNK, RET_QK_W), lambda b, c: (b, c, OFF_RK // RET_QK_W)),
            pl.BlockSpec((1, RET_CHUNK, RET_V_W), lambda b, c: (b, c, OFF_RV // RET_V_W)),
            pl.BlockSpec((1, RET_CHUNK, RET_V_W), lambda b, c: (b, c, OFF_RG // RET_V_W)),
        ],
        out_specs=pl.BlockSpec((1, RET_CHUNK, RET_V_W), lambda b, c: (b, c, 0)),
        out_shape=jax.ShapeDtypeStruct((B, S, RET_V_W), BF16),
        scratch_shapes=[pltpu.VMEM((RET_HEADS, RET_DK, RET_DV), F32),
                        pltpu.VMEM((RET_HEADS, RET_CHUNK, RET_CHUNK), F32)],
        compiler_params=_cparams(("parallel", "arbitrary")),
        name="retention",
    )(proj3, proj3, proj3, proj3)


def _att_kernel(q_ref, kp_ref, kc_ref, vp_ref, vc_ref, o_ref, lse_ref, *, group, dilation, span):
    n = pl.program_id(1)
    r = pl.program_id(2)
    iq = lax.broadcasted_iota(jnp.int32, (ATT_BLOCK, 2 * ATT_BLOCK), 0)
    jk = lax.broadcasted_iota(jnp.int32, (ATT_BLOCK, 2 * ATT_BLOCK), 1)
    dist = iq + ATT_BLOCK - jk
    valid = (dist >= 0) & (dist <= span) & ((jk >= ATT_BLOCK) | (n > 0))
    distf = (dilation * dist).astype(F32)
    rows = pl.ds(r, ATT_BLOCK, stride=dilation) if dilation > 1 else slice(None)
    for bb in range(ATT_BATCH):
        q = q_ref[bb, 0]
        kk = jnp.concatenate([kp_ref[bb, 0], kc_ref[bb, 0]], axis=0)
        vv = jnp.concatenate([vp_ref[bb, 0], vc_ref[bb, 0]], axis=0)
        for i in range(ATT_HPG):
            slope = 2.0 ** (-8.0 * (group * ATT_HPG + i + 1) / ATT_HEADS)
            sl = slice(i * ATT_DH, (i + 1) * ATT_DH)
            s = lax.dot_general(q[:, sl], kk[:, sl], (((1,), (1,)), ((), ())), preferred_element_type=F32)
            s = s * (ATT_DH ** -0.5)
            s = jnp.where(valid, s - slope * distf, -jnp.inf)
            m = jnp.max(s, axis=-1, keepdims=True)
            p = jnp.exp(s - m)
            den = jnp.sum(p, axis=-1, keepdims=True)
            o_ref[i, bb, rows, :] = jnp.dot(p.astype(BF16), vv[:, sl], preferred_element_type=F32) / den
            lse_ref[i, bb, rows, :] = jnp.broadcast_to(m + jnp.log(den), (ATT_BLOCK, ATT_DH))


def _att_group(qkv, group):
    window, d = ATT_GROUPS[group]
    B, _, L, _ = qkv.shape
    S = L * d
    nL = L // ATT_BLOCK

    def cur(sec):
        return pl.BlockSpec((ATT_BATCH, 1, ATT_BLOCK, ATT_OUT_W), lambda b, n, r: (b, r, n, sec))

    def prev(sec):
        return pl.BlockSpec((ATT_BATCH, 1, ATT_BLOCK, ATT_OUT_W), lambda b, n, r: (b, r, jnp.maximum(n - 1, 0), sec))

    out_spec = pl.BlockSpec((ATT_HPG, ATT_BATCH, ATT_BLOCK * d, ATT_DH), lambda b, n, r: (0, b, n, 0))
    out_shape = jax.ShapeDtypeStruct((ATT_HPG, B, S, ATT_DH), F32)
    o, lse = pl.pallas_call(
        functools.partial(_att_kernel, group=group, dilation=d, span=window // d),
        grid=(B // ATT_BATCH, nL, d),
        in_specs=[cur(0), prev(1), cur(1), prev(2), cur(2)],
        out_specs=[out_spec, out_spec],
        out_shape=[out_shape, out_shape],
        compiler_params=_cparams(("parallel", "arbitrary", "arbitrary")),
        name=f"att_g{group}",
    )(qkv, qkv, qkv, qkv, qkv)
    return o.reshape(ATT_HPG, B * S, ATT_DH), lse.reshape(ATT_HPG, B * S, ATT_DH)


def _mix_kernel(retg_ref, o1_ref, o2_ref, o3_ref, l1_ref, l2_ref, l3_ref, gr_ref, ga_ref, x_ref,
                wro_ref, wao_ref, wo_ref, g2_ref, wq_ref, sk_ref, h_ref, xn_ref, sc_ref):
    heads = lambda ref: jnp.concatenate([ref[i] for i in range(ATT_HPG)], axis=1)
    l1, l2, l3 = heads(l1_ref), heads(l2_ref), heads(l3_ref)
    mx = jnp.maximum(jnp.maximum(l1, l2), l3)
    e1, e2, e3 = jnp.exp(l1 - mx), jnp.exp(l2 - mx), jnp.exp(l3 - mx)
    att = (e1 * heads(o1_ref) + e2 * heads(o2_ref) + e3 * heads(o3_ref)) / (e1 + e2 + e3)
    a_branch = jnp.dot(att.astype(BF16), wao_ref[...], preferred_element_type=F32)
    r_branch = jnp.dot(retg_ref[...], wro_ref[...], preferred_element_type=F32)
    merged = (jax.nn.sigmoid(gr_ref[...].astype(F32)) * r_branch
              + jax.nn.sigmoid(ga_ref[...].astype(F32)) * a_branch)
    h = x_ref[...] + jnp.dot(merged.astype(BF16), wo_ref[...], preferred_element_type=F32)
    h_ref[...] = h
    xn = h * lax.rsqrt(jnp.mean(h * h, axis=-1, keepdims=True) + EPS) * g2_ref[...]
    for c in range(D_MODEL // 128):
        xn_ref[pl.ds(c, h.shape[0], stride=D_MODEL // 128), :] = xn[:, c * 128:(c + 1) * 128]
    qp = jnp.dot(xn.astype(BF16), wq_ref[...], preferred_element_type=F32).astype(BF16)
    half = PEER_DKEY // 2
    for hp in range(2 * PEER_HEADS):
        sc_ref[hp] = lax.dot_general(sk_ref[hp], qp[:, hp * half:(hp + 1) * half],
                                     (((1,), (1,)), ((), ())), preferred_element_type=F32)


def _mix(retg, o1, o2, o3, l1, l2, l3, proj, x2, wro, wao, wo, g2, wq, sk):
    T = x2.shape[0]
    tm = MIX_TM
    row = lambda w: pl.BlockSpec((tm, w), lambda i: (i, 0))
    hd = pl.BlockSpec((ATT_HPG, tm, ATT_DH), lambda i: (0, i, 0))
    nhp = 2 * PEER_HEADS
    nt = D_MODEL // 128
    return pl.pallas_call(
        _mix_kernel,
        grid=(T // tm,),
        in_specs=[
            row(RET_V_W), hd, hd, hd, hd, hd, hd,
            pl.BlockSpec((tm, D_MODEL), lambda i: (i, OFF_GR // D_MODEL)),
            pl.BlockSpec((tm, D_MODEL), lambda i: (i, OFF_GA // D_MODEL)),
            row(D_MODEL),
            _const_spec(wro.shape), _const_spec(wao.shape), _const_spec(wo.shape),
            _const_spec(g2.shape), _const_spec(wq.shape), _const_spec(sk.shape),
        ],
        out_specs=[row(D_MODEL), pl.BlockSpec((tm * nt, 128), lambda i: (i, 0)),
                   pl.BlockSpec((nhp, PEER_NKEYS, tm), lambda i: (0, 0, i))],
        out_shape=[jax.ShapeDtypeStruct((T, D_MODEL), F32),
                   jax.ShapeDtypeStruct((T * nt, 128), F32),
                   jax.ShapeDtypeStruct((nhp, PEER_NKEYS, T), F32)],
        compiler_params=_cparams(("parallel",)),
        name="mix",
    )(retg, o1, o2, o3, l1, l2, l3, proj, proj, x2, wro, wao, wo, g2, wq, sk)


def _topk_rows(s, ids, k, fill):
    vals, idxs = [], []
    for _ in range(k):
        m = jnp.max(s, axis=0, keepdims=True)
        i = jnp.min(jnp.where(s == m, ids, fill), axis=0, keepdims=True)
        vals.append(m)
        idxs.append(i)
        s = jnp.where(ids == i, -jnp.inf, s)
    return jnp.concatenate(vals, axis=0), jnp.concatenate(idxs, axis=0)


_CAND_B = [PEER_TOPK // (a + 1) for a in range(PEER_TOPK)]
_CAND_ROWS = -(-sum(_CAND_B) // 8) * 8


def _topk_consts(W):
    K = PEER_TOPK
    key_ids = lax.broadcasted_iota(jnp.int32, (PEER_NKEYS, W), 0)
    npad = _CAND_ROWS - sum(_CAND_B)
    cand_ids = jnp.concatenate(
        [jnp.full((1, W), a * K + b, jnp.int32) for a in range(K) for b in range(_CAND_B[a])]
        + [jnp.full((npad, W), K * K, jnp.int32)], axis=0)
    pad = jnp.full((npad, W), -jnp.inf, F32)
    return key_ids, cand_ids, pad


def _topk_head(sc_ref, h, consts):
    K = PEER_TOPK
    key_ids, cand_ids, pad = consts
    v0, i0 = _topk_rows(sc_ref[2 * h], key_ids, K, PEER_NKEYS)
    v1, i1 = _topk_rows(sc_ref[2 * h + 1], key_ids, K, PEER_NKEYS)
    cand = jnp.concatenate([v0[a:a + 1] + v1[0:_CAND_B[a]] for a in range(K)] + [pad], axis=0)
    best_s, best_c = _topk_rows(cand, cand_ids, K, K * K)
    ca = best_c // K
    cb = best_c % K
    ia = jnp.zeros_like(best_c)
    ib = jnp.zeros_like(best_c)
    for a in range(K):
        ia = jnp.where(ca == a, i0[a:a + 1], ia)
        ib = jnp.where(cb == a, i1[a:a + 1], ib)
    ex = jnp.exp(best_s - best_s[0:1])
    return (ia * PEER_NKEYS + ib).astype(F32), ex / jnp.sum(ex, axis=0, keepdims=True)


def _topk_finish(e_scr, g_scr, off_ref, par_ref, gate_ref):
    W = e_scr.shape[-1]
    e = e_scr[...].reshape(PEER_PICKS, W).T.astype(jnp.int32)
    off_ref[...] = lax.shift_right_logical(e, 1) * 8
    par_ref[...] = (e & 1).astype(F32)
    gate_ref[...] = g_scr[...].reshape(PEER_PICKS, W).T


def _topk_kernel(sc_ref, off_ref, par_ref, gate_ref, e_scr, g_scr):
    consts = _topk_consts(sc_ref.shape[-1])

    def head(h, carry):
        e_scr[h], g_scr[h] = _topk_head(sc_ref, h, consts)
        return carry

    lax.fori_loop(0, PEER_HEADS, head, 0)
    _topk_finish(e_scr, g_scr, off_ref, par_ref, gate_ref)


def _topk(scT):
    T = scT.shape[-1]
    tb = min(TOPK_TB, T)
    out_spec = pl.BlockSpec((tb, PEER_PICKS), lambda i: (i, 0))
    return pl.pallas_call(
        _topk_kernel,
        grid=(T // tb,),
        in_specs=[pl.BlockSpec((2 * PEER_HEADS, PEER_NKEYS, tb), lambda i: (0, 0, i))],
        out_specs=[out_spec, out_spec, out_spec],
        out_shape=[jax.ShapeDtypeStruct((T, PEER_PICKS), jnp.int32),
                   jax.ShapeDtypeStruct((T, PEER_PICKS), F32),
                   jax.ShapeDtypeStruct((T, PEER_PICKS), F32)],
        scratch_shapes=[pltpu.VMEM((PEER_HEADS, PEER_TOPK, tb), F32),
                        pltpu.VMEM((PEER_HEADS, PEER_TOPK, tb), F32)],
        compiler_params=_cparams(("parallel",)),
        name="topk",
    )(scT)


PEER_U_VALU = 16
PEER_V_VALU = 16


PACK_TE = 1024


def _pack_kernel(x_ref, o_ref):
    s = pl.program_id(1)
    half = PACK_TE // 2
    bits = lambda v: pltpu.bitcast(v.astype(BF16).astype(F32), jnp.uint32)
    even = bits(x_ref[pl.ds(0, half, stride=2), :])
    odd = bits(x_ref[pl.ds(1, half, stride=2), :])
    o_ref[pl.ds(s, half, stride=D_MODEL // 128), :] = (even >> 16) | (odd & jnp.uint32(0xFFFF0000))


def _pack_table(tab):
    E = tab.shape[0]
    return pl.pallas_call(
        _pack_kernel,
        grid=(E // PACK_TE, D_MODEL // 128),
        in_specs=[pl.BlockSpec((PACK_TE, 128), lambda i, s: (i, s))],
        out_specs=pl.BlockSpec((4 * PACK_TE, 128), lambda i, s: (i, 0)),
        out_shape=jax.ShapeDtypeStruct((4 * E, 128), jnp.uint32),
        compiler_params=_cparams(("parallel", "arbitrary")),
        name="pack_table",
    )(tab)


def _split3(a):
    p0 = a.astype(BF16)
    r1 = a - p0.astype(F32)
    p1 = r1.astype(BF16)
    p2 = (r1 - p1.astype(F32)).astype(BF16)
    return p0, p1, p2


def _pair_expand(v, which, base, pairs):
    j = lax.broadcasted_iota(jnp.int32, (PEER_PICKS, 16 * pairs), 0)
    k = lax.broadcasted_iota(jnp.int32, (PEER_PICKS, 16 * pairs), 1)
    onehot = jnp.where(j == base + 2 * (k // 16) + which, 1.0, 0.0).astype(BF16)
    return sum(jnp.dot(p, onehot, preferred_element_type=F32) for p in _split3(v))


def _pair_targets(par, which, base, pairs):
    r = lax.broadcasted_iota(jnp.int32, (par.shape[0], 16 * pairs), 1) % 16
    return 0.5 * (r.astype(F32) - _pair_expand(par, which, base, pairs))


def _tile_words(tab_ref, off_ref, t, j):
    off = pl.multiple_of(off_ref[t, j], 8)
    return tab_ref[pl.ds(off, 8), :]


def _fill_lane_broadcasts(dst_ref, src, n):
    for j in range(n):
        dst_ref[j] = jnp.broadcast_to(src[:, j:j + 1], dst_ref.shape[1:])


def _expert_rows(words, shv):
    return pltpu.bitcast((words << shv) & jnp.uint32(0xFFFF0000), F32)


def _sublane_sums(ps):
    sub = lax.broadcasted_iota(jnp.int32, (8, 128), 0)
    lo4 = sub < 4
    t = [jnp.where(lo4, ps[j], ps[j + 4]) + pltpu.roll(jnp.where(lo4, ps[j + 4], ps[j]), 4, 0)
         for j in range(4)]
    m2 = (sub & 2) == 0
    u = [jnp.where(m2, t[j] + pltpu.roll(t[j], 6, 0), t[j + 2] + pltpu.roll(t[j + 2], 2, 0))
         for j in range(2)]
    m1 = (sub & 1) == 0
    return jnp.where(m1, u[0] + pltpu.roll(u[0], 7, 0), u[1] + pltpu.roll(u[1], 1, 0))


def _token_loop(tb, body, per_group=None):
    def group(i, carry):
        if per_group is not None:
            per_group(i)
        for s in range(PEER_UNROLL):
            body(i * PEER_UNROLL + s)
        return carry

    lax.fori_loop(0, tb // PEER_UNROLL, group, 0)


def _peer_u_kernel(off0_ref, par0_ref, gate0_ref, x_ref, tab_ref, sc_ref, w_ref, offo_ref, paro_ref,
                   off_ref, sem_ref, offv_ref, par_ref, gate_ref, e_scr, g_scr, tga_ref, tgb_ref, v3_ref, shb_ref,
                   actv_ref):
    tb = x_ref.shape[0]
    nv = PEER_U_VALU
    pairs = (PEER_PICKS - nv) // 2
    nblk = 16 * pairs // 128

    @pl.when(pl.program_id(0) == 0)
    def _():
        offv_ref[...] = off0_ref[...]
        par_ref[...] = par0_ref[...]
        gate_ref[...] = gate0_ref[...]

    def offsets_copy(g):
        rows = pl.ds(g * PEER_UNROLL, PEER_UNROLL)
        return pltpu.make_async_copy(offv_ref.at[rows], off_ref.at[rows], sem_ref.at[g])

    for g in range(tb // PEER_UNROLL):
        offsets_copy(g).start()
    offo_ref[...] = offv_ref[...]
    par = par_ref[...]
    paro_ref[...] = par
    tga_ref[...] = _pair_targets(par, 0, nv, pairs)
    tgb_ref[...] = _pair_targets(par, 1, nv, pairs)
    _fill_lane_broadcasts(shb_ref, ((1.0 - par) * 16.0).astype(jnp.uint32), nv)
    m_iota = lax.broadcasted_iota(jnp.int32, (8, 16 * pairs), 0).astype(F32)
    zero = jnp.zeros((8, 128), F32)
    ones = jnp.ones((8, 128), BF16)
    nt = (((1,), (1,)), ((), ()))

    def body(t):
        xt = x_ref[t]
        row = pl.ds(t, 1)
        rs = []
        for g in range(nv // 8):
            ps = []
            for jj in range(8):
                j = g * 8 + jj
                shv = jnp.broadcast_to(shb_ref[j, row, :], (8, 128))
                ps.append(_expert_rows(_tile_words(tab_ref, off_ref, t, j), shv) * xt)
            rs.append(_sublane_sums(ps))
        r = jnp.concatenate(rs, axis=0)
        hi = r.astype(BF16)
        lo = (r - hi.astype(F32)).astype(BF16)
        av = (lax.dot_general(ones, hi, nt, preferred_element_type=F32)
              + lax.dot_general(ones, lo, nt, preferred_element_type=F32))
        actv_ref[row, 0:nv] = av[0:1]
        tiles = [pltpu.bitcast(_tile_words(tab_ref, off_ref, t, j), BF16) for j in range(nv, PEER_PICKS)]
        wmat = jnp.concatenate(
            [jnp.concatenate([tiles[2 * q], tiles[2 * q + 1]], axis=1) for q in range(pairs)], axis=0)
        xf = jnp.concatenate([jnp.concatenate([xt, zero], axis=1),
                              jnp.concatenate([zero, xt], axis=1)], axis=0)
        xh = xf.astype(BF16)
        xl = (xf - xh.astype(F32)).astype(BF16)
        o = lax.dot_general(jnp.concatenate([xh, xl], axis=0), wmat, nt, preferred_element_type=F32)
        o = o[0:16] + o[16:32]
        za = jnp.where(tga_ref[row, :] == m_iota, o[0:8], 0.0)
        zb = jnp.where(tgb_ref[row, :] == m_iota, o[8:16], 0.0)
        for half, z in enumerate((za, zb)):
            v = _sublane_sums([z[:, i * 128:(i + 1) * 128] for i in range(nblk)] + [zero] * (8 - nblk))
            for i in range(nblk):
                v3_ref[half * nblk + i, row, :] = v[i:i + 1]

    consts = _topk_consts(tb)

    def next_block_head(h):
        offsets_copy(h).wait()
        e_scr[h], g_scr[h] = _topk_head(sc_ref, h, consts)

    _token_loop(tb, body, next_block_head)
    lane = lax.broadcasted_iota(jnp.int32, (128, PEER_PICKS), 0)
    pick = lax.broadcasted_iota(jnp.int32, (128, PEER_PICKS), 1)
    act = jnp.zeros((tb, PEER_PICKS), F32)
    for c in range(2 * nblk):
        sel = jnp.where(pick == nv + 2 * (8 * (c % nblk) + lane // 16) + c // nblk, 1.0, 0.0).astype(BF16)
        for p in _split3(v3_ref[c]):
            act = act + jnp.dot(p, sel, preferred_element_type=F32)
    act = jnp.where(pick[0:1] < nv, actv_ref[...], act)
    gelu = 0.5 * act * (1.0 + lax.erf(act * (2.0 ** -0.5)))
    w_ref[...] = gate_ref[...] * gelu
    _topk_finish(e_scr, g_scr, offv_ref, par_ref, gate_ref)


def _peer_specs(tb, tab):
    rows = pl.BlockSpec((tb, PEER_PICKS), lambda i: (i, 0))
    smem = pl.BlockSpec((tb, PEER_PICKS), lambda i: (i, 0), memory_space=pltpu.SMEM)
    table = pl.BlockSpec(tab.shape, lambda i: (0, 0), pipeline_mode=pl.Buffered(1))
    return rows, smem, table


def _peer_u(scT, x3, tab):
    T = x3.shape[0]
    tb = PEER_TB
    assert tb // PEER_UNROLL == PEER_HEADS
    nb = T // tb
    rows, _, table = _peer_specs(tb, tab)
    first = pl.BlockSpec((tb, PEER_PICKS), lambda i: (0, 0))
    pairs = (PEER_PICKS - PEER_U_VALU) // 2
    off0, par0, gate0 = _topk(scT[:, :, :tb])
    return pl.pallas_call(
        _peer_u_kernel,
        grid=(nb,),
        in_specs=[first, first, first, pl.BlockSpec((tb, 8, 128), lambda i: (i, 0, 0)), table,
                  pl.BlockSpec((2 * PEER_HEADS, PEER_NKEYS, tb), lambda i: (0, 0, jnp.minimum(i + 1, nb - 1)))],
        out_specs=[rows, rows, rows],
        out_shape=[jax.ShapeDtypeStruct((T, PEER_PICKS), F32),
                   jax.ShapeDtypeStruct((T, PEER_PICKS), jnp.int32),
                   jax.ShapeDtypeStruct((T, PEER_PICKS), F32)],
        scratch_shapes=[pltpu.SMEM((tb, PEER_PICKS), jnp.int32),
                        pltpu.SemaphoreType.DMA((tb // PEER_UNROLL,)),
                        pltpu.VMEM((tb, PEER_PICKS), jnp.int32),
                        pltpu.VMEM((tb, PEER_PICKS), F32), pltpu.VMEM((tb, PEER_PICKS), F32),
                        pltpu.VMEM((PEER_HEADS, PEER_TOPK, tb), F32), pltpu.VMEM((PEER_HEADS, PEER_TOPK, tb), F32),
                        pltpu.VMEM((tb, 16 * pairs), F32), pltpu.VMEM((tb, 16 * pairs), F32),
                        pltpu.VMEM((pairs // 4, tb, 128), F32),
                        pltpu.VMEM((PEER_U_VALU, tb, 128), jnp.uint32),
                        pltpu.VMEM((tb, PEER_PICKS), F32)],
        compiler_params=_cparams(("arbitrary",)),
        name="peer_u",
    )(off0, par0, gate0, x3, tab, scT)


def _peer_v_kernel(off_ref, par_ref, w_ref, tab_ref, o_ref, wa_ref, wb_ref, tga_ref, tgb_ref, shb_ref, wbc_ref):
    tb = o_ref.shape[0]
    nv = PEER_V_VALU
    pairs = (PEER_PICKS - nv) // 2
    par = par_ref[...]
    w = w_ref[...]
    wa_ref[...] = _pair_expand(w, 0, nv, pairs)
    wb_ref[...] = _pair_expand(w, 1, nv, pairs)
    tga_ref[...] = _pair_targets(par, 0, nv, pairs)
    tgb_ref[...] = _pair_targets(par, 1, nv, pairs)
    _fill_lane_broadcasts(shb_ref, ((1.0 - par) * 16.0).astype(jnp.uint32), nv)
    _fill_lane_broadcasts(wbc_ref, w, nv)
    m_iota = lax.broadcasted_iota(jnp.int32, (8, 16 * pairs), 0).astype(F32)
    nacc = 4

    def body(t):
        row = pl.ds(t, 1)
        acc = [jnp.zeros((8, 128), F32) for _ in range(nacc)]
        for j in range(nv):
            shv = jnp.broadcast_to(shb_ref[j, row, :], (8, 128))
            wv = jnp.broadcast_to(wbc_ref[j, row, :], (8, 128))
            acc[j % nacc] = acc[j % nacc] + wv * _expert_rows(_tile_words(tab_ref, off_ref, t, j), shv)
        tiles = [pltpu.bitcast(_tile_words(tab_ref, off_ref, t, j), BF16) for j in range(nv, PEER_PICKS)]
        vmat = jnp.concatenate([jnp.concatenate(tiles[0::2], axis=0),
                                jnp.concatenate(tiles[1::2], axis=0)], axis=1)
        la = jnp.where(tga_ref[row, :] == m_iota, wa_ref[row, :], 0.0)
        lb = jnp.where(tgb_ref[row, :] == m_iota, wb_ref[row, :], 0.0)
        lf = jnp.concatenate([la, lb], axis=0)
        lh = lf.astype(BF16)
        ll = (lf - lh.astype(F32)).astype(BF16)
        o = jnp.dot(jnp.concatenate([lh, ll], axis=0), vmat, preferred_element_type=F32)
        o_ref[t] = (((o[0:8, 0:128] + o[8:16, 128:256]) + (o[16:24, 0:128] + o[24:32, 128:256]))
                    + ((acc[0] + acc[1]) + (acc[2] + acc[3])))

    _token_loop(tb, body)


def _peer_v(off, par, w, tab):
    T = par.shape[0]
    tb = PEER_TB
    rows, smem, table = _peer_specs(tb, tab)
    pairs = (PEER_PICKS - PEER_V_VALU) // 2
    return pl.pallas_call(
        _peer_v_kernel,
        grid=(T // tb,),
        in_specs=[smem, rows, rows, table],
        out_specs=pl.BlockSpec((tb, 8, 128), lambda i: (i, 0, 0)),
        out_shape=jax.ShapeDtypeStruct((T, 8, 128), F32),
        scratch_shapes=[pltpu.VMEM((tb, 16 * pairs), F32) for _ in range(4)]
        + [pltpu.VMEM((PEER_V_VALU, tb, 128), jnp.uint32), pltpu.VMEM((PEER_V_VALU, tb, 128), F32)],
        compiler_params=_cparams(("parallel",)),
        name="peer_v",
    )(off, par, w, tab)


def _final_kernel(h_ref, p_ref, g_ref, o_ref):
    nt = D_MODEL // 128
    tm = h_ref.shape[0]
    p = jnp.concatenate([p_ref[pl.ds(c, tm, stride=nt), :] for c in range(nt)], axis=1)
    h = h_ref[...] + p
    o_ref[...] = h * lax.rsqrt(jnp.mean(h * h, axis=-1, keepdims=True) + EPS) * g_ref[...]


def _final(h, p, g):
    T = h.shape[0]
    row = pl.BlockSpec((FINAL_TM, D_MODEL), lambda i: (i, 0))
    return pl.pallas_call(
        _final_kernel,
        grid=(T // FINAL_TM,),
        in_specs=[row, pl.BlockSpec((FINAL_TM * (D_MODEL // 128), 128), lambda i: (i, 0)),
                  _const_spec((1, D_MODEL))],
        out_specs=row,
        out_shape=jax.ShapeDtypeStruct((T, D_MODEL), F32),
        compiler_params=_cparams(("parallel",)),
        name="final_norm",
    )(h, p, g)


def _layer(h2, B, S, norm1_g, w_in, w_ret_out, w_att_out, w_out, norm2_g, peer_wq, peer_subkeys, peer_u, peer_v):
    T = B * S
    aq, ak, av = 3072, 3072 + ATT_W, 3072 + 2 * ATT_W
    att_cols = [w_in[:, o + g * ATT_OUT_W:o + (g + 1) * ATT_OUT_W] for g in range(len(ATT_GROUPS)) for o in (aq, ak, av)]
    w_perm = jnp.concatenate([w_in[:, :3072], w_in[:, 7680:9728]] + att_cols, axis=1).astype(BF16)
    main, a0, a1, a2 = _proj(h2, norm1_g.reshape(1, D_MODEL), w_perm, B, S)
    retg = _retention(main.reshape(B, S, MAIN_W)).reshape(T, RET_V_W)
    (o1, l1), (o2, l2), (o3, l3) = (_att_group(a, g) for g, a in enumerate((a0.reshape(B, 1, S, ATT_SEC_W), a1, a2)))
    sk = peer_subkeys.reshape(2 * PEER_HEADS, PEER_NKEYS, PEER_DKEY // 2).astype(BF16)
    h_mid, xn2, scT = _mix(retg, o1, o2, o3, l1, l2, l3, main, h2,
                           w_ret_out.astype(BF16), w_att_out.astype(BF16), w_out.astype(BF16),
                           norm2_g.reshape(1, D_MODEL), peer_wq.astype(BF16), sk)
    w, off, par = _peer_u(scT, xn2.reshape(T, 8, 128), _pack_table(peer_u))
    pout = _peer_v(off, par, w, _pack_table(peer_v))
    return h_mid, pout.reshape(T * 8, 128)


def kernel(x, norm1_g, w_in, w_ret_out, w_att_out, w_out, norm2_g, peer_wq, peer_subkeys, peer_u, peer_v, normf_g):
    B, S, D = x.shape
    assert w_in.shape[0] == 1, "single-layer block"
    h, pout = _layer(x.reshape(B * S, D), B, S, norm1_g[0], w_in[0], w_ret_out[0], w_att_out[0], w_out[0],
                     norm2_g[0], peer_wq[0], peer_subkeys[0], peer_u[0], peer_v[0])
    return _final(h, pout, normf_g.reshape(1, D)).reshape(B, S, D)
```

```python
import functools
import math

import jax
import jax.numpy as jnp
from jax import lax
from jax.experimental import pallas as pl
from jax.experimental.pallas import tpu as pltpu

F32 = jnp.float32
BF16 = jnp.bfloat16

D_MODEL = 1024
RET_HEADS = 8
RET_DK = 64
RET_DV = 128
RET_CHUNK = 128
ATT_GROUPS = ((128, 1), (512, 4), (2048, 16))
ATT_HPG = 4
ATT_HEADS = 12
ATT_DH = 128
ATT_BLOCK = 128
ATT_BATCH = 2
PEER_HEADS = 8
PEER_NKEYS = 128
PEER_DKEY = 256
PEER_TOPK = 16
PEER_PICKS = PEER_HEADS * PEER_TOPK
EPS = 1e-6

RET_QK_W = RET_HEADS * RET_DK
RET_V_W = RET_HEADS * RET_DV
ATT_W = ATT_HEADS * ATT_DH
ATT_OUT_W = ATT_HPG * ATT_DH
OFF_RQ, OFF_RK, OFF_RV, OFF_RG = 0, 512, 1024, 2048
OFF_GR, OFF_GA = 3072, 4096
MAIN_W = 5120
ATT_SEC_W = 3 * ATT_OUT_W
IN_W = MAIN_W + len(ATT_GROUPS) * ATT_SEC_W

VMEM_LIMIT_BYTES = 60 * 1024 * 1024

PROJ_TM = 256
PROJ_TN = 512
MIX_TM = 256
TOPK_TB = 512
PEER_TB = 256
PEER_UNROLL = 32
FINAL_TM = 512


def _cparams(sem):
    return pltpu.CompilerParams(dimension_semantics=sem, vmem_limit_bytes=VMEM_LIMIT_BYTES)


def _const_spec(shape):
    nd = len(shape)
    return pl.BlockSpec(shape, lambda *_: (0,) * nd)


def _proj_kernel(x_ref, g_ref, w_ref, main_ref, a0_ref, a1_ref, a2_ref, scr_ref):
    x = x_ref[...]
    ms = jnp.mean(x * x, axis=-1, keepdims=True)
    xn = (x * lax.rsqrt(ms + EPS) * g_ref[...]).astype(BF16)

    def chunk(j):
        return jnp.dot(xn, w_ref[:, j * PROJ_TN:(j + 1) * PROJ_TN], preferred_element_type=F32)

    nmain = MAIN_W // PROJ_TN
    for j in range(nmain):
        main_ref[:, j * PROJ_TN:(j + 1) * PROJ_TN] = chunk(j).astype(BF16)
    for s in range(3):
        a0_ref[:, s * PROJ_TN:(s + 1) * PROJ_TN] = chunk(nmain + s).astype(BF16)
    for g, a_ref in ((1, a1_ref), (2, a2_ref)):
        d = ATT_GROUPS[g][1]
        for s in range(3):
            res = chunk(nmain + 3 * g + s)
            for c in range(PROJ_TN // 128):
                scr_ref[c] = res[:, c * 128:(c + 1) * 128]
            for r in range(d):
                piece = jnp.concatenate(
                    [scr_ref[c, pl.ds(r, PROJ_TM // d, stride=d), :] for c in range(PROJ_TN // 128)], axis=1)
                a_ref[0, r, :, s * PROJ_TN:(s + 1) * PROJ_TN] = piece.astype(BF16)


def _proj(x2, g, w, B, S):
    T = x2.shape[0]
    tiles = S // PROJ_TM
    d1, d2 = ATT_GROUPS[1][1], ATT_GROUPS[2][1]
    return pl.pallas_call(
        _proj_kernel,
        grid=(T // PROJ_TM,),
        in_specs=[
            pl.BlockSpec((PROJ_TM, D_MODEL), lambda i: (i, 0)),
            _const_spec((1, D_MODEL)),
            pl.BlockSpec((D_MODEL, IN_W), lambda i: (0, 0), pipeline_mode=pl.Buffered(1)),
        ],
        out_specs=[
            pl.BlockSpec((PROJ_TM, MAIN_W), lambda i: (i, 0)),
            pl.BlockSpec((PROJ_TM, ATT_SEC_W), lambda i: (i, 0)),
            pl.BlockSpec((1, d1, PROJ_TM // d1, ATT_SEC_W), lambda i: (i // tiles, 0, i % tiles, 0)),
            pl.BlockSpec((1, d2, PROJ_TM // d2, ATT_SEC_W), lambda i: (i // tiles, 0, i % tiles, 0)),
        ],
        out_shape=[
            jax.ShapeDtypeStruct((T, MAIN_W), BF16),
            jax.ShapeDtypeStruct((T, ATT_SEC_W), BF16),
            jax.ShapeDtypeStruct((B, d1, S // d1, ATT_SEC_W), BF16),
            jax.ShapeDtypeStruct((B, d2, S // d2, ATT_SEC_W), BF16),
        ],
        scratch_shapes=[pltpu.VMEM((PROJ_TN // 128, PROJ_TM, 128), F32)],
        compiler_params=_cparams(("parallel",)),
        name="proj",
    )(x2, g, w)


def _retention_kernel(q_ref, k_ref, v_ref, rg_ref, o_ref, state_ref, decay_ref):
    C = RET_CHUNK

    @pl.when(pl.program_id(1) == 0)
    def _():
        state_ref[...] = jnp.zeros_like(state_ref)
        pi = lax.broadcasted_iota(jnp.int32, (C, C), 0)
        pj = lax.broadcasted_iota(jnp.int32, (C, C), 1)
        diff = (pi - pj).astype(F32)
        for h in range(RET_HEADS):
            lg = math.log1p(-(2.0 ** (-5.0 - h)))
            decay_ref[h] = jnp.where(diff >= 0, jnp.exp(jnp.maximum(diff, 0.0) * lg), 0.0)

    pos = lax.broadcasted_iota(jnp.int32, (C, 1), 0).astype(F32)
    q = q_ref[0]
    k = k_ref[0]
    v = v_ref[0]
    rg = rg_ref[0]
    for h in range(RET_HEADS):
        lg = math.log1p(-(2.0 ** (-5.0 - h)))
        decay = decay_ref[h]
        w_k = jnp.exp((C - 1 - pos) * lg)
        w_q = jnp.exp((pos + 1.0) * lg)
        qh = q[:, h * RET_DK:(h + 1) * RET_DK]
        kh = k[:, h * RET_DK:(h + 1) * RET_DK].astype(F32) * (RET_DK ** -0.5)
        vh = v[:, h * RET_DV:(h + 1) * RET_DV]
        s = lax.dot_general(qh, kh.astype(BF16), (((1,), (1,)), ((), ())), preferred_element_type=F32)
        p = (s * decay).astype(BF16)
        inner = jnp.dot(p, vh, preferred_element_type=F32)
        st = state_ref[h]
        cross = jnp.dot((qh.astype(F32) * w_q).astype(BF16), st.astype(BF16), preferred_element_type=F32)
        kw = (kh * w_k).astype(BF16)
        kv = lax.dot_general(kw, vh, (((0,), (0,)), ((), ())), preferred_element_type=F32)
        state_ref[h] = math.exp(C * lg) * st + kv
        ret = inner + cross
        rn = ret * lax.rsqrt(jnp.mean(ret * ret, axis=-1, keepdims=True) + EPS)
        g = rg[:, h * RET_DV:(h + 1) * RET_DV].astype(F32)
        o_ref[0, :, h * RET_DV:(h + 1) * RET_DV] = (g * jax.nn.sigmoid(g) * rn).astype(BF16)


def _retention(proj3):
    B, S, _ = proj3.shape
    n = S // RET_CHUNK
    return pl.pallas_call(
        _retention_kernel,
        grid=(B, n),
        in_specs=[
            pl.BlockSpec((1, RET_CHUNK, RET_QK_W), lambda b, c: (b, c, OFF_RQ // RET_QK_W)),
            pl.BlockSpec((1, RET_CHUNK, RET_QK_W), lambda b, c: (b, c, OFF_RK // RET_QK_W)),
            pl.BlockSpec((1, RET_CHUNK, RET_V_W), lambda b, c: (b, c, OFF_RV // RET_V_W)),
            pl.BlockSpec((1, RET_CHUNK, RET_V_W), lambda b, c: (b, c, OFF_RG // RET_V_W)),
        ],
        out_specs=pl.BlockSpec((1, RET_CHUNK, RET_V_W), lambda b, c: (b, c, 0)),
        out_shape=jax.ShapeDtypeStruct((B, S, RET_V_W), BF16),
        scratch_shapes=[pltpu.VMEM((RET_HEADS, RET_DK, RET_DV), F32),
                        pltpu.VMEM((RET_HEADS, RET_CHUNK, RET_CHUNK), F32)],
        compiler_params=_cparams(("parallel", "arbitrary")),
        name="retention",
    )(proj3, proj3, proj3, proj3)


def _att_kernel(q_ref, kp_ref, kc_ref, vp_ref, vc_ref, o_ref, lse_ref, *, group, dilation, span):
    n = pl.program_id(1)
    r = pl.program_id(2)
    iq = lax.broadcasted_iota(jnp.int32, (ATT_BLOCK, 2 * ATT_BLOCK), 0)
    jk = lax.broadcasted_iota(jnp.int32, (ATT_BLOCK, 2 * ATT_BLOCK), 1)
    dist = iq + ATT_BLOCK - jk
    valid = (dist >= 0) & (dist <= span) & ((jk >= ATT_BLOCK) | (n > 0))
    distf = (dilation * dist).astype(F32)
    rows = pl.ds(r, ATT_BLOCK, stride=dilation) if dilation > 1 else slice(None)
    for bb in range(ATT_BATCH):
        q = q_ref[bb, 0]
        kk = jnp.concatenate([kp_ref[bb, 0], kc_ref[bb, 0]], axis=0)
        vv = jnp.concatenate([vp_ref[bb, 0], vc_ref[bb, 0]], axis=0)
        for i in range(ATT_HPG):
            slope = 2.0 ** (-8.0 * (group * ATT_HPG + i + 1) / ATT_HEADS)
            sl = slice(i * ATT_DH, (i + 1) * ATT_DH)
            s = lax.dot_general(q[:, sl], kk[:, sl], (((1,), (1,)), ((), ())), preferred_element_type=F32)
            s = s * (ATT_DH ** -0.5)
            s = jnp.where(valid, s - slope * distf, -jnp.inf)
            m = jnp.max(s, axis=-1, keepdims=True)
            p = jnp.exp(s - m)
            den = jnp.sum(p, axis=-1, keepdims=True)
            o_ref[i, bb, rows, :] = jnp.dot(p.astype(BF16), vv[:, sl], preferred_element_type=F32) / den
            lse_ref[i, bb, rows, :] = jnp.broadcast_to(m + jnp.log(den), (ATT_BLOCK, ATT_DH))


def _att_group(qkv, group):
    window, d = ATT_GROUPS[group]
    B, _, L, _ = qkv.shape
    S = L * d
    nL = L // ATT_BLOCK

    def cur(sec):
        return pl.BlockSpec((ATT_BATCH, 1, ATT_BLOCK, ATT_OUT_W), lambda b, n, r: (b, r, n, sec))

    def prev(sec):
        return pl.BlockSpec((ATT_BATCH, 1, ATT_BLOCK, ATT_OUT_W), lambda b, n, r: (b, r, jnp.maximum(n - 1, 0), sec))

    out_spec = pl.BlockSpec((ATT_HPG, ATT_BATCH, ATT_BLOCK * d, ATT_DH), lambda b, n, r: (0, b, n, 0))
    out_shape = jax.ShapeDtypeStruct((ATT_HPG, B, S, ATT_DH), F32)
    o, lse = pl.pallas_call(
        functools.partial(_att_kernel, group=group, dilation=d, span=window // d),
        grid=(B // ATT_BATCH, nL, d),
        in_specs=[cur(0), prev(1), cur(1), prev(2), cur(2)],
        out_specs=[out_spec, out_spec],
        out_shape=[out_shape, out_shape],
        compiler_params=_cparams(("parallel", "arbitrary", "arbitrary")),
        name=f"att_g{group}",
    )(qkv, qkv, qkv, qkv, qkv)
    return o.reshape(ATT_HPG, B * S, ATT_DH), lse.reshape(ATT_HPG, B * S, ATT_DH)


def _mix_kernel(retg_ref, o1_ref, o2_ref, o3_ref, l1_ref, l2_ref, l3_ref, gr_ref, ga_ref, x_ref,
                wro_ref, wao_ref, wo_ref, g2_ref, wq_ref, sk_ref, h_ref, xn_ref, sc_ref):
    heads = lambda ref: jnp.concatenate([ref[i] for i in range(ATT_HPG)], axis=1)
    l1, l2, l3 = heads(l1_ref), heads(l2_ref), heads(l3_ref)
    mx = jnp.maximum(jnp.maximum(l1, l2), l3)
    e1, e2, e3 = jnp.exp(l1 - mx), jnp.exp(l2 - mx), jnp.exp(l3 - mx)
    att = (e1 * heads(o1_ref) + e2 * heads(o2_ref) + e3 * heads(o3_ref)) / (e1 + e2 + e3)
    a_branch = jnp.dot(att.astype(BF16), wao_ref[...], preferred_element_type=F32)
    r_branch = jnp.dot(retg_ref[...], wro_ref[...], preferred_element_type=F32)
    merged = (jax.nn.sigmoid(gr_ref[...].astype(F32)) * r_branch
              + jax.nn.sigmoid(ga_ref[...].astype(F32)) * a_branch)
    h = x_ref[...] + jnp.dot(merged.astype(BF16), wo_ref[...], preferred_element_type=F32)
    h_ref[...] = h
    xn = h * lax.rsqrt(jnp.mean(h * h, axis=-1, keepdims=True) + EPS) * g2_ref[...]
    for c in range(D_MODEL // 128):
        xn_ref[pl.ds(c, h.shape[0], stride=D_MODEL // 128), :] = xn[:, c * 128:(c + 1) * 128]
    qp = jnp.dot(xn.astype(BF16), wq_ref[...], preferred_element_type=F32).astype(BF16)
    half = PEER_DKEY // 2
    for hp in range(2 * PEER_HEADS):
        sc_ref[hp] = lax.dot_general(sk_ref[hp], qp[:, hp * half:(hp + 1) * half],
                                     (((1,), (1,)), ((), ())), preferred_element_type=F32)


def _mix(retg, o1, o2, o3, l1, l2, l3, proj, x2, wro, wao, wo, g2, wq, sk):
    T = x2.shape[0]
    tm = MIX_TM
    row = lambda w: pl.BlockSpec((tm, w), lambda i: (i, 0))
    hd = pl.BlockSpec((ATT_HPG, tm, ATT_DH), lambda i: (0, i, 0))
    nhp = 2 * PEER_HEADS
    nt = D_MODEL // 128
    return pl.pallas_call(
        _mix_kernel,
        grid=(T // tm,),
        in_specs=[
            row(RET_V_W), hd, hd, hd, hd, hd, hd,
            pl.BlockSpec((tm, D_MODEL), lambda i: (i, OFF_GR // D_MODEL)),
            pl.BlockSpec((tm, D_MODEL), lambda i: (i, OFF_GA // D_MODEL)),
            row(D_MODEL),
            _const_spec(wro.shape), _const_spec(wao.shape), _const_spec(wo.shape),
            _const_spec(g2.shape), _const_spec(wq.shape), _const_spec(sk.shape),
        ],
        out_specs=[row(D_MODEL), pl.BlockSpec((tm * nt, 128), lambda i: (i, 0)),
                   pl.BlockSpec((nhp, PEER_NKEYS, tm), lambda i: (0, 0, i))],
        out_shape=[jax.ShapeDtypeStruct((T, D_MODEL), F32),
                   jax.ShapeDtypeStruct((T * nt, 128), F32),
                   jax.ShapeDtypeStruct((nhp, PEER_NKEYS, T), F32)],
        compiler_params=_cparams(("parallel",)),
        name="mix",
    )(retg, o1, o2, o3, l1, l2, l3, proj, proj, x2, wro, wao, wo, g2, wq, sk)


def _topk_rows(s, ids, k, fill):
    vals, idxs = [], []
    for _ in range(k):
        m = jnp.max(s, axis=0, keepdims=True)
        i = jnp.min(jnp.where(s == m, ids, fill), axis=0, keepdims=True)
        vals.append(m)
        idxs.append(i)
        s = jnp.where(ids == i, -jnp.inf, s)
    return jnp.concatenate(vals, axis=0), jnp.concatenate(idxs, axis=0)


_CAND_B = [PEER_TOPK // (a + 1) for a in range(PEER_TOPK)]
_CAND_ROWS = -(-sum(_CAND_B) // 8) * 8


def _topk_consts(W):
    K = PEER_TOPK
    key_ids = lax.broadcasted_iota(jnp.int32, (PEER_NKEYS, W), 0)
    npad = _CAND_ROWS - sum(_CAND_B)
    cand_ids = jnp.concatenate(
        [jnp.full((1, W), a * K + b, jnp.int32) for a in range(K) for b in range(_CAND_B[a])]
        + [jnp.full((npad, W), K * K, jnp.int32)], axis=0)
    pad = jnp.full((npad, W), -jnp.inf, F32)
    return key_ids, cand_ids, pad


def _topk_head(sc_ref, h, consts):
    K = PEER_TOPK
    key_ids, cand_ids, pad = consts
    v0, i0 = _topk_rows(sc_ref[2 * h], key_ids, K, PEER_NKEYS)
    v1, i1 = _topk_rows(sc_ref[2 * h + 1], key_ids, K, PEER_NKEYS)
    cand = jnp.concatenate([v0[a:a + 1] + v1[0:_CAND_B[a]] for a in range(K)] + [pad], axis=0)
    best_s, best_c = _topk_rows(cand, cand_ids, K, K * K)
    ca = best_c // K
    cb = best_c % K
    ia = jnp.zeros_like(best_c)
    ib = jnp.zeros_like(best_c)
    for a in range(K):
        ia = jnp.where(ca == a, i0[a:a + 1], ia)
        ib = jnp.where(cb == a, i1[a:a + 1], ib)
    ex = jnp.exp(best_s - best_s[0:1])
    return (ia * PEER_NKEYS + ib).astype(F32), ex / jnp.sum(ex, axis=0, keepdims=True)


def _topk_finish(e_scr, g_scr, off_ref, par_ref, gate_ref):
    W = e_scr.shape[-1]
    e = e_scr[...].reshape(PEER_PICKS, W).T.astype(jnp.int32)
    off_ref[...] = lax.shift_right_logical(e, 1) * 8
    par_ref[...] = (e & 1).astype(F32)
    gate_ref[...] = g_scr[...].reshape(PEER_PICKS, W).T


def _topk_kernel(sc_ref, off_ref, par_ref, gate_ref, e_scr, g_scr):
    consts = _topk_consts(sc_ref.shape[-1])

    def head(h, carry):
        e_scr[h], g_scr[h] = _topk_head(sc_ref, h, consts)
        return carry

    lax.fori_loop(0, PEER_HEADS, head, 0)
    _topk_finish(e_scr, g_scr, off_ref, par_ref, gate_ref)


def _topk(scT):
    T = scT.shape[-1]
    tb = min(TOPK_TB, T)
    out_spec = pl.BlockSpec((tb, PEER_PICKS), lambda i: (i, 0))
    return pl.pallas_call(
        _topk_kernel,
        grid=(T // tb,),
        in_specs=[pl.BlockSpec((2 * PEER_HEADS, PEER_NKEYS, tb), lambda i: (0, 0, i))],
        out_specs=[out_spec, out_spec, out_spec],
        out_shape=[jax.ShapeDtypeStruct((T, PEER_PICKS), jnp.int32),
                   jax.ShapeDtypeStruct((T, PEER_PICKS), F32),
                   jax.ShapeDtypeStruct((T, PEER_PICKS), F32)],
        scratch_shapes=[pltpu.VMEM((PEER_HEADS, PEER_TOPK, tb), F32),
                        pltpu.VMEM((PEER_HEADS, PEER_TOPK, tb), F32)],
        compiler_params=_cparams(("parallel",)),
        name="topk",
    )(scT)


PEER_U_VALU = 16
PEER_V_VALU = 16


PACK_TE = 2048


def _pack_kernel(x_ref, o_ref):
    s = pl.program_id(1)
    half = PACK_TE // 2
    bits = lambda v: pltpu.bitcast(v.astype(BF16).astype(F32), jnp.uint32)
    even = bits(x_ref[pl.ds(0, half, stride=2), :])
    odd = bits(x_ref[pl.ds(1, half, stride=2), :])
    o_ref[pl.ds(s, half, stride=D_MODEL // 128), :] = (even >> 16) | (odd & jnp.uint32(0xFFFF0000))


def _pack_table(tab):
    E = tab.shape[0]
    return pl.pallas_call(
        _pack_kernel,
        grid=(E // PACK_TE, D_MODEL // 128),
        in_specs=[pl.BlockSpec((PACK_TE, 128), lambda i, s: (i, s))],
        out_specs=pl.BlockSpec((4 * PACK_TE, 128), lambda i, s: (i, 0)),
        out_shape=jax.ShapeDtypeStruct((4 * E, 128), jnp.uint32),
        compiler_params=_cparams(("parallel", "arbitrary")),
        name="pack_table",
    )(tab)


def _split3(a):
    p0 = a.astype(BF16)
    r1 = a - p0.astype(F32)
    p1 = r1.astype(BF16)
    p2 = (r1 - p1.astype(F32)).astype(BF16)
    return p0, p1, p2


def _pair_expand(v, which, base, pairs, pieces=_split3):
    j = lax.broadcasted_iota(jnp.int32, (PEER_PICKS, 16 * pairs), 0)
    k = lax.broadcasted_iota(jnp.int32, (PEER_PICKS, 16 * pairs), 1)
    onehot = jnp.where(j == base + 2 * (k // 16) + which, 1.0, 0.0).astype(BF16)
    return sum(jnp.dot(p, onehot, preferred_element_type=F32) for p in pieces(v))


def _pair_targets(par, which, base, pairs):
    r = lax.broadcasted_iota(jnp.int32, (par.shape[0], 16 * pairs), 1) % 16
    one_piece = lambda p: (p.astype(BF16),)
    return 0.5 * (r.astype(F32) - _pair_expand(par, which, base, pairs, one_piece))


def _tile_words(tab_ref, off_ref, t, j):
    off = pl.multiple_of(off_ref[t, j], 8)
    return tab_ref[pl.ds(off, 8), :]


def _fill_lane_broadcasts(dst_ref, src, n):
    for j in range(n):
        dst_ref[j] = jnp.broadcast_to(src[:, j:j + 1], dst_ref.shape[1:])


def _expert_rows(words, shv):
    return pltpu.bitcast((words << shv) & jnp.uint32(0xFFFF0000), F32)


def _sublane_sums(ps):
    sub = lax.broadcasted_iota(jnp.int32, (8, 128), 0)
    lo4 = sub < 4
    t = [jnp.where(lo4, ps[j], ps[j + 4]) + pltpu.roll(jnp.where(lo4, ps[j + 4], ps[j]), 4, 0)
         for j in range(4)]
    m2 = (sub & 2) == 0
    u = [jnp.where(m2, t[j] + pltpu.roll(t[j], 6, 0), t[j + 2] + pltpu.roll(t[j + 2], 2, 0))
         for j in range(2)]
    m1 = (sub & 1) == 0
    return jnp.where(m1, u[0] + pltpu.roll(u[0], 7, 0), u[1] + pltpu.roll(u[1], 1, 0))


def _token_loop(tb, body, per_group=None):
    def group(i, carry):
        if per_group is not None:
            per_group(i)
        for s in range(PEER_UNROLL):
            body(i * PEER_UNROLL + s)
        return carry

    lax.fori_loop(0, tb // PEER_UNROLL, group, 0)


def _peer_u_kernel(off0_ref, par0_ref, gate0_ref, x_ref, tab_ref, sc_ref, w_ref, offo_ref, paro_ref,
                   off_ref, sem_ref, offv_ref, par_ref, gate_ref, e_scr, g_scr, tga_ref, tgb_ref, v3_ref, shb_ref,
                   actv_ref):
    tb = x_ref.shape[0]
    nv = PEER_U_VALU
    pairs = (PEER_PICKS - nv) // 2
    nblk = 16 * pairs // 128

    @pl.when(pl.program_id(0) == 0)
    def _():
        offv_ref[...] = off0_ref[...]
        par_ref[...] = par0_ref[...]
        gate_ref[...] = gate0_ref[...]

    def offsets_copy(g):
        rows = pl.ds(g * PEER_UNROLL, PEER_UNROLL)
        return pltpu.make_async_copy(offv_ref.at[rows], off_ref.at[rows], sem_ref.at[g])

    for g in range(tb // PEER_UNROLL):
        offsets_copy(g).start()
    offo_ref[...] = offv_ref[...]
    par = par_ref[...]
    paro_ref[...] = par
    tga_ref[...] = _pair_targets(par, 0, nv, pairs)
    tgb_ref[...] = _pair_targets(par, 1, nv, pairs)
    _fill_lane_broadcasts(shb_ref, ((1.0 - par) * 16.0).astype(jnp.uint32), nv)
    m_iota = lax.broadcasted_iota(jnp.int32, (8, 16 * pairs), 0).astype(F32)
    zero = jnp.zeros((8, 128), F32)
    ones = jnp.ones((8, 128), BF16)
    nt = (((1,), (1,)), ((), ()))

    def body(t):
        xt = x_ref[t]
        row = pl.ds(t, 1)
        rs = []
        for g in range(nv // 8):
            ps = []
            for jj in range(8):
                j = g * 8 + jj
                shv = jnp.broadcast_to(shb_ref[j, row, :], (8, 128))
                ps.append(_expert_rows(_tile_words(tab_ref, off_ref, t, j), shv) * xt)
            rs.append(_sublane_sums(ps))
        r = jnp.concatenate(rs, axis=0)
        hi = r.astype(BF16)
        lo = (r - hi.astype(F32)).astype(BF16)
        av = (lax.dot_general(ones, hi, nt, preferred_element_type=F32)
              + lax.dot_general(ones, lo, nt, preferred_element_type=F32))
        actv_ref[row, 0:nv] = av[0:1]
        tiles = [pltpu.bitcast(_tile_words(tab_ref, off_ref, t, j), BF16) for j in range(nv, PEER_PICKS)]
        wmat = jnp.concatenate(
            [jnp.concatenate([tiles[2 * q], tiles[2 * q + 1]], axis=1) for q in range(pairs)], axis=0)
        xf = jnp.concatenate([jnp.concatenate([xt, zero], axis=1),
                              jnp.concatenate([zero, xt], axis=1)], axis=0)
        xh = xf.astype(BF16)
        xl = (xf - xh.astype(F32)).astype(BF16)
        o = lax.dot_general(jnp.concatenate([xh, xl], axis=0), wmat, nt, preferred_element_type=F32)
        o = o[0:16] + o[16:32]
        za = jnp.where(tga_ref[row, :] == m_iota, o[0:8], 0.0)
        zb = jnp.where(tgb_ref[row, :] == m_iota, o[8:16], 0.0)
        for half, z in enumerate((za, zb)):
            v = _sublane_sums([z[:, i * 128:(i + 1) * 128] for i in range(nblk)] + [zero] * (8 - nblk))
            for i in range(nblk):
                v3_ref[half * nblk + i, row, :] = v[i:i + 1]

    consts = _topk_consts(tb)

    def next_block_head(h):
        offsets_copy(h).wait()
        e_scr[h], g_scr[h] = _topk_head(sc_ref, h, consts)

    _token_loop(tb, body, next_block_head)
    lane = lax.broadcasted_iota(jnp.int32, (128, PEER_PICKS), 0)
    pick = lax.broadcasted_iota(jnp.int32, (128, PEER_PICKS), 1)
    act = jnp.zeros((tb, PEER_PICKS), F32)
    for c in range(2 * nblk):
        sel = jnp.where(pick == nv + 2 * (8 * (c % nblk) + lane // 16) + c // nblk, 1.0, 0.0).astype(BF16)
        for p in _split3(v3_ref[c]):
            act = act + jnp.dot(p, sel, preferred_element_type=F32)
    act = jnp.where(pick[0:1] < nv, actv_ref[...], act)
    gelu = 0.5 * act * (1.0 + lax.erf(act * (2.0 ** -0.5)))
    w_ref[...] = gate_ref[...] * gelu
    _topk_finish(e_scr, g_scr, offv_ref, par_ref, gate_ref)


def _peer_specs(tb, tab):
    rows = pl.BlockSpec((tb, PEER_PICKS), lambda i: (i, 0))
    smem = pl.BlockSpec((tb, PEER_PICKS), lambda i: (i, 0), memory_space=pltpu.SMEM)
    table = pl.BlockSpec(tab.shape, lambda i: (0, 0), pipeline_mode=pl.Buffered(1))
    return rows, smem, table


def _peer_u(scT, x3, tab):
    T = x3.shape[0]
    tb = PEER_TB
    assert tb // PEER_UNROLL == PEER_HEADS
    nb = T // tb
    rows, _, table = _peer_specs(tb, tab)
    first = pl.BlockSpec((tb, PEER_PICKS), lambda i: (0, 0))
    pairs = (PEER_PICKS - PEER_U_VALU) // 2
    off0, par0, gate0 = _topk(scT[:, :, :tb])
    return pl.pallas_call(
        _peer_u_kernel,
        grid=(nb,),
        in_specs=[first, first, first, pl.BlockSpec((tb, 8, 128), lambda i: (i, 0, 0)), table,
                  pl.BlockSpec((2 * PEER_HEADS, PEER_NKEYS, tb), lambda i: (0, 0, jnp.minimum(i + 1, nb - 1)))],
        out_specs=[rows, rows, rows],
        out_shape=[jax.ShapeDtypeStruct((T, PEER_PICKS), F32),
                   jax.ShapeDtypeStruct((T, PEER_PICKS), jnp.int32),
                   jax.ShapeDtypeStruct((T, PEER_PICKS), F32)],
        scratch_shapes=[pltpu.SMEM((tb, PEER_PICKS), jnp.int32),
                        pltpu.SemaphoreType.DMA((tb // PEER_UNROLL,)),
                        pltpu.VMEM((tb, PEER_PICKS), jnp.int32),
                        pltpu.VMEM((tb, PEER_PICKS), F32), pltpu.VMEM((tb, PEER_PICKS), F32),
                        pltpu.VMEM((PEER_HEADS, PEER_TOPK, tb), F32), pltpu.VMEM((PEER_HEADS, PEER_TOPK, tb), F32),
                        pltpu.VMEM((tb, 16 * pairs), F32), pltpu.VMEM((tb, 16 * pairs), F32),
                        pltpu.VMEM((pairs // 4, tb, 128), F32),
                        pltpu.VMEM((PEER_U_VALU, tb, 128), jnp.uint32),
                        pltpu.VMEM((tb, PEER_PICKS), F32)],
        compiler_params=_cparams(("arbitrary",)),
        name="peer_u",
    )(off0, par0, gate0, x3, tab, scT)


def _peer_v_kernel(off_ref, par_ref, w_ref, tab_ref, o_ref, wa_ref, wb_ref, tga_ref, tgb_ref, shb_ref, wbc_ref):
    tb = o_ref.shape[0]
    nv = PEER_V_VALU
    pairs = (PEER_PICKS - nv) // 2
    par = par_ref[...]
    w = w_ref[...]
    wa_ref[...] = _pair_expand(w, 0, nv, pairs)
    wb_ref[...] = _pair_expand(w, 1, nv, pairs)
    tga_ref[...] = _pair_targets(par, 0, nv, pairs)
    tgb_ref[...] = _pair_targets(par, 1, nv, pairs)
    _fill_lane_broadcasts(shb_ref, ((1.0 - par) * 16.0).astype(jnp.uint32), nv)
    _fill_lane_broadcasts(wbc_ref, w, nv)
    m_iota = lax.broadcasted_iota(jnp.int32, (8, 16 * pairs), 0).astype(F32)
    nacc = 4

    def body(t):
        row = pl.ds(t, 1)
        acc = [jnp.zeros((8, 128), F32) for _ in range(nacc)]
        for j in range(nv):
            shv = jnp.broadcast_to(shb_ref[j, row, :], (8, 128))
            wv = jnp.broadcast_to(wbc_ref[j, row, :], (8, 128))
            acc[j % nacc] = acc[j % nacc] + wv * _expert_rows(_tile_words(tab_ref, off_ref, t, j), shv)
        tiles = [pltpu.bitcast(_tile_words(tab_ref, off_ref, t, j), BF16) for j in range(nv, PEER_PICKS)]
        vmat = jnp.concatenate([jnp.concatenate(tiles[0::2], axis=0),
                                jnp.concatenate(tiles[1::2], axis=0)], axis=1)
        la = jnp.where(tga_ref[row, :] == m_iota, wa_ref[row, :], 0.0)
        lb = jnp.where(tgb_ref[row, :] == m_iota, wb_ref[row, :], 0.0)
        lf = jnp.concatenate([la, lb], axis=0)
        lh = lf.astype(BF16)
        ll = (lf - lh.astype(F32)).astype(BF16)
        o = jnp.dot(jnp.concatenate([lh, ll], axis=0), vmat, preferred_element_type=F32)
        o_ref[t] = (((o[0:8, 0:128] + o[8:16, 128:256]) + (o[16:24, 0:128] + o[24:32, 128:256]))
                    + ((acc[0] + acc[1]) + (acc[2] + acc[3])))

    _token_loop(tb, body)


def _peer_v(off, par, w, tab):
    T = par.shape[0]
    tb = PEER_TB
    rows, smem, table = _peer_specs(tb, tab)
    pairs = (PEER_PICKS - PEER_V_VALU) // 2
    return pl.pallas_call(
        _peer_v_kernel,
        grid=(T // tb,),
        in_specs=[smem, rows, rows, table],
        out_specs=pl.BlockSpec((tb, 8, 128), lambda i: (i, 0, 0)),
        out_shape=jax.ShapeDtypeStruct((T, 8, 128), F32),
        scratch_shapes=[pltpu.VMEM((tb, 16 * pairs), F32) for _ in range(4)]
        + [pltpu.VMEM((PEER_V_VALU, tb, 128), jnp.uint32), pltpu.VMEM((PEER_V_VALU, tb, 128), F32)],
        compiler_params=_cparams(("parallel",)),
        name="peer_v",
    )(off, par, w, tab)


def _final_kernel(h_ref, p_ref, g_ref, o_ref):
    nt = D_MODEL // 128
    tm = h_ref.shape[0]
    p = jnp.concatenate([p_ref[pl.ds(c, tm, stride=nt), :] for c in range(nt)], axis=1)
    h = h_ref[...] + p
    o_ref[...] = h * lax.rsqrt(jnp.mean(h * h, axis=-1, keepdims=True) + EPS) * g_ref[...]


def _final(h, p, g):
    T = h.shape[0]
    row = pl.BlockSpec((FINAL_TM, D_MODEL), lambda i: (i, 0))
    return pl.pallas_call(
        _final_kernel,
        grid=(T // FINAL_TM,),
        in_specs=[row, pl.BlockSpec((FINAL_TM * (D_MODEL // 128), 128), lambda i: (i, 0)),
                  _const_spec((1, D_MODEL))],
        out_specs=row,
        out_shape=jax.ShapeDtypeStruct((T, D_MODEL), F32),
        compiler_params=_cparams(("parallel",)),
        name="final_norm",
    )(h, p, g)


def _layer(h2, B, S, norm1_g, w_in, w_ret_out, w_att_out, w_out, norm2_g, peer_wq, peer_subkeys, peer_u, peer_v):
    T = B * S
    aq, ak, av = 3072, 3072 + ATT_W, 3072 + 2 * ATT_W
    att_cols = [w_in[:, o + g * ATT_OUT_W:o + (g + 1) * ATT_OUT_W] for g in range(len(ATT_GROUPS)) for o in (aq, ak, av)]
    w_perm = jnp.concatenate([w_in[:, :3072], w_in[:, 7680:9728]] + att_cols, axis=1).astype(BF16)
    main, a0, a1, a2 = _proj(h2, norm1_g.reshape(1, D_MODEL), w_perm, B, S)
    retg = _retention(main.reshape(B, S, MAIN_W)).reshape(T, RET_V_W)
    (o1, l1), (o2, l2), (o3, l3) = (_att_group(a, g) for g, a in enumerate((a0.reshape(B, 1, S, ATT_SEC_W), a1, a2)))
    sk = peer_subkeys.reshape(2 * PEER_HEADS, PEER_NKEYS, PEER_DKEY // 2).astype(BF16)
    h_mid, xn2, scT = _mix(retg, o1, o2, o3, l1, l2, l3, main, h2,
                           w_ret_out.astype(BF16), w_att_out.astype(BF16), w_out.astype(BF16),
                           norm2_g.reshape(1, D_MODEL), peer_wq.astype(BF16), sk)
    w, off, par = _peer_u(scT, xn2.reshape(T, 8, 128), _pack_table(peer_u))
    pout = _peer_v(off, par, w, _pack_table(peer_v))
    return h_mid, pout.reshape(T * 8, 128)


def kernel(x, norm1_g, w_in, w_ret_out, w_att_out, w_out, norm2_g, peer_wq, peer_subkeys, peer_u, peer_v, normf_g):
    B, S, D = x.shape
    assert w_in.shape[0] == 1, "single-layer block"
    h, pout = _layer(x.reshape(B * S, D), B, S, norm1_g[0], w_in[0], w_ret_out[0], w_att_out[0], w_out[0],
                     norm2_g[0], peer_wq[0], peer_subkeys[0], peer_u[0], peer_v[0])
    return _final(h, pout, normf_g.reshape(1, D)).reshape(B, S, D)
```

```python
import functools
import math

import jax
import jax.numpy as jnp
from jax import lax
from jax.experimental import pallas as pl
from jax.experimental.pallas import tpu as pltpu

F32 = jnp.float32
BF16 = jnp.bfloat16

D_MODEL = 1024
RET_HEADS = 8
RET_DK = 64
RET_DV = 128
RET_CHUNK = 128
ATT_GROUPS = ((128, 1), (512, 4), (2048, 16))
ATT_HPG = 4
ATT_HEADS = 12
ATT_DH = 128
ATT_BLOCK = 128
ATT_BATCH = 2
PEER_HEADS = 8
PEER_NKEYS = 128
PEER_DKEY = 256
PEER_TOPK = 16
PEER_PICKS = PEER_HEADS * PEER_TOPK
EPS = 1e-6

RET_QK_W = RET_HEADS * RET_DK
RET_V_W = RET_HEADS * RET_DV
ATT_W = ATT_HEADS * ATT_DH
ATT_OUT_W = ATT_HPG * ATT_DH
OFF_RQ, OFF_RK, OFF_RV, OFF_RG = 0, 512, 1024, 2048
OFF_GR, OFF_GA = 3072, 4096
MAIN_W = 5120
ATT_SEC_W = 3 * ATT_OUT_W
IN_W = MAIN_W + len(ATT_GROUPS) * ATT_SEC_W

VMEM_LIMIT_BYTES = 60 * 1024 * 1024

PROJ_TM = 256
PROJ_TN = 512
MIX_TM = 256
TOPK_TB = 512
PEER_TB = 256
PEER_UNROLL = 32
FINAL_TM = 512


def _cparams(sem):
    return pltpu.CompilerParams(dimension_semantics=sem, vmem_limit_bytes=VMEM_LIMIT_BYTES)


def _const_spec(shape):
    nd = len(shape)
    return pl.BlockSpec(shape, lambda *_: (0,) * nd)


def _proj_kernel(x_ref, g_ref, w_ref, main_ref, a0_ref, a1_ref, a2_ref, scr_ref):
    x = x_ref[...]
    ms = jnp.mean(x * x, axis=-1, keepdims=True)
    xn = (x * lax.rsqrt(ms + EPS) * g_ref[...]).astype(BF16)

    def chunk(j):
        return jnp.dot(xn, w_ref[:, j * PROJ_TN:(j + 1) * PROJ_TN], preferred_element_type=F32)

    nmain = MAIN_W // PROJ_TN
    for j in range(nmain):
        main_ref[:, j * PROJ_TN:(j + 1) * PROJ_TN] = chunk(j).astype(BF16)
    for s in range(3):
        a0_ref[:, s * PROJ_TN:(s + 1) * PROJ_TN] = chunk(nmain + s).astype(BF16)
    for g, a_ref in ((1, a1_ref), (2, a2_ref)):
        d = ATT_GROUPS[g][1]
        for s in range(3):
            res = chunk(nmain + 3 * g + s)
            for c in range(PROJ_TN // 128):
                scr_ref[c] = res[:, c * 128:(c + 1) * 128]
            for r in range(d):
                piece = jnp.concatenate(
                    [scr_ref[c, pl.ds(r, PROJ_TM // d, stride=d), :] for c in range(PROJ_TN // 128)], axis=1)
                a_ref[0, r, :, s * PROJ_TN:(s + 1) * PROJ_TN] = piece.astype(BF16)


def _proj(x2, g, w, B, S):
    T = x2.shape[0]
    tiles = S // PROJ_TM
    d1, d2 = ATT_GROUPS[1][1], ATT_GROUPS[2][1]
    return pl.pallas_call(
        _proj_kernel,
        grid=(T // PROJ_TM,),
        in_specs=[
            pl.BlockSpec((PROJ_TM, D_MODEL), lambda i: (i, 0)),
            _const_spec((1, D_MODEL)),
            pl.BlockSpec((D_MODEL, IN_W), lambda i: (0, 0), pipeline_mode=pl.Buffered(1)),
        ],
        out_specs=[
            pl.BlockSpec((PROJ_TM, MAIN_W), lambda i: (i, 0)),
            pl.BlockSpec((PROJ_TM, ATT_SEC_W), lambda i: (i, 0)),
            pl.BlockSpec((1, d1, PROJ_TM // d1, ATT_SEC_W), lambda i: (i // tiles, 0, i % tiles, 0)),
            pl.BlockSpec((1, d2, PROJ_TM // d2, ATT_SEC_W), lambda i: (i // tiles, 0, i % tiles, 0)),
        ],
        out_shape=[
            jax.ShapeDtypeStruct((T, MAIN_W), BF16),
            jax.ShapeDtypeStruct((T, ATT_SEC_W), BF16),
            jax.ShapeDtypeStruct((B, d1, S // d1, ATT_SEC_W), BF16),
            jax.ShapeDtypeStruct((B, d2, S // d2, ATT_SEC_W), BF16),
        ],
        scratch_shapes=[pltpu.VMEM((PROJ_TN // 128, PROJ_TM, 128), F32)],
        compiler_params=_cparams(("parallel",)),
        name="proj",
    )(x2, g, w)


def _retention_kernel(q_ref, k_ref, v_ref, rg_ref, o_ref, state_ref, decay_ref):
    C = RET_CHUNK

    @pl.when(pl.program_id(1) == 0)
    def _():
        state_ref[...] = jnp.zeros_like(state_ref)
        pi = lax.broadcasted_iota(jnp.int32, (C, C), 0)
        pj = lax.broadcasted_iota(jnp.int32, (C, C), 1)
        diff = (pi - pj).astype(F32)
        for h in range(RET_HEADS):
            lg = math.log1p(-(2.0 ** (-5.0 - h)))
            decay_ref[h] = jnp.where(diff >= 0, jnp.exp(jnp.maximum(diff, 0.0) * lg), 0.0)

    pos = lax.broadcasted_iota(jnp.int32, (C, 1), 0).astype(F32)
    q = q_ref[0]
    k = k_ref[0]
    v = v_ref[0]
    rg = rg_ref[0]
    for h in range(RET_HEADS):
        lg = math.log1p(-(2.0 ** (-5.0 - h)))
        decay = decay_ref[h]
        w_k = jnp.exp((C - 1 - pos) * lg)
        w_q = jnp.exp((pos + 1.0) * lg)
        qh = q[:, h * RET_DK:(h + 1) * RET_DK]
        kh = k[:, h * RET_DK:(h + 1) * RET_DK].astype(F32) * (RET_DK ** -0.5)
        vh = v[:, h * RET_DV:(h + 1) * RET_DV]
        s = lax.dot_general(qh, kh.astype(BF16), (((1,), (1,)), ((), ())), preferred_element_type=F32)
        p = (s * decay).astype(BF16)
        inner = jnp.dot(p, vh, preferred_element_type=F32)
        st = state_ref[h]
        cross = jnp.dot((qh.astype(F32) * w_q).astype(BF16), st.astype(BF16), preferred_element_type=F32)
        kw = (kh * w_k).astype(BF16)
        kv = lax.dot_general(kw, vh, (((0,), (0,)), ((), ())), preferred_element_type=F32)
        state_ref[h] = math.exp(C * lg) * st + kv
        ret = inner + cross
        rn = ret * lax.rsqrt(jnp.mean(ret * ret, axis=-1, keepdims=True) + EPS)
        g = rg[:, h * RET_DV:(h + 1) * RET_DV].astype(F32)
        o_ref[0, :, h * RET_DV:(h + 1) * RET_DV] = (g * jax.nn.sigmoid(g) * rn).astype(BF16)


def _retention(proj3):
    B, S, _ = proj3.shape
    n = S // RET_CHUNK
    return pl.pallas_call(
        _retention_kernel,
        grid=(B, n),
        in_specs=[
            pl.BlockSpec((1, RET_CHUNK, RET_QK_W), lambda b, c: (b, c, OFF_RQ // RET_QK_W)),
            pl.BlockSpec((1, RET_CHUNK, RET_QK_W), lambda b, c: (b, c, OFF_RK // RET_QK_W)),
            pl.BlockSpec((1, RET_CHUNK, RET_V_W), lambda b, c: (b, c, OFF_RV // RET_V_W)),
            pl.BlockSpec((1, RET_CHUNK, RET_V_W), lambda b, c: (b, c, OFF_RG // RET_V_W)),
        ],
        out_specs=pl.BlockSpec((1, RET_CHUNK, RET_V_W), lambda b, c: (b, c, 0)),
        out_shape=jax.ShapeDtypeStruct((B, S, RET_V_W), BF16),
        scratch_shapes=[pltpu.VMEM((RET_HEADS, RET_DK, RET_DV), F32),
                        pltpu.VMEM((RET_HEADS, RET_CHUNK, RET_CHUNK), F32)],
        compiler_params=_cparams(("parallel", "arbitrary")),
        name="retention",
    )(proj3, proj3, proj3, proj3)


def _att_kernel(q_ref, kp_ref, kc_ref, vp_ref, vc_ref, o_ref, lse_ref, *, group, dilation, span):
    n = pl.program_id(1)
    r = pl.program_id(2)
    iq = lax.broadcasted_iota(jnp.int32, (ATT_BLOCK, 2 * ATT_BLOCK), 0)
    jk = lax.broadcasted_iota(jnp.int32, (ATT_BLOCK, 2 * ATT_BLOCK), 1)
    dist = iq + ATT_BLOCK - jk
    valid = (dist >= 0) & (dist <= span) & ((jk >= ATT_BLOCK) | (n > 0))
    distf = (dilation * dist).astype(F32)
    rows = pl.ds(r, ATT_BLOCK, stride=dilation) if dilation > 1 else slice(None)
    for bb in range(ATT_BATCH):
        q = q_ref[bb, 0]
        kk = jnp.concatenate([kp_ref[bb, 0], kc_ref[bb, 0]], axis=0)
        vv = jnp.concatenate([vp_ref[bb, 0], vc_ref[bb, 0]], axis=0)
        for i in range(ATT_HPG):
            slope = 2.0 ** (-8.0 * (group * ATT_HPG + i + 1) / ATT_HEADS)
            sl = slice(i * ATT_DH, (i + 1) * ATT_DH)
            s = lax.dot_general(q[:, sl], kk[:, sl], (((1,), (1,)), ((), ())), preferred_element_type=F32)
            s = s * (ATT_DH ** -0.5)
            s = jnp.where(valid, s - slope * distf, -jnp.inf)
            m = jnp.max(s, axis=-1, keepdims=True)
            p = jnp.exp(s - m)
            den = jnp.sum(p, axis=-1, keepdims=True)
            o_ref[i, bb, rows, :] = jnp.dot(p.astype(BF16), vv[:, sl], preferred_element_type=F32) / den
            lse_ref[i, bb, rows, :] = jnp.broadcast_to(m + jnp.log(den), (ATT_BLOCK, ATT_DH))


def _att_group(qkv, group):
    window, d = ATT_GROUPS[group]
    B, _, L, _ = qkv.shape
    S = L * d
    nL = L // ATT_BLOCK

    def cur(sec):
        return pl.BlockSpec((ATT_BATCH, 1, ATT_BLOCK, ATT_OUT_W), lambda b, n, r: (b, r, n, sec))

    def prev(sec):
        return pl.BlockSpec((ATT_BATCH, 1, ATT_BLOCK, ATT_OUT_W), lambda b, n, r: (b, r, jnp.maximum(n - 1, 0), sec))

    out_spec = pl.BlockSpec((ATT_HPG, ATT_BATCH, ATT_BLOCK * d, ATT_DH), lambda b, n, r: (0, b, n, 0))
    out_shape = jax.ShapeDtypeStruct((ATT_HPG, B, S, ATT_DH), F32)
    o, lse = pl.pallas_call(
        functools.partial(_att_kernel, group=group, dilation=d, span=window // d),
        grid=(B // ATT_BATCH, nL, d),
        in_specs=[cur(0), prev(1), cur(1), prev(2), cur(2)],
        out_specs=[out_spec, out_spec],
        out_shape=[out_shape, out_shape],
        compiler_params=_cparams(("parallel", "arbitrary", "arbitrary")),
        name=f"att_g{group}",
    )(qkv, qkv, qkv, qkv, qkv)
    return o.reshape(ATT_HPG, B * S, ATT_DH), lse.reshape(ATT_HPG, B * S, ATT_DH)


def _mix_kernel(retg_ref, o1_ref, o2_ref, o3_ref, l1_ref, l2_ref, l3_ref, gr_ref, ga_ref, x_ref,
                wro_ref, wao_ref, wo_ref, g2_ref, wq_ref, sk_ref, h_ref, xn_ref, sc_ref):
    heads = lambda ref: jnp.concatenate([ref[i] for i in range(ATT_HPG)], axis=1)
    l1, l2, l3 = heads(l1_ref), heads(l2_ref), heads(l3_ref)
    mx = jnp.maximum(jnp.maximum(l1, l2), l3)
    e1, e2, e3 = jnp.exp(l1 - mx), jnp.exp(l2 - mx), jnp.exp(l3 - mx)
    att = (e1 * heads(o1_ref) + e2 * heads(o2_ref) + e3 * heads(o3_ref)) / (e1 + e2 + e3)
    a_branch = jnp.dot(att.astype(BF16), wao_ref[...], preferred_element_type=F32)
    r_branch = jnp.dot(retg_ref[...], wro_ref[...], preferred_element_type=F32)
    merged = (jax.nn.sigmoid(gr_ref[...].astype(F32)) * r_branch
              + jax.nn.sigmoid(ga_ref[...].astype(F32)) * a_branch)
    h = x_ref[...] + jnp.dot(merged.astype(BF16), wo_ref[...], preferred_element_type=F32)
    h_ref[...] = h
    xn = h * lax.rsqrt(jnp.mean(h * h, axis=-1, keepdims=True) + EPS) * g2_ref[...]
    for c in range(D_MODEL // 128):
        xn_ref[pl.ds(c, h.shape[0], stride=D_MODEL // 128), :] = xn[:, c * 128:(c + 1) * 128]
    qp = jnp.dot(xn.astype(BF16), wq_ref[...], preferred_element_type=F32).astype(BF16)
    half = PEER_DKEY // 2
    for hp in range(2 * PEER_HEADS):
        sc_ref[hp] = lax.dot_general(sk_ref[hp], qp[:, hp * half:(hp + 1) * half],
                                     (((1,), (1,)), ((), ())), preferred_element_type=F32)


def _mix(retg, o1, o2, o3, l1, l2, l3, proj, x2, wro, wao, wo, g2, wq, sk):
    T = x2.shape[0]
    tm = MIX_TM
    row = lambda w: pl.BlockSpec((tm, w), lambda i: (i, 0))
    hd = pl.BlockSpec((ATT_HPG, tm, ATT_DH), lambda i: (0, i, 0))
    nhp = 2 * PEER_HEADS
    nt = D_MODEL // 128
    return pl.pallas_call(
        _mix_kernel,
        grid=(T // tm,),
        in_specs=[
            row(RET_V_W), hd, hd, hd, hd, hd, hd,
            pl.BlockSpec((tm, D_MODEL), lambda i: (i, OFF_GR // D_MODEL)),
            pl.BlockSpec((tm, D_MODEL), lambda i: (i, OFF_GA // D_MODEL)),
            row(D_MODEL),
            _const_spec(wro.shape), _const_spec(wao.shape), _const_spec(wo.shape),
            _const_spec(g2.shape), _const_spec(wq.shape), _const_spec(sk.shape),
        ],
        out_specs=[row(D_MODEL), pl.BlockSpec((tm * nt, 128), lambda i: (i, 0)),
                   pl.BlockSpec((nhp, PEER_NKEYS, tm), lambda i: (0, 0, i))],
        out_shape=[jax.ShapeDtypeStruct((T, D_MODEL), F32),
                   jax.ShapeDtypeStruct((T * nt, 128), F32),
                   jax.ShapeDtypeStruct((nhp, PEER_NKEYS, T), F32)],
        compiler_params=_cparams(("parallel",)),
        name="mix",
    )(retg, o1, o2, o3, l1, l2, l3, proj, proj, x2, wro, wao, wo, g2, wq, sk)


def _topk_rows(s, ids, k, fill):
    vals, idxs = [], []
    for _ in range(k):
        m = jnp.max(s, axis=0, keepdims=True)
        i = jnp.min(jnp.where(s == m, ids, fill), axis=0, keepdims=True)
        vals.append(m)
        idxs.append(i)
        s = jnp.where(ids == i, -jnp.inf, s)
    return jnp.concatenate(vals, axis=0), jnp.concatenate(idxs, axis=0)


_CAND_B = [PEER_TOPK // (a + 1) for a in range(PEER_TOPK)]
_CAND_ROWS = -(-sum(_CAND_B) // 8) * 8


def _topk_consts(W):
    K = PEER_TOPK
    key_ids = lax.broadcasted_iota(jnp.int32, (PEER_NKEYS, W), 0)
    npad = _CAND_ROWS - sum(_CAND_B)
    cand_ids = jnp.concatenate(
        [jnp.full((1, W), a * K + b, jnp.int32) for a in range(K) for b in range(_CAND_B[a])]
        + [jnp.full((npad, W), K * K, jnp.int32)], axis=0)
    pad = jnp.full((npad, W), -jnp.inf, F32)
    return key_ids, cand_ids, pad


def _topk_head(sc_ref, h, consts):
    K = PEER_TOPK
    key_ids, cand_ids, pad = consts
    v0, i0 = _topk_rows(sc_ref[2 * h], key_ids, K, PEER_NKEYS)
    v1, i1 = _topk_rows(sc_ref[2 * h + 1], key_ids, K, PEER_NKEYS)
    cand = jnp.concatenate([v0[a:a + 1] + v1[0:_CAND_B[a]] for a in range(K)] + [pad], axis=0)
    best_s, best_c = _topk_rows(cand, cand_ids, K, K * K)
    ca = best_c // K
    cb = best_c % K
    ia = jnp.zeros_like(best_c)
    ib = jnp.zeros_like(best_c)
    for a in range(K):
        ia = jnp.where(ca == a, i0[a:a + 1], ia)
        ib = jnp.where(cb == a, i1[a:a + 1], ib)
    ex = jnp.exp(best_s - best_s[0:1])
    return (ia * PEER_NKEYS + ib).astype(F32), ex / jnp.sum(ex, axis=0, keepdims=True)


def _topk_finish(e_scr, g_scr, off_ref, par_ref, gate_ref):
    W = e_scr.shape[-1]
    e = e_scr[...].reshape(PEER_PICKS, W).T.astype(jnp.int32)
    off_ref[...] = lax.shift_right_logical(e, 1) * 8
    par_ref[...] = (e & 1).astype(F32)
    gate_ref[...] = g_scr[...].reshape(PEER_PICKS, W).T


def _topk_kernel(sc_ref, off_ref, par_ref, gate_ref, e_scr, g_scr):
    consts = _topk_consts(sc_ref.shape[-1])

    def head(h, carry):
        e_scr[h], g_scr[h] = _topk_head(sc_ref, h, consts)
        return carry

    lax.fori_loop(0, PEER_HEADS, head, 0)
    _topk_finish(e_scr, g_scr, off_ref, par_ref, gate_ref)


def _topk(scT):
    T = scT.shape[-1]
    tb = min(TOPK_TB, T)
    out_spec = pl.BlockSpec((tb, PEER_PICKS), lambda i: (i, 0))
    return pl.pallas_call(
        _topk_kernel,
        grid=(T // tb,),
        in_specs=[pl.BlockSpec((2 * PEER_HEADS, PEER_NKEYS, tb), lambda i: (0, 0, i))],
        out_specs=[out_spec, out_spec, out_spec],
        out_shape=[jax.ShapeDtypeStruct((T, PEER_PICKS), jnp.int32),
                   jax.ShapeDtypeStruct((T, PEER_PICKS), F32),
                   jax.ShapeDtypeStruct((T, PEER_PICKS), F32)],
        scratch_shapes=[pltpu.VMEM((PEER_HEADS, PEER_TOPK, tb), F32),
                        pltpu.VMEM((PEER_HEADS, PEER_TOPK, tb), F32)],
        compiler_params=_cparams(("parallel",)),
        name="topk",
    )(scT)


PEER_U_VALU = 16
PEER_V_VALU = 32


PACK_TE = 2048


def _pack_kernel(x_ref, o_ref):
    s = pl.program_id(1)
    half = PACK_TE // 2
    bits = lambda v: pltpu.bitcast(v.astype(BF16).astype(F32), jnp.uint32)
    even = bits(x_ref[pl.ds(0, half, stride=2), :])
    odd = bits(x_ref[pl.ds(1, half, stride=2), :])
    o_ref[pl.ds(s, half, stride=D_MODEL // 128), :] = (even >> 16) | (odd & jnp.uint32(0xFFFF0000))


def _pack_table(tab):
    E = tab.shape[0]
    return pl.pallas_call(
        _pack_kernel,
        grid=(E // PACK_TE, D_MODEL // 128),
        in_specs=[pl.BlockSpec((PACK_TE, 128), lambda i, s: (i, s))],
        out_specs=pl.BlockSpec((4 * PACK_TE, 128), lambda i, s: (i, 0)),
        out_shape=jax.ShapeDtypeStruct((4 * E, 128), jnp.uint32),
        compiler_params=_cparams(("parallel", "arbitrary")),
        name="pack_table",
    )(tab)


def _split3(a):
    p0 = a.astype(BF16)
    r1 = a - p0.astype(F32)
    p1 = r1.astype(BF16)
    p2 = (r1 - p1.astype(F32)).astype(BF16)
    return p0, p1, p2


def _pair_expand(v, which, base, pairs, pieces=_split3):
    j = lax.broadcasted_iota(jnp.int32, (PEER_PICKS, 16 * pairs), 0)
    k = lax.broadcasted_iota(jnp.int32, (PEER_PICKS, 16 * pairs), 1)
    onehot = jnp.where(j == base + 2 * (k // 16) + which, 1.0, 0.0).astype(BF16)
    return sum(jnp.dot(p, onehot, preferred_element_type=F32) for p in pieces(v))


def _pair_targets(par, which, base, pairs):
    r = lax.broadcasted_iota(jnp.int32, (par.shape[0], 16 * pairs), 1) % 16
    one_piece = lambda p: (p.astype(BF16),)
    return 0.5 * (r.astype(F32) - _pair_expand(par, which, base, pairs, one_piece))


def _tile_words(tab_ref, off_ref, t, j):
    off = pl.multiple_of(off_ref[t, j], 8)
    return tab_ref[pl.ds(off, 8), :]


def _fill_lane_broadcasts(dst_ref, src, n):
    for j in range(n):
        dst_ref[j] = jnp.broadcast_to(src[:, j:j + 1], dst_ref.shape[1:])


def _expert_rows(words, shv):
    return pltpu.bitcast((words << shv) & jnp.uint32(0xFFFF0000), F32)


def _sublane_sums(ps):
    sub = lax.broadcasted_iota(jnp.int32, (8, 128), 0)
    lo4 = sub < 4
    t = [jnp.where(lo4, ps[j], ps[j + 4]) + pltpu.roll(jnp.where(lo4, ps[j + 4], ps[j]), 4, 0)
         for j in range(4)]
    m2 = (sub & 2) == 0
    u = [jnp.where(m2, t[j] + pltpu.roll(t[j], 6, 0), t[j + 2] + pltpu.roll(t[j + 2], 2, 0))
         for j in range(2)]
    m1 = (sub & 1) == 0
    return jnp.where(m1, u[0] + pltpu.roll(u[0], 7, 0), u[1] + pltpu.roll(u[1], 1, 0))


def _token_loop(tb, body, per_group=None):
    def group(i, carry):
        if per_group is not None:
            per_group(i)
        for s in range(PEER_UNROLL):
            body(i * PEER_UNROLL + s)
        return carry

    lax.fori_loop(0, tb // PEER_UNROLL, group, 0)


def _peer_u_kernel(off0_ref, par0_ref, gate0_ref, x_ref, tab_ref, sc_ref, w_ref, offo_ref, paro_ref,
                   off_ref, sem_ref, offv_ref, par_ref, gate_ref, e_scr, g_scr, tga_ref, tgb_ref, v3_ref, shb_ref,
                   actv_ref):
    tb = x_ref.shape[0]
    nv = PEER_U_VALU
    pairs = (PEER_PICKS - nv) // 2
    nblk = 16 * pairs // 128

    @pl.when(pl.program_id(0) == 0)
    def _():
        offv_ref[...] = off0_ref[...]
        par_ref[...] = par0_ref[...]
        gate_ref[...] = gate0_ref[...]

    def offsets_copy(g):
        rows = pl.ds(g * PEER_UNROLL, PEER_UNROLL)
        return pltpu.make_async_copy(offv_ref.at[rows], off_ref.at[rows], sem_ref.at[g])

    for g in range(tb // PEER_UNROLL):
        offsets_copy(g).start()
    offo_ref[...] = offv_ref[...]
    par = par_ref[...]
    paro_ref[...] = par
    tga_ref[...] = _pair_targets(par, 0, nv, pairs)
    tgb_ref[...] = _pair_targets(par, 1, nv, pairs)
    _fill_lane_broadcasts(shb_ref, ((1.0 - par) * 16.0).astype(jnp.uint32), nv)
    m_iota = lax.broadcasted_iota(jnp.int32, (8, 16 * pairs), 0).astype(F32)
    zero = jnp.zeros((8, 128), F32)
    ones = jnp.ones((8, 128), BF16)
    nt = (((1,), (1,)), ((), ()))

    def body(t):
        xt = x_ref[t]
        row = pl.ds(t, 1)
        rs = []
        for g in range(nv // 8):
            ps = []
            for jj in range(8):
                j = g * 8 + jj
                shv = jnp.broadcast_to(shb_ref[j, row, :], (8, 128))
                ps.append(_expert_rows(_tile_words(tab_ref, off_ref, t, j), shv) * xt)
            rs.append(_sublane_sums(ps))
        r = jnp.concatenate(rs, axis=0)
        hi = r.astype(BF16)
        lo = (r - hi.astype(F32)).astype(BF16)
        av = (lax.dot_general(ones, hi, nt, preferred_element_type=F32)
              + lax.dot_general(ones, lo, nt, preferred_element_type=F32))
        actv_ref[row, 0:nv] = av[0:1]
        tiles = [pltpu.bitcast(_tile_words(tab_ref, off_ref, t, j), BF16) for j in range(nv, PEER_PICKS)]
        wmat = jnp.concatenate(
            [jnp.concatenate([tiles[2 * q], tiles[2 * q + 1]], axis=1) for q in range(pairs)], axis=0)
        xf = jnp.concatenate([jnp.concatenate([xt, zero], axis=1),
                              jnp.concatenate([zero, xt], axis=1)], axis=0)
        xh = xf.astype(BF16)
        xl = (xf - xh.astype(F32)).astype(BF16)
        o = lax.dot_general(jnp.concatenate([xh, xl], axis=0), wmat, nt, preferred_element_type=F32)
        o = o[0:16] + o[16:32]
        za = jnp.where(tga_ref[row, :] == m_iota, o[0:8], 0.0)
        zb = jnp.where(tgb_ref[row, :] == m_iota, o[8:16], 0.0)
        for half, z in enumerate((za, zb)):
            v = _sublane_sums([z[:, i * 128:(i + 1) * 128] for i in range(nblk)] + [zero] * (8 - nblk))
            for i in range(nblk):
                v3_ref[half * nblk + i, row, :] = v[i:i + 1]

    consts = _topk_consts(tb)

    def next_block_head(h):
        offsets_copy(h).wait()
        e_scr[h], g_scr[h] = _topk_head(sc_ref, h, consts)

    _token_loop(tb, body, next_block_head)
    lane = lax.broadcasted_iota(jnp.int32, (128, PEER_PICKS), 0)
    pick = lax.broadcasted_iota(jnp.int32, (128, PEER_PICKS), 1)
    act = jnp.zeros((tb, PEER_PICKS), F32)
    for c in range(2 * nblk):
        sel = jnp.where(pick == nv + 2 * (8 * (c % nblk) + lane // 16) + c // nblk, 1.0, 0.0).astype(BF16)
        for p in _split3(v3_ref[c]):
            act = act + jnp.dot(p, sel, preferred_element_type=F32)
    act = jnp.where(pick[0:1] < nv, actv_ref[...], act)
    gelu = 0.5 * act * (1.0 + lax.erf(act * (2.0 ** -0.5)))
    w_ref[...] = gate_ref[...] * gelu
    _topk_finish(e_scr, g_scr, offv_ref, par_ref, gate_ref)


def _peer_specs(tb, tab):
    rows = pl.BlockSpec((tb, PEER_PICKS), lambda i: (i, 0))
    smem = pl.BlockSpec((tb, PEER_PICKS), lambda i: (i, 0), memory_space=pltpu.SMEM)
    table = pl.BlockSpec(tab.shape, lambda i: (0, 0), pipeline_mode=pl.Buffered(1))
    return rows, smem, table


def _peer_u(scT, x3, tab):
    T = x3.shape[0]
    tb = PEER_TB
    assert tb // PEER_UNROLL == PEER_HEADS
    nb = T // tb
    rows, _, table = _peer_specs(tb, tab)
    first = pl.BlockSpec((tb, PEER_PICKS), lambda i: (0, 0))
    pairs = (PEER_PICKS - PEER_U_VALU) // 2
    off0, par0, gate0 = _topk(scT[:, :, :tb])
    return pl.pallas_call(
        _peer_u_kernel,
        grid=(nb,),
        in_specs=[first, first, first, pl.BlockSpec((tb, 8, 128), lambda i: (i, 0, 0)), table,
                  pl.BlockSpec((2 * PEER_HEADS, PEER_NKEYS, tb), lambda i: (0, 0, jnp.minimum(i + 1, nb - 1)))],
        out_specs=[rows, rows, rows],
        out_shape=[jax.ShapeDtypeStruct((T, PEER_PICKS), F32),
                   jax.ShapeDtypeStruct((T, PEER_PICKS), jnp.int32),
                   jax.ShapeDtypeStruct((T, PEER_PICKS), F32)],
        scratch_shapes=[pltpu.SMEM((tb, PEER_PICKS), jnp.int32),
                        pltpu.SemaphoreType.DMA((tb // PEER_UNROLL,)),
                        pltpu.VMEM((tb, PEER_PICKS), jnp.int32),
                        pltpu.VMEM((tb, PEER_PICKS), F32), pltpu.VMEM((tb, PEER_PICKS), F32),
                        pltpu.VMEM((PEER_HEADS, PEER_TOPK, tb), F32), pltpu.VMEM((PEER_HEADS, PEER_TOPK, tb), F32),
                        pltpu.VMEM((tb, 16 * pairs), F32), pltpu.VMEM((tb, 16 * pairs), F32),
                        pltpu.VMEM((pairs // 4, tb, 128), F32),
                        pltpu.VMEM((PEER_U_VALU, tb, 128), jnp.uint32),
                        pltpu.VMEM((tb, PEER_PICKS), F32)],
        compiler_params=_cparams(("arbitrary",)),
        name="peer_u",
    )(off0, par0, gate0, x3, tab, scT)


def _peer_v_kernel(off_ref, par_ref, w_ref, tab_ref, o_ref, wa_ref, wb_ref, tga_ref, tgb_ref, shb_ref, wbc_ref):
    tb = o_ref.shape[0]
    nv = PEER_V_VALU
    pairs = (PEER_PICKS - nv) // 2
    par = par_ref[...]
    w = w_ref[...]
    wa_ref[...] = _pair_expand(w, 0, nv, pairs)
    wb_ref[...] = _pair_expand(w, 1, nv, pairs)
    tga_ref[...] = _pair_targets(par, 0, nv, pairs)
    tgb_ref[...] = _pair_targets(par, 1, nv, pairs)
    _fill_lane_broadcasts(shb_ref, ((1.0 - par) * 16.0).astype(jnp.uint32), nv)
    _fill_lane_broadcasts(wbc_ref, w, nv)
    m_iota = lax.broadcasted_iota(jnp.int32, (8, 16 * pairs), 0).astype(F32)
    nacc = 4

    def body(t):
        row = pl.ds(t, 1)
        acc = [jnp.zeros((8, 128), F32) for _ in range(nacc)]
        for j in range(nv):
            shv = jnp.broadcast_to(shb_ref[j, row, :], (8, 128))
            wv = jnp.broadcast_to(wbc_ref[j, row, :], (8, 128))
            acc[j % nacc] = acc[j % nacc] + wv * _expert_rows(_tile_words(tab_ref, off_ref, t, j), shv)
        tiles = [pltpu.bitcast(_tile_words(tab_ref, off_ref, t, j), BF16) for j in range(nv, PEER_PICKS)]
        vmat = jnp.concatenate([jnp.concatenate(tiles[0::2], axis=0),
                                jnp.concatenate(tiles[1::2], axis=0)], axis=1)
        la = jnp.where(tga_ref[row, :] == m_iota, wa_ref[row, :], 0.0)
        lb = jnp.where(tgb_ref[row, :] == m_iota, wb_ref[row, :], 0.0)
        lf = jnp.concatenate([la, lb], axis=0)
        lh = lf.astype(BF16)
        ll = (lf - lh.astype(F32)).astype(BF16)
        o = jnp.dot(jnp.concatenate([lh, ll], axis=0), vmat, preferred_element_type=F32)
        o_ref[t] = (((o[0:8, 0:128] + o[8:16, 128:256]) + (o[16:24, 0:128] + o[24:32, 128:256]))
                    + ((acc[0] + acc[1]) + (acc[2] + acc[3])))

    _token_loop(tb, body)


def _peer_v(off, par, w, tab):
    T = par.shape[0]
    tb = PEER_TB
    rows, smem, table = _peer_specs(tb, tab)
    pairs = (PEER_PICKS - PEER_V_VALU) // 2
    return pl.pallas_call(
        _peer_v_kernel,
        grid=(T // tb,),
        in_specs=[smem, rows, rows, table],
        out_specs=pl.BlockSpec((tb, 8, 128), lambda i: (i, 0, 0)),
        out_shape=jax.ShapeDtypeStruct((T, 8, 128), F32),
        scratch_shapes=[pltpu.VMEM((tb, 16 * pairs), F32) for _ in range(4)]
        + [pltpu.VMEM((PEER_V_VALU, tb, 128), jnp.uint32), pltpu.VMEM((PEER_V_VALU, tb, 128), F32)],
        compiler_params=_cparams(("parallel",)),
        name="peer_v",
    )(off, par, w, tab)


def _final_kernel(h_ref, p_ref, g_ref, o_ref):
    nt = D_MODEL // 128
    tm = h_ref.shape[0]
    p = jnp.concatenate([p_ref[pl.ds(c, tm, stride=nt), :] for c in range(nt)], axis=1)
    h = h_ref[...] + p
    o_ref[...] = h * lax.rsqrt(jnp.mean(h * h, axis=-1, keepdims=True) + EPS) * g_ref[...]


def _final(h, p, g):
    T = h.shape[0]
    row = pl.BlockSpec((FINAL_TM, D_MODEL), lambda i: (i, 0))
    return pl.pallas_call(
        _final_kernel,
        grid=(T // FINAL_TM,),
        in_specs=[row, pl.BlockSpec((FINAL_TM * (D_MODEL // 128), 128), lambda i: (i, 0)),
                  _const_spec((1, D_MODEL))],
        out_specs=row,
        out_shape=jax.ShapeDtypeStruct((T, D_MODEL), F32),
        compiler_params=_cparams(("parallel",)),
        name="final_norm",
    )(h, p, g)


def _layer(h2, B, S, norm1_g, w_in, w_ret_out, w_att_out, w_out, norm2_g, peer_wq, peer_subkeys, peer_u, peer_v):
    T = B * S
    aq, ak, av = 3072, 3072 + ATT_W, 3072 + 2 * ATT_W
    att_cols = [w_in[:, o + g * ATT_OUT_W:o + (g + 1) * ATT_OUT_W] for g in range(len(ATT_GROUPS)) for o in (aq, ak, av)]
    w_perm = jnp.concatenate([w_in[:, :3072], w_in[:, 7680:9728]] + att_cols, axis=1).astype(BF16)
    main, a0, a1, a2 = _proj(h2, norm1_g.reshape(1, D_MODEL), w_perm, B, S)
    retg = _retention(main.reshape(B, S, MAIN_W)).reshape(T, RET_V_W)
    (o1, l1), (o2, l2), (o3, l3) = (_att_group(a, g) for g, a in enumerate((a0.reshape(B, 1, S, ATT_SEC_W), a1, a2)))
    sk = peer_subkeys.reshape(2 * PEER_HEADS, PEER_NKEYS, PEER_DKEY // 2).astype(BF16)
    h_mid, xn2, scT = _mix(retg, o1, o2, o3, l1, l2, l3, main, h2,
                           w_ret_out.astype(BF16), w_att_out.astype(BF16), w_out.astype(BF16),
                           norm2_g.reshape(1, D_MODEL), peer_wq.astype(BF16), sk)
    w, off, par = _peer_u(scT, xn2.reshape(T, 8, 128), _pack_table(peer_u))
    pout = _peer_v(off, par, w, _pack_table(peer_v))
    return h_mid, pout.reshape(T * 8, 128)


def kernel(x, norm1_g, w_in, w_ret_out, w_att_out, w_out, norm2_g, peer_wq, peer_subkeys, peer_u, peer_v, normf_g):
    B, S, D = x.shape
    assert w_in.shape[0] == 1, "single-layer block"
    h, pout = _layer(x.reshape(B * S, D), B, S, norm1_g[0], w_in[0], w_ret_out[0], w_att_out[0], w_out[0],
                     norm2_g[0], peer_wq[0], peer_subkeys[0], peer_u[0], peer_v[0])
    return _final(h, pout, normf_g.reshape(1, D)).reshape(B, S, D)
```

```python
import functools
import math

import jax
import jax.numpy as jnp
from jax import lax
from jax.experimental import pallas as pl
from jax.experimental.pallas import tpu as pltpu

F32 = jnp.float32
BF16 = jnp.bfloat16

D_MODEL = 1024
RET_HEADS = 8
RET_DK = 64
RET_DV = 128
RET_CHUNK = 128
ATT_GROUPS = ((128, 1), (512, 4), (2048, 16))
ATT_HPG = 4
ATT_HEADS = 12
ATT_DH = 128
ATT_BLOCK = 128
ATT_BATCH = 2
PEER_HEADS = 8
PEER_NKEYS = 128
PEER_DKEY = 256
PEER_TOPK = 16
PEER_PICKS = PEER_HEADS * PEER_TOPK
EPS = 1e-6

RET_QK_W = RET_HEADS * RET_DK
RET_V_W = RET_HEADS * RET_DV
ATT_W = ATT_HEADS * ATT_DH
ATT_OUT_W = ATT_HPG * ATT_DH
OFF_RQ, OFF_RK, OFF_RV, OFF_RG = 0, 512, 1024, 2048
OFF_GR, OFF_GA = 3072, 4096
MAIN_W = 5120
ATT_SEC_W = 3 * ATT_OUT_W
IN_W = MAIN_W + len(ATT_GROUPS) * ATT_SEC_W

VMEM_LIMIT_BYTES = 60 * 1024 * 1024

PROJ_TM = 256
PROJ_TN = 512
MIX_TM = 256
TOPK_TB = 512
PEER_TB = 256
PEER_UNROLL = 32
FINAL_TM = 512


def _cparams(sem):
    return pltpu.CompilerParams(dimension_semantics=sem, vmem_limit_bytes=VMEM_LIMIT_BYTES)


def _const_spec(shape):
    nd = len(shape)
    return pl.BlockSpec(shape, lambda *_: (0,) * nd)


def _proj_kernel(x_ref, g_ref, w_ref, main_ref, a0_ref, a1_ref, a2_ref, scr_ref):
    x = x_ref[...]
    ms = jnp.mean(x * x, axis=-1, keepdims=True)
    xn = (x * lax.rsqrt(ms + EPS) * g_ref[...]).astype(BF16)

    def chunk(j):
        return jnp.dot(xn, w_ref[:, j * PROJ_TN:(j + 1) * PROJ_TN], preferred_element_type=F32)

    nmain = MAIN_W // PROJ_TN
    for j in range(nmain):
        main_ref[:, j * PROJ_TN:(j + 1) * PROJ_TN] = chunk(j).astype(BF16)
    for s in range(3):
        a0_ref[:, s * PROJ_TN:(s + 1) * PROJ_TN] = chunk(nmain + s).astype(BF16)
    for g, a_ref in ((1, a1_ref), (2, a2_ref)):
        d = ATT_GROUPS[g][1]
        for s in range(3):
            res = chunk(nmain + 3 * g + s)
            for c in range(PROJ_TN // 128):
                scr_ref[c] = res[:, c * 128:(c + 1) * 128]
            for r in range(d):
                piece = jnp.concatenate(
                    [scr_ref[c, pl.ds(r, PROJ_TM // d, stride=d), :] for c in range(PROJ_TN // 128)], axis=1)
                a_ref[0, r, :, s * PROJ_TN:(s + 1) * PROJ_TN] = piece.astype(BF16)


def _proj(x2, g, w, B, S):
    T = x2.shape[0]
    tiles = S // PROJ_TM
    d1, d2 = ATT_GROUPS[1][1], ATT_GROUPS[2][1]
    return pl.pallas_call(
        _proj_kernel,
        grid=(T // PROJ_TM,),
        in_specs=[
            pl.BlockSpec((PROJ_TM, D_MODEL), lambda i: (i, 0)),
            _const_spec((1, D_MODEL)),
            pl.BlockSpec((D_MODEL, IN_W), lambda i: (0, 0), pipeline_mode=pl.Buffered(1)),
        ],
        out_specs=[
            pl.BlockSpec((PROJ_TM, MAIN_W), lambda i: (i, 0)),
            pl.BlockSpec((PROJ_TM, ATT_SEC_W), lambda i: (i, 0)),
            pl.BlockSpec((1, d1, PROJ_TM // d1, ATT_SEC_W), lambda i: (i // tiles, 0, i % tiles, 0)),
            pl.BlockSpec((1, d2, PROJ_TM // d2, ATT_SEC_W), lambda i: (i // tiles, 0, i % tiles, 0)),
        ],
        out_shape=[
            jax.ShapeDtypeStruct((T, MAIN_W), BF16),
            jax.ShapeDtypeStruct((T, ATT_SEC_W), BF16),
            jax.ShapeDtypeStruct((B, d1, S // d1, ATT_SEC_W), BF16),
            jax.ShapeDtypeStruct((B, d2, S // d2, ATT_SEC_W), BF16),
        ],
        scratch_shapes=[pltpu.VMEM((PROJ_TN // 128, PROJ_TM, 128), F32)],
        compiler_params=_cparams(("parallel",)),
        name="proj",
    )(x2, g, w)


def _retention_kernel(q_ref, k_ref, v_ref, rg_ref, o_ref, state_ref, decay_ref):
    C = RET_CHUNK

    @pl.when(pl.program_id(1) == 0)
    def _():
        state_ref[...] = jnp.zeros_like(state_ref)
        pi = lax.broadcasted_iota(jnp.int32, (C, C), 0)
        pj = lax.broadcasted_iota(jnp.int32, (C, C), 1)
        diff = (pi - pj).astype(F32)
        for h in range(RET_HEADS):
            lg = math.log1p(-(2.0 ** (-5.0 - h)))
            decay_ref[h] = jnp.where(diff >= 0, jnp.exp(jnp.maximum(diff, 0.0) * lg), 0.0)

    pos = lax.broadcasted_iota(jnp.int32, (C, 1), 0).astype(F32)
    q = q_ref[0]
    k = k_ref[0]
    v = v_ref[0]
    rg = rg_ref[0]
    for h in range(RET_HEADS):
        lg = math.log1p(-(2.0 ** (-5.0 - h)))
        decay = decay_ref[h]
        w_k = jnp.exp((C - 1 - pos) * lg)
        w_q = jnp.exp((pos + 1.0) * lg)
        qh = q[:, h * RET_DK:(h + 1) * RET_DK]
        kh = k[:, h * RET_DK:(h + 1) * RET_DK].astype(F32) * (RET_DK ** -0.5)
        vh = v[:, h * RET_DV:(h + 1) * RET_DV]
        s = lax.dot_general(qh, kh.astype(BF16), (((1,), (1,)), ((), ())), preferred_element_type=F32)
        p = (s * decay).astype(BF16)
        inner = jnp.dot(p, vh, preferred_element_type=F32)
        st = state_ref[h]
        cross = jnp.dot((qh.astype(F32) * w_q).astype(BF16), st.astype(BF16), preferred_element_type=F32)
        kw = (kh * w_k).astype(BF16)
        kv = lax.dot_general(kw, vh, (((0,), (0,)), ((), ())), preferred_element_type=F32)
        state_ref[h] = math.exp(C * lg) * st + kv
        ret = inner + cross
        rn = ret * lax.rsqrt(jnp.mean(ret * ret, axis=-1, keepdims=True) + EPS)
        g = rg[:, h * RET_DV:(h + 1) * RET_DV].astype(F32)
        o_ref[0, :, h * RET_DV:(h + 1) * RET_DV] = (g * jax.nn.sigmoid(g) * rn).astype(BF16)


def _retention(proj3):
    B, S, _ = proj3.shape
    n = S // RET_CHUNK
    return pl.pallas_call(
        _retention_kernel,
        grid=(B, n),
        in_specs=[
            pl.BlockSpec((1, RET_CHUNK, RET_QK_W), lambda b, c: (b, c, OFF_RQ // RET_QK_W)),
            pl.BlockSpec((1, RET_CHUNK, RET_QK_W), lambda b, c: (b, c, OFF_RK // RET_QK_W)),
            pl.BlockSpec((1, RET_CHUNK, RET_V_W), lambda b, c: (b, c, OFF_RV // RET_V_W)),
            pl.BlockSpec((1, RET_CHUNK, RET_V_W), lambda b, c: (b, c, OFF_RG // RET_V_W)),
        ],
        out_specs=pl.BlockSpec((1, RET_CHUNK, RET_V_W), lambda b, c: (b, c, 0)),
        out_shape=jax.ShapeDtypeStruct((B, S, RET_V_W), BF16),
        scratch_shapes=[pltpu.VMEM((RET_HEADS, RET_DK, RET_DV), F32),
                        pltpu.VMEM((RET_HEADS, RET_CHUNK, RET_CHUNK), F32)],
        compiler_params=_cparams(("parallel", "arbitrary")),
        name="retention",
    )(proj3, proj3, proj3, proj3)


def _att_kernel(q_ref, kp_ref, kc_ref, vp_ref, vc_ref, o_ref, lse_ref, *, group, dilation, span):
    n = pl.program_id(1)
    r = pl.program_id(2)
    iq = lax.broadcasted_iota(jnp.int32, (ATT_BLOCK, 2 * ATT_BLOCK), 0)
    jk = lax.broadcasted_iota(jnp.int32, (ATT_BLOCK, 2 * ATT_BLOCK), 1)
    dist = iq + ATT_BLOCK - jk
    valid = (dist >= 0) & (dist <= span) & ((jk >= ATT_BLOCK) | (n > 0))
    distf = (dilation * dist).astype(F32)
    rows = pl.ds(r, ATT_BLOCK, stride=dilation) if dilation > 1 else slice(None)
    for bb in range(ATT_BATCH):
        q = q_ref[bb, 0]
        kk = jnp.concatenate([kp_ref[bb, 0], kc_ref[bb, 0]], axis=0)
        vv = jnp.concatenate([vp_ref[bb, 0], vc_ref[bb, 0]], axis=0)
        for i in range(ATT_HPG):
            slope = 2.0 ** (-8.0 * (group * ATT_HPG + i + 1) / ATT_HEADS)
            sl = slice(i * ATT_DH, (i + 1) * ATT_DH)
            s = lax.dot_general(q[:, sl], kk[:, sl], (((1,), (1,)), ((), ())), preferred_element_type=F32)
            s = s * (ATT_DH ** -0.5)
            s = jnp.where(valid, s - slope * distf, -jnp.inf)
            m = jnp.max(s, axis=-1, keepdims=True)
            p = jnp.exp(s - m)
            den = jnp.sum(p, axis=-1, keepdims=True)
            o_ref[i, bb, rows, :] = jnp.dot(p.astype(BF16), vv[:, sl], preferred_element_type=F32) / den
            lse_ref[i, bb, rows, :] = jnp.broadcast_to(m + jnp.log(den), (ATT_BLOCK, ATT_DH))


def _att_group(qkv, group):
    window, d = ATT_GROUPS[group]
    B, _, L, _ = qkv.shape
    S = L * d
    nL = L // ATT_BLOCK

    def cur(sec):
        return pl.BlockSpec((ATT_BATCH, 1, ATT_BLOCK, ATT_OUT_W), lambda b, n, r: (b, r, n, sec))

    def prev(sec):
        return pl.BlockSpec((ATT_BATCH, 1, ATT_BLOCK, ATT_OUT_W), lambda b, n, r: (b, r, jnp.maximum(n - 1, 0), sec))

    out_spec = pl.BlockSpec((ATT_HPG, ATT_BATCH, ATT_BLOCK * d, ATT_DH), lambda b, n, r: (0, b, n, 0))
    out_shape = jax.ShapeDtypeStruct((ATT_HPG, B, S, ATT_DH), F32)
    o, lse = pl.pallas_call(
        functools.partial(_att_kernel, group=group, dilation=d, span=window // d),
        grid=(B // ATT_BATCH, nL, d),
        in_specs=[cur(0), prev(1), cur(1), prev(2), cur(2)],
        out_specs=[out_spec, out_spec],
        out_shape=[out_shape, out_shape],
        compiler_params=_cparams(("parallel", "arbitrary", "arbitrary")),
        name=f"att_g{group}",
    )(qkv, qkv, qkv, qkv, qkv)
    return o.reshape(ATT_HPG, B * S, ATT_DH), lse.reshape(ATT_HPG, B * S, ATT_DH)


def _mix_kernel(retg_ref, o1_ref, o2_ref, o3_ref, l1_ref, l2_ref, l3_ref, gr_ref, ga_ref, x_ref,
                wro_ref, wao_ref, wo_ref, g2_ref, wq_ref, sk_ref, h_ref, xn_ref, sc_ref):
    heads = lambda ref: jnp.concatenate([ref[i] for i in range(ATT_HPG)], axis=1)
    l1, l2, l3 = heads(l1_ref), heads(l2_ref), heads(l3_ref)
    mx = jnp.maximum(jnp.maximum(l1, l2), l3)
    e1, e2, e3 = jnp.exp(l1 - mx), jnp.exp(l2 - mx), jnp.exp(l3 - mx)
    att = (e1 * heads(o1_ref) + e2 * heads(o2_ref) + e3 * heads(o3_ref)) / (e1 + e2 + e3)
    a_branch = jnp.dot(att.astype(BF16), wao_ref[...], preferred_element_type=F32)
    r_branch = jnp.dot(retg_ref[...], wro_ref[...], preferred_element_type=F32)
    merged = (jax.nn.sigmoid(gr_ref[...].astype(F32)) * r_branch
              + jax.nn.sigmoid(ga_ref[...].astype(F32)) * a_branch)
    h = x_ref[...] + jnp.dot(merged.astype(BF16), wo_ref[...], preferred_element_type=F32)
    h_ref[...] = h
    xn = h * lax.rsqrt(jnp.mean(h * h, axis=-1, keepdims=True) + EPS) * g2_ref[...]
    for c in range(D_MODEL // 128):
        xn_ref[pl.ds(c, h.shape[0], stride=D_MODEL // 128), :] = xn[:, c * 128:(c + 1) * 128]
    qp = jnp.dot(xn.astype(BF16), wq_ref[...], preferred_element_type=F32).astype(BF16)
    half = PEER_DKEY // 2
    for hp in range(2 * PEER_HEADS):
        sc_ref[hp] = lax.dot_general(sk_ref[hp], qp[:, hp * half:(hp + 1) * half],
                                     (((1,), (1,)), ((), ())), preferred_element_type=F32)


def _mix(retg, o1, o2, o3, l1, l2, l3, proj, x2, wro, wao, wo, g2, wq, sk):
    T = x2.shape[0]
    tm = MIX_TM
    row = lambda w: pl.BlockSpec((tm, w), lambda i: (i, 0))
    hd = pl.BlockSpec((ATT_HPG, tm, ATT_DH), lambda i: (0, i, 0))
    nhp = 2 * PEER_HEADS
    nt = D_MODEL // 128
    return pl.pallas_call(
        _mix_kernel,
        grid=(T // tm,),
        in_specs=[
            row(RET_V_W), hd, hd, hd, hd, hd, hd,
            pl.BlockSpec((tm, D_MODEL), lambda i: (i, OFF_GR // D_MODEL)),
            pl.BlockSpec((tm, D_MODEL), lambda i: (i, OFF_GA // D_MODEL)),
            row(D_MODEL),
            _const_spec(wro.shape), _const_spec(wao.shape), _const_spec(wo.shape),
            _const_spec(g2.shape), _const_spec(wq.shape), _const_spec(sk.shape),
        ],
        out_specs=[row(D_MODEL), pl.BlockSpec((tm * nt, 128), lambda i: (i, 0)),
                   pl.BlockSpec((nhp, PEER_NKEYS, tm), lambda i: (0, 0, i))],
        out_shape=[jax.ShapeDtypeStruct((T, D_MODEL), F32),
                   jax.ShapeDtypeStruct((T * nt, 128), F32),
                   jax.ShapeDtypeStruct((nhp, PEER_NKEYS, T), F32)],
        compiler_params=_cparams(("parallel",)),
        name="mix",
    )(retg, o1, o2, o3, l1, l2, l3, proj, proj, x2, wro, wao, wo, g2, wq, sk)


def _topk_rows(s, ids, k, fill):
    vals, idxs = [], []
    for _ in range(k):
        m = jnp.max(s, axis=0, keepdims=True)
        i = jnp.min(jnp.where(s == m, ids, fill), axis=0, keepdims=True)
        vals.append(m)
        idxs.append(i)
        s = jnp.where(ids == i, -jnp.inf, s)
    return jnp.concatenate(vals, axis=0), jnp.concatenate(idxs, axis=0)


_CAND_B = [PEER_TOPK // (a + 1) for a in range(PEER_TOPK)]
_CAND_ROWS = -(-sum(_CAND_B) // 8) * 8


def _topk_consts(W):
    K = PEER_TOPK
    key_ids = lax.broadcasted_iota(jnp.int32, (PEER_NKEYS, W), 0)
    npad = _CAND_ROWS - sum(_CAND_B)
    cand_ids = jnp.concatenate(
        [jnp.full((1, W), a * K + b, jnp.int32) for a in range(K) for b in range(_CAND_B[a])]
        + [jnp.full((npad, W), K * K, jnp.int32)], axis=0)
    pad = jnp.full((npad, W), -jnp.inf, F32)
    return key_ids, cand_ids, pad


def _topk_head(sc_ref, h, consts):
    K = PEER_TOPK
    key_ids, cand_ids, pad = consts
    v0, i0 = _topk_rows(sc_ref[2 * h], key_ids, K, PEER_NKEYS)
    v1, i1 = _topk_rows(sc_ref[2 * h + 1], key_ids, K, PEER_NKEYS)
    cand = jnp.concatenate([v0[a:a + 1] + v1[0:_CAND_B[a]] for a in range(K)] + [pad], axis=0)
    best_s, best_c = _topk_rows(cand, cand_ids, K, K * K)
    ca = best_c // K
    cb = best_c % K
    ia = jnp.zeros_like(best_c)
    ib = jnp.zeros_like(best_c)
    for a in range(K):
        ia = jnp.where(ca == a, i0[a:a + 1], ia)
        ib = jnp.where(cb == a, i1[a:a + 1], ib)
    ex = jnp.exp(best_s - best_s[0:1])
    return (ia * PEER_NKEYS + ib).astype(F32), ex / jnp.sum(ex, axis=0, keepdims=True)


def _topk_finish(e_scr, g_scr, off_ref, par_ref, gate_ref):
    W = e_scr.shape[-1]
    e = e_scr[...].reshape(PEER_PICKS, W).T.astype(jnp.int32)
    off_ref[...] = lax.shift_right_logical(e, 1) * 8
    par_ref[...] = (e & 1).astype(F32)
    gate_ref[...] = g_scr[...].reshape(PEER_PICKS, W).T


def _topk_kernel(sc_ref, off_ref, par_ref, gate_ref, e_scr, g_scr):
    consts = _topk_consts(sc_ref.shape[-1])

    def head(h, carry):
        e_scr[h], g_scr[h] = _topk_head(sc_ref, h, consts)
        return carry

    lax.fori_loop(0, PEER_HEADS, head, 0)
    _topk_finish(e_scr, g_scr, off_ref, par_ref, gate_ref)


def _topk(scT):
    T = scT.shape[-1]
    tb = min(TOPK_TB, T)
    out_spec = pl.BlockSpec((tb, PEER_PICKS), lambda i: (i, 0))
    return pl.pallas_call(
        _topk_kernel,
        grid=(T // tb,),
        in_specs=[pl.BlockSpec((2 * PEER_HEADS, PEER_NKEYS, tb), lambda i: (0, 0, i))],
        out_specs=[out_spec, out_spec, out_spec],
        out_shape=[jax.ShapeDtypeStruct((T, PEER_PICKS), jnp.int32),
                   jax.ShapeDtypeStruct((T, PEER_PICKS), F32),
                   jax.ShapeDtypeStruct((T, PEER_PICKS), F32)],
        scratch_shapes=[pltpu.VMEM((PEER_HEADS, PEER_TOPK, tb), F32),
                        pltpu.VMEM((PEER_HEADS, PEER_TOPK, tb), F32)],
        compiler_params=_cparams(("parallel",)),
        name="topk",
    )(scT)


PEER_U_VALU = 16
PEER_V_VALU = 16


PACK_TE = 2048


def _pack_kernel(x_ref, o_ref):
    s = pl.program_id(1)
    half = PACK_TE // 2
    bits = lambda v: pltpu.bitcast(v.astype(BF16).astype(F32), jnp.uint32)
    even = bits(x_ref[pl.ds(0, half, stride=2), :])
    odd = bits(x_ref[pl.ds(1, half, stride=2), :])
    o_ref[pl.ds(s, half, stride=D_MODEL // 128), :] = (even >> 16) | (odd & jnp.uint32(0xFFFF0000))


def _pack_table(tab):
    E = tab.shape[0]
    return pl.pallas_call(
        _pack_kernel,
        grid=(E // PACK_TE, D_MODEL // 128),
        in_specs=[pl.BlockSpec((PACK_TE, 128), lambda i, s: (i, s))],
        out_specs=pl.BlockSpec((4 * PACK_TE, 128), lambda i, s: (i, 0)),
        out_shape=jax.ShapeDtypeStruct((4 * E, 128), jnp.uint32),
        compiler_params=_cparams(("parallel", "arbitrary")),
        name="pack_table",
    )(tab)


def _split3(a):
    p0 = a.astype(BF16)
    r1 = a - p0.astype(F32)
    p1 = r1.astype(BF16)
    p2 = (r1 - p1.astype(F32)).astype(BF16)
    return p0, p1, p2


def _pair_expand(v, which, base, pairs, pieces=_split3):
    j = lax.broadcasted_iota(jnp.int32, (PEER_PICKS, 16 * pairs), 0)
    k = lax.broadcasted_iota(jnp.int32, (PEER_PICKS, 16 * pairs), 1)
    onehot = jnp.where(j == base + 2 * (k // 16) + which, 1.0, 0.0).astype(BF16)
    return sum(jnp.dot(p, onehot, preferred_element_type=F32) for p in pieces(v))


def _pair_targets(par, which, base, pairs):
    r = lax.broadcasted_iota(jnp.int32, (par.shape[0], 16 * pairs), 1) % 16
    one_piece = lambda p: (p.astype(BF16),)
    return 0.5 * (r.astype(F32) - _pair_expand(par, which, base, pairs, one_piece))


def _tile_words(tab_ref, off_ref, t, j):
    off = pl.multiple_of(off_ref[t, j], 8)
    return tab_ref[pl.ds(off, 8), :]


def _fill_lane_broadcasts(dst_ref, src, n):
    for j in range(n):
        dst_ref[j] = jnp.broadcast_to(src[:, j:j + 1], dst_ref.shape[1:])


def _expert_rows(words, shv):
    return pltpu.bitcast((words << shv) & jnp.uint32(0xFFFF0000), F32)


def _sublane_sums(ps):
    sub = lax.broadcasted_iota(jnp.int32, (8, 128), 0)
    lo4 = sub < 4
    t = [jnp.where(lo4, ps[j], ps[j + 4]) + pltpu.roll(jnp.where(lo4, ps[j + 4], ps[j]), 4, 0)
         for j in range(4)]
    m2 = (sub & 2) == 0
    u = [jnp.where(m2, t[j] + pltpu.roll(t[j], 6, 0), t[j + 2] + pltpu.roll(t[j + 2], 2, 0))
         for j in range(2)]
    m1 = (sub & 1) == 0
    return jnp.where(m1, u[0] + pltpu.roll(u[0], 7, 0), u[1] + pltpu.roll(u[1], 1, 0))


def _token_loop(tb, body, per_group=None):
    def group(i, carry):
        if per_group is not None:
            per_group(i)
        for s in range(PEER_UNROLL):
            body(i * PEER_UNROLL + s)
        return carry

    lax.fori_loop(0, tb // PEER_UNROLL, group, 0)


def _peer_u_kernel(off0_ref, par0_ref, gate0_ref, x_ref, tab_ref, sc_ref, w_ref, offo_ref, paro_ref,
                   off_ref, sem_ref, offv_ref, par_ref, gate_ref, e_scr, g_scr, tga_ref, tgb_ref, v3_ref, shb_ref,
                   actv_ref):
    tb = x_ref.shape[0]
    nv = PEER_U_VALU
    pairs = (PEER_PICKS - nv) // 2
    nblk = 16 * pairs // 128

    @pl.when(pl.program_id(0) == 0)
    def _():
        offv_ref[...] = off0_ref[...]
        par_ref[...] = par0_ref[...]
        gate_ref[...] = gate0_ref[...]

    def offsets_copy(g):
        rows = pl.ds(g * PEER_UNROLL, PEER_UNROLL)
        return pltpu.make_async_copy(offv_ref.at[rows], off_ref.at[rows], sem_ref.at[g])

    for g in range(tb // PEER_UNROLL):
        offsets_copy(g).start()
    offo_ref[...] = offv_ref[...]
    par = par_ref[...]
    paro_ref[...] = par
    tga_ref[...] = _pair_targets(par, 0, nv, pairs)
    tgb_ref[...] = _pair_targets(par, 1, nv, pairs)
    _fill_lane_broadcasts(shb_ref, ((1.0 - par) * 16.0).astype(jnp.uint32), nv)
    m_iota = lax.broadcasted_iota(jnp.int32, (8, 16 * pairs), 0).astype(F32)
    zero = jnp.zeros((8, 128), F32)
    ones = jnp.ones((8, 128), BF16)
    nt = (((1,), (1,)), ((), ()))

    def body(t):
        xt = x_ref[t]
        row = pl.ds(t, 1)
        rs = []
        for g in range(nv // 8):
            ps = []
            for jj in range(8):
                j = g * 8 + jj
                shv = jnp.broadcast_to(shb_ref[j, row, :], (8, 128))
                ps.append(_expert_rows(_tile_words(tab_ref, off_ref, t, j), shv) * xt)
            rs.append(_sublane_sums(ps))
        r = jnp.concatenate(rs, axis=0)
        hi = r.astype(BF16)
        lo = (r - hi.astype(F32)).astype(BF16)
        av = (lax.dot_general(ones, hi, nt, preferred_element_type=F32)
              + lax.dot_general(ones, lo, nt, preferred_element_type=F32))
        actv_ref[row, 0:nv] = av[0:1]
        tiles = [pltpu.bitcast(_tile_words(tab_ref, off_ref, t, j), BF16) for j in range(nv, PEER_PICKS)]
        wmat = jnp.concatenate(
            [jnp.concatenate([tiles[2 * q], tiles[2 * q + 1]], axis=1) for q in range(pairs)], axis=0)
        xf = jnp.concatenate([jnp.concatenate([xt, zero], axis=1),
                              jnp.concatenate([zero, xt], axis=1)], axis=0)
        xh = xf.astype(BF16)
        xl = (xf - xh.astype(F32)).astype(BF16)
        o = lax.dot_general(jnp.concatenate([xh, xl], axis=0), wmat, nt, preferred_element_type=F32)
        o = o[0:16] + o[16:32]
        za = jnp.where(tga_ref[row, :] == m_iota, o[0:8], 0.0)
        zb = jnp.where(tgb_ref[row, :] == m_iota, o[8:16], 0.0)
        for half, z in enumerate((za, zb)):
            v = _sublane_sums([z[:, i * 128:(i + 1) * 128] for i in range(nblk)] + [zero] * (8 - nblk))
            for i in range(nblk):
                v3_ref[half * nblk + i, row, :] = v[i:i + 1]

    consts = _topk_consts(128)

    def next_block_head(h):
        offsets_copy(h).wait()
        for c in range(tb // 128):
            lanes = slice(c * 128, (c + 1) * 128)
            e_scr[h, :, lanes], g_scr[h, :, lanes] = _topk_head(sc_ref.at[:, :, lanes], h, consts)

    _token_loop(tb, body, next_block_head)
    lane = lax.broadcasted_iota(jnp.int32, (128, PEER_PICKS), 0)
    pick = lax.broadcasted_iota(jnp.int32, (128, PEER_PICKS), 1)
    act = jnp.zeros((tb, PEER_PICKS), F32)
    for c in range(2 * nblk):
        sel = jnp.where(pick == nv + 2 * (8 * (c % nblk) + lane // 16) + c // nblk, 1.0, 0.0).astype(BF16)
        for p in _split3(v3_ref[c]):
            act = act + jnp.dot(p, sel, preferred_element_type=F32)
    act = jnp.where(pick[0:1] < nv, actv_ref[...], act)
    gelu = 0.5 * act * (1.0 + lax.erf(act * (2.0 ** -0.5)))
    w_ref[...] = gate_ref[...] * gelu
    _topk_finish(e_scr, g_scr, offv_ref, par_ref, gate_ref)


def _peer_specs(tb, tab):
    rows = pl.BlockSpec((tb, PEER_PICKS), lambda i: (i, 0))
    smem = pl.BlockSpec((tb, PEER_PICKS), lambda i: (i, 0), memory_space=pltpu.SMEM)
    table = pl.BlockSpec(tab.shape, lambda i: (0, 0), pipeline_mode=pl.Buffered(1))
    return rows, smem, table


def _peer_u(scT, x3, tab):
    T = x3.shape[0]
    tb = PEER_TB
    assert tb // PEER_UNROLL == PEER_HEADS
    nb = T // tb
    rows, _, table = _peer_specs(tb, tab)
    first = pl.BlockSpec((tb, PEER_PICKS), lambda i: (0, 0))
    pairs = (PEER_PICKS - PEER_U_VALU) // 2
    off0, par0, gate0 = _topk(scT[:, :, :tb])
    return pl.pallas_call(
        _peer_u_kernel,
        grid=(nb,),
        in_specs=[first, first, first, pl.BlockSpec((tb, 8, 128), lambda i: (i, 0, 0)), table,
                  pl.BlockSpec((2 * PEER_HEADS, PEER_NKEYS, tb), lambda i: (0, 0, jnp.minimum(i + 1, nb - 1)))],
        out_specs=[rows, rows, rows],
        out_shape=[jax.ShapeDtypeStruct((T, PEER_PICKS), F32),
                   jax.ShapeDtypeStruct((T, PEER_PICKS), jnp.int32),
                   jax.ShapeDtypeStruct((T, PEER_PICKS), F32)],
        scratch_shapes=[pltpu.SMEM((tb, PEER_PICKS), jnp.int32),
                        pltpu.SemaphoreType.DMA((tb // PEER_UNROLL,)),
                        pltpu.VMEM((tb, PEER_PICKS), jnp.int32),
                        pltpu.VMEM((tb, PEER_PICKS), F32), pltpu.VMEM((tb, PEER_PICKS), F32),
                        pltpu.VMEM((PEER_HEADS, PEER_TOPK, tb), F32), pltpu.VMEM((PEER_HEADS, PEER_TOPK, tb), F32),
                        pltpu.VMEM((tb, 16 * pairs), F32), pltpu.VMEM((tb, 16 * pairs), F32),
                        pltpu.VMEM((pairs // 4, tb, 128), F32),
                        pltpu.VMEM((PEER_U_VALU, tb, 128), jnp.uint32),
                        pltpu.VMEM((tb, PEER_PICKS), F32)],
        compiler_params=_cparams(("arbitrary",)),
        name="peer_u",
    )(off0, par0, gate0, x3, tab, scT)


def _peer_v_kernel(off_ref, par_ref, w_ref, tab_ref, o_ref, wa_ref, wb_ref, tga_ref, tgb_ref, shb_ref, wbc_ref):
    tb = o_ref.shape[0]
    nv = PEER_V_VALU
    pairs = (PEER_PICKS - nv) // 2
    par = par_ref[...]
    w = w_ref[...]
    wa_ref[...] = _pair_expand(w, 0, nv, pairs)
    wb_ref[...] = _pair_expand(w, 1, nv, pairs)
    tga_ref[...] = _pair_targets(par, 0, nv, pairs)
    tgb_ref[...] = _pair_targets(par, 1, nv, pairs)
    _fill_lane_broadcasts(shb_ref, ((1.0 - par) * 16.0).astype(jnp.uint32), nv)
    _fill_lane_broadcasts(wbc_ref, w, nv)
    m_iota = lax.broadcasted_iota(jnp.int32, (8, 16 * pairs), 0).astype(F32)
    nacc = 4

    def body(t):
        row = pl.ds(t, 1)
        acc = [jnp.zeros((8, 128), F32) for _ in range(nacc)]
        for j in range(nv):
            shv = jnp.broadcast_to(shb_ref[j, row, :], (8, 128))
            wv = jnp.broadcast_to(wbc_ref[j, row, :], (8, 128))
            acc[j % nacc] = acc[j % nacc] + wv * _expert_rows(_tile_words(tab_ref, off_ref, t, j), shv)
        tiles = [pltpu.bitcast(_tile_words(tab_ref, off_ref, t, j), BF16) for j in range(nv, PEER_PICKS)]
        vmat = jnp.concatenate([jnp.concatenate(tiles[0::2], axis=0),
                                jnp.concatenate(tiles[1::2], axis=0)], axis=1)
        la = jnp.where(tga_ref[row, :] == m_iota, wa_ref[row, :], 0.0)
        lb = jnp.where(tgb_ref[row, :] == m_iota, wb_ref[row, :], 0.0)
        lf = jnp.concatenate([la, lb], axis=0)
        lh = lf.astype(BF16)
        ll = (lf - lh.astype(F32)).astype(BF16)
        o = jnp.dot(jnp.concatenate([lh, ll], axis=0), vmat, preferred_element_type=F32)
        o_ref[t] = (((o[0:8, 0:128] + o[8:16, 128:256]) + (o[16:24, 0:128] + o[24:32, 128:256]))
                    + ((acc[0] + acc[1]) + (acc[2] + acc[3])))

    _token_loop(tb, body)


def _peer_v(off, par, w, tab):
    T = par.shape[0]
    tb = PEER_TB
    rows, smem, table = _peer_specs(tb, tab)
    pairs = (PEER_PICKS - PEER_V_VALU) // 2
    return pl.pallas_call(
        _peer_v_kernel,
        grid=(T // tb,),
        in_specs=[smem, rows, rows, table],
        out_specs=pl.BlockSpec((tb, 8, 128), lambda i: (i, 0, 0)),
        out_shape=jax.ShapeDtypeStruct((T, 8, 128), F32),
        scratch_shapes=[pltpu.VMEM((tb, 16 * pairs), F32) for _ in range(4)]
        + [pltpu.VMEM((PEER_V_VALU, tb, 128), jnp.uint32), pltpu.VMEM((PEER_V_VALU, tb, 128), F32)],
        compiler_params=_cparams(("parallel",)),
        name="peer_v",
    )(off, par, w, tab)


def _final_kernel(h_ref, p_ref, g_ref, o_ref):
    nt = D_MODEL // 128
    tm = h_ref.shape[0]
    p = jnp.concatenate([p_ref[pl.ds(c, tm, stride=nt), :] for c in range(nt)], axis=1)
    h = h_ref[...] + p
    o_ref[...] = h * lax.rsqrt(jnp.mean(h * h, axis=-1, keepdims=True) + EPS) * g_ref[...]


def _final(h, p, g):
    T = h.shape[0]
    row = pl.BlockSpec((FINAL_TM, D_MODEL), lambda i: (i, 0))
    return pl.pallas_call(
        _final_kernel,
        grid=(T // FINAL_TM,),
        in_specs=[row, pl.BlockSpec((FINAL_TM * (D_MODEL // 128), 128), lambda i: (i, 0)),
                  _const_spec((1, D_MODEL))],
        out_specs=row,
        out_shape=jax.ShapeDtypeStruct((T, D_MODEL), F32),
        compiler_params=_cparams(("parallel",)),
        name="final_norm",
    )(h, p, g)


def _layer(h2, B, S, norm1_g, w_in, w_ret_out, w_att_out, w_out, norm2_g, peer_wq, peer_subkeys, peer_u, peer_v):
    T = B * S
    aq, ak, av = 3072, 3072 + ATT_W, 3072 + 2 * ATT_W
    att_cols = [w_in[:, o + g * ATT_OUT_W:o + (g + 1) * ATT_OUT_W] for g in range(len(ATT_GROUPS)) for o in (aq, ak, av)]
    w_perm = jnp.concatenate([w_in[:, :3072], w_in[:, 7680:9728]] + att_cols, axis=1).astype(BF16)
    main, a0, a1, a2 = _proj(h2, norm1_g.reshape(1, D_MODEL), w_perm, B, S)
    retg = _retention(main.reshape(B, S, MAIN_W)).reshape(T, RET_V_W)
    (o1, l1), (o2, l2), (o3, l3) = (_att_group(a, g) for g, a in enumerate((a0.reshape(B, 1, S, ATT_SEC_W), a1, a2)))
    sk = peer_subkeys.reshape(2 * PEER_HEADS, PEER_NKEYS, PEER_DKEY // 2).astype(BF16)
    h_mid, xn2, scT = _mix(retg, o1, o2, o3, l1, l2, l3, main, h2,
                           w_ret_out.astype(BF16), w_att_out.astype(BF16), w_out.astype(BF16),
                           norm2_g.reshape(1, D_MODEL), peer_wq.astype(BF16), sk)
    w, off, par = _peer_u(scT, xn2.reshape(T, 8, 128), _pack_table(peer_u))
    pout = _peer_v(off, par, w, _pack_table(peer_v))
    return h_mid, pout.reshape(T * 8, 128)


def kernel(x, norm1_g, w_in, w_ret_out, w_att_out, w_out, norm2_g, peer_wq, peer_subkeys, peer_u, peer_v, normf_g):
    B, S, D = x.shape
    assert w_in.shape[0] == 1, "single-layer block"
    h, pout = _layer(x.reshape(B * S, D), B, S, norm1_g[0], w_in[0], w_ret_out[0], w_att_out[0], w_out[0],
                     norm2_g[0], peer_wq[0], peer_subkeys[0], peer_u[0], peer_v[0])
    return _final(h, pout, normf_g.reshape(1, D)).reshape(B, S, D)
```

```python
import functools
import math

import jax
import jax.numpy as jnp
from jax import lax
from jax.experimental import pallas as pl
from jax.experimental.pallas import tpu as pltpu

F32 = jnp.float32
BF16 = jnp.bfloat16

D_MODEL = 1024
RET_HEADS = 8
RET_DK = 64
RET_DV = 128
RET_CHUNK = 128
ATT_GROUPS = ((128, 1), (512, 4), (2048, 16))
ATT_HPG = 4
ATT_HEADS = 12
ATT_DH = 128
ATT_BLOCK = 128
ATT_BATCH = (4, 4, 2)
PEER_HEADS = 8
PEER_NKEYS = 128
PEER_DKEY = 256
PEER_TOPK = 16
PEER_PICKS = PEER_HEADS * PEER_TOPK
EPS = 1e-6

RET_QK_W = RET_HEADS * RET_DK
RET_V_W = RET_HEADS * RET_DV
ATT_W = ATT_HEADS * ATT_DH
ATT_OUT_W = ATT_HPG * ATT_DH
OFF_RQ, OFF_RK, OFF_RV, OFF_RG = 0, 512, 1024, 2048
OFF_GR, OFF_GA = 3072, 4096
MAIN_W = 5120
ATT_SEC_W = 3 * ATT_OUT_W
IN_W = MAIN_W + len(ATT_GROUPS) * ATT_SEC_W

VMEM_LIMIT_BYTES = 60 * 1024 * 1024

PROJ_TM = 256
PROJ_TN = 512
MIX_TM = 256
TOPK_TB = 512
PEER_TB = 256
PEER_UNROLL = 32
FINAL_TM = 512


def _cparams(sem):
    return pltpu.CompilerParams(dimension_semantics=sem, vmem_limit_bytes=VMEM_LIMIT_BYTES)


def _const_spec(shape):
    nd = len(shape)
    return pl.BlockSpec(shape, lambda *_: (0,) * nd)


def _proj_kernel(x_ref, g_ref, w_ref, main_ref, a0_ref, a1_ref, a2_ref, scr_ref):
    x = x_ref[...]
    ms = jnp.mean(x * x, axis=-1, keepdims=True)
    xn = (x * lax.rsqrt(ms + EPS) * g_ref[...]).astype(BF16)

    def chunk(j):
        return jnp.dot(xn, w_ref[:, j * PROJ_TN:(j + 1) * PROJ_TN], preferred_element_type=F32)

    nmain = MAIN_W // PROJ_TN
    for j in range(nmain):
        main_ref[:, j * PROJ_TN:(j + 1) * PROJ_TN] = chunk(j).astype(BF16)
    for s in range(3):
        a0_ref[:, s * PROJ_TN:(s + 1) * PROJ_TN] = chunk(nmain + s).astype(BF16)
    for g, a_ref in ((1, a1_ref), (2, a2_ref)):
        d = ATT_GROUPS[g][1]
        for s in range(3):
            res = chunk(nmain + 3 * g + s)
            for c in range(PROJ_TN // 128):
                scr_ref[c] = res[:, c * 128:(c + 1) * 128]
            for r in range(d):
                piece = jnp.concatenate(
                    [scr_ref[c, pl.ds(r, PROJ_TM // d, stride=d), :] for c in range(PROJ_TN // 128)], axis=1)
                a_ref[0, r, :, s * PROJ_TN:(s + 1) * PROJ_TN] = piece.astype(BF16)


def _proj(x2, g, w, B, S):
    T = x2.shape[0]
    tiles = S // PROJ_TM
    d1, d2 = ATT_GROUPS[1][1], ATT_GROUPS[2][1]
    return pl.pallas_call(
        _proj_kernel,
        grid=(T // PROJ_TM,),
        in_specs=[
            pl.BlockSpec((PROJ_TM, D_MODEL), lambda i: (i, 0)),
            _const_spec((1, D_MODEL)),
            pl.BlockSpec((D_MODEL, IN_W), lambda i: (0, 0), pipeline_mode=pl.Buffered(1)),
        ],
        out_specs=[
            pl.BlockSpec((PROJ_TM, MAIN_W), lambda i: (i, 0)),
            pl.BlockSpec((PROJ_TM, ATT_SEC_W), lambda i: (i, 0)),
            pl.BlockSpec((1, d1, PROJ_TM // d1, ATT_SEC_W), lambda i: (i // tiles, 0, i % tiles, 0)),
            pl.BlockSpec((1, d2, PROJ_TM // d2, ATT_SEC_W), lambda i: (i // tiles, 0, i % tiles, 0)),
        ],
        out_shape=[
            jax.ShapeDtypeStruct((T, MAIN_W), BF16),
            jax.ShapeDtypeStruct((T, ATT_SEC_W), BF16),
            jax.ShapeDtypeStruct((B, d1, S // d1, ATT_SEC_W), BF16),
            jax.ShapeDtypeStruct((B, d2, S // d2, ATT_SEC_W), BF16),
        ],
        scratch_shapes=[pltpu.VMEM((PROJ_TN // 128, PROJ_TM, 128), F32)],
        compiler_params=_cparams(("parallel",)),
        name="proj",
    )(x2, g, w)


def _retention_kernel(q_ref, k_ref, v_ref, rg_ref, o_ref, state_ref, decay_ref):
    C = RET_CHUNK

    @pl.when(pl.program_id(1) == 0)
    def _():
        state_ref[...] = jnp.zeros_like(state_ref)
        pi = lax.broadcasted_iota(jnp.int32, (C, C), 0)
        pj = lax.broadcasted_iota(jnp.int32, (C, C), 1)
        diff = (pi - pj).astype(F32)
        for h in range(RET_HEADS):
            lg = math.log1p(-(2.0 ** (-5.0 - h)))
            decay_ref[h] = jnp.where(diff >= 0, jnp.exp(jnp.maximum(diff, 0.0) * lg), 0.0)

    pos = lax.broadcasted_iota(jnp.int32, (C, 1), 0).astype(F32)
    q = q_ref[0]
    k = k_ref[0]
    v = v_ref[0]
    rg = rg_ref[0]
    for h in range(RET_HEADS):
        lg = math.log1p(-(2.0 ** (-5.0 - h)))
        decay = decay_ref[h]
        w_k = jnp.exp((C - 1 - pos) * lg)
        w_q = jnp.exp((pos + 1.0) * lg)
        qh = q[:, h * RET_DK:(h + 1) * RET_DK]
        kh = k[:, h * RET_DK:(h + 1) * RET_DK].astype(F32) * (RET_DK ** -0.5)
        vh = v[:, h * RET_DV:(h + 1) * RET_DV]
        s = lax.dot_general(qh, kh.astype(BF16), (((1,), (1,)), ((), ())), preferred_element_type=F32)
        p = (s * decay).astype(BF16)
        inner = jnp.dot(p, vh, preferred_element_type=F32)
        st = state_ref[h]
        cross = jnp.dot((qh.astype(F32) * w_q).astype(BF16), st.astype(BF16), preferred_element_type=F32)
        kw = (kh * w_k).astype(BF16)
        kv = lax.dot_general(kw, vh, (((0,), (0,)), ((), ())), preferred_element_type=F32)
        state_ref[h] = math.exp(C * lg) * st + kv
        ret = inner + cross
        rn = ret * lax.rsqrt(jnp.mean(ret * ret, axis=-1, keepdims=True) + EPS)
        g = rg[:, h * RET_DV:(h + 1) * RET_DV].astype(F32)
        o_ref[0, :, h * RET_DV:(h + 1) * RET_DV] = (g * jax.nn.sigmoid(g) * rn).astype(BF16)


def _retention(proj3):
    B, S, _ = proj3.shape
    n = S // RET_CHUNK
    return pl.pallas_call(
        _retention_kernel,
        grid=(B, n),
        in_specs=[
            pl.BlockSpec((1, RET_CHUNK, RET_QK_W), lambda b, c: (b, c, OFF_RQ // RET_QK_W)),
            pl.BlockSpec((1, RET_CHUNK, RET_QK_W), lambda b, c: (b, c, OFF_RK // RET_QK_W)),
            pl.BlockSpec((1, RET_CHUNK, RET_V_W), lambda b, c: (b, c, OFF_RV // RET_V_W)),
            pl.BlockSpec((1, RET_CHUNK, RET_V_W), lambda b, c: (b, c, OFF_RG // RET_V_W)),
        ],
        out_specs=pl.BlockSpec((1, RET_CHUNK, RET_V_W), lambda b, c: (b, c, 0)),
        out_shape=jax.ShapeDtypeStruct((B, S, RET_V_W), BF16),
        scratch_shapes=[pltpu.VMEM((RET_HEADS, RET_DK, RET_DV), F32),
                        pltpu.VMEM((RET_HEADS, RET_CHUNK, RET_CHUNK), F32)],
        compiler_params=_cparams(("parallel", "arbitrary")),
        name="retention",
    )(proj3, proj3, proj3, proj3)


def _att_kernel(q_ref, kp_ref, kc_ref, vp_ref, vc_ref, o_ref, lse_ref, *, group, dilation, span):
    n = pl.program_id(1)
    r = pl.program_id(2)
    iq = lax.broadcasted_iota(jnp.int32, (ATT_BLOCK, 2 * ATT_BLOCK), 0)
    jk = lax.broadcasted_iota(jnp.int32, (ATT_BLOCK, 2 * ATT_BLOCK), 1)
    dist = iq + ATT_BLOCK - jk
    valid = (dist >= 0) & (dist <= span) & ((jk >= ATT_BLOCK) | (n > 0))
    distf = (dilation * dist).astype(F32)
    rows = pl.ds(r, ATT_BLOCK, stride=dilation) if dilation > 1 else slice(None)
    for bb in range(q_ref.shape[0]):
        q = q_ref[bb, 0]
        kk = jnp.concatenate([kp_ref[bb, 0], kc_ref[bb, 0]], axis=0)
        vv = jnp.concatenate([vp_ref[bb, 0], vc_ref[bb, 0]], axis=0)
        for i in range(ATT_HPG):
            slope = 2.0 ** (-8.0 * (group * ATT_HPG + i + 1) / ATT_HEADS)
            sl = slice(i * ATT_DH, (i + 1) * ATT_DH)
            s = lax.dot_general(q[:, sl], kk[:, sl], (((1,), (1,)), ((), ())), preferred_element_type=F32)
            s = s * (ATT_DH ** -0.5)
            s = jnp.where(valid, s - slope * distf, -jnp.inf)
            m = jnp.max(s, axis=-1, keepdims=True)
            p = jnp.exp(s - m)
            den = jnp.sum(p, axis=-1, keepdims=True)
            o_ref[i, bb, rows, :] = jnp.dot(p.astype(BF16), vv[:, sl], preferred_element_type=F32) / den
            lse_ref[i, bb, rows, :] = jnp.broadcast_to(m + jnp.log(den), (ATT_BLOCK, ATT_DH))


def _att_group(qkv, group):
    window, d = ATT_GROUPS[group]
    B, _, L, _ = qkv.shape
    S = L * d
    nL = L // ATT_BLOCK

    nbat = ATT_BATCH[group]

    def cur(sec):
        return pl.BlockSpec((nbat, 1, ATT_BLOCK, ATT_OUT_W), lambda b, n, r: (b, r, n, sec))

    def prev(sec):
        return pl.BlockSpec((nbat, 1, ATT_BLOCK, ATT_OUT_W), lambda b, n, r: (b, r, jnp.maximum(n - 1, 0), sec))

    out_spec = pl.BlockSpec((ATT_HPG, nbat, ATT_BLOCK * d, ATT_DH), lambda b, n, r: (0, b, n, 0))
    out_shape = jax.ShapeDtypeStruct((ATT_HPG, B, S, ATT_DH), F32)
    o, lse = pl.pallas_call(
        functools.partial(_att_kernel, group=group, dilation=d, span=window // d),
        grid=(B // nbat, nL, d),
        in_specs=[cur(0), prev(1), cur(1), prev(2), cur(2)],
        out_specs=[out_spec, out_spec],
        out_shape=[out_shape, out_shape],
        compiler_params=_cparams(("parallel", "arbitrary", "arbitrary")),
        name=f"att_g{group}",
    )(qkv, qkv, qkv, qkv, qkv)
    return o.reshape(ATT_HPG, B * S, ATT_DH), lse.reshape(ATT_HPG, B * S, ATT_DH)


def _mix_kernel(retg_ref, o1_ref, o2_ref, o3_ref, l1_ref, l2_ref, l3_ref, gr_ref, ga_ref, x_ref,
                wro_ref, wao_ref, wo_ref, g2_ref, wq_ref, sk_ref, h_ref, xn_ref, sc_ref):
    heads = lambda ref: jnp.concatenate([ref[i] for i in range(ATT_HPG)], axis=1)
    l1, l2, l3 = heads(l1_ref), heads(l2_ref), heads(l3_ref)
    mx = jnp.maximum(jnp.maximum(l1, l2), l3)
    e1, e2, e3 = jnp.exp(l1 - mx), jnp.exp(l2 - mx), jnp.exp(l3 - mx)
    att = (e1 * heads(o1_ref) + e2 * heads(o2_ref) + e3 * heads(o3_ref)) / (e1 + e2 + e3)
    a_branch = jnp.dot(att.astype(BF16), wao_ref[...], preferred_element_type=F32)
    r_branch = jnp.dot(retg_ref[...], wro_ref[...], preferred_element_type=F32)
    merged = (jax.nn.sigmoid(gr_ref[...].astype(F32)) * r_branch
              + jax.nn.sigmoid(ga_ref[...].astype(F32)) * a_branch)
    h = x_ref[...] + jnp.dot(merged.astype(BF16), wo_ref[...], preferred_element_type=F32)
    h_ref[...] = h
    xn = h * lax.rsqrt(jnp.mean(h * h, axis=-1, keepdims=True) + EPS) * g2_ref[...]
    for c in range(D_MODEL // 128):
        xn_ref[pl.ds(c, h.shape[0], stride=D_MODEL // 128), :] = xn[:, c * 128:(c + 1) * 128]
    qp = jnp.dot(xn.astype(BF16), wq_ref[...], preferred_element_type=F32).astype(BF16)
    half = PEER_DKEY // 2
    for hp in range(2 * PEER_HEADS):
        sc_ref[hp] = lax.dot_general(sk_ref[hp], qp[:, hp * half:(hp + 1) * half],
                                     (((1,), (1,)), ((), ())), preferred_element_type=F32)


def _mix(retg, o1, o2, o3, l1, l2, l3, proj, x2, wro, wao, wo, g2, wq, sk):
    T = x2.shape[0]
    tm = MIX_TM
    row = lambda w: pl.BlockSpec((tm, w), lambda i: (i, 0))
    hd = pl.BlockSpec((ATT_HPG, tm, ATT_DH), lambda i: (0, i, 0))
    nhp = 2 * PEER_HEADS
    nt = D_MODEL // 128
    return pl.pallas_call(
        _mix_kernel,
        grid=(T // tm,),
        in_specs=[
            row(RET_V_W), hd, hd, hd, hd, hd, hd,
            pl.BlockSpec((tm, D_MODEL), lambda i: (i, OFF_GR // D_MODEL)),
            pl.BlockSpec((tm, D_MODEL), lambda i: (i, OFF_GA // D_MODEL)),
            row(D_MODEL),
            _const_spec(wro.shape), _const_spec(wao.shape), _const_spec(wo.shape),
            _const_spec(g2.shape), _const_spec(wq.shape), _const_spec(sk.shape),
        ],
        out_specs=[row(D_MODEL), pl.BlockSpec((tm * nt, 128), lambda i: (i, 0)),
                   pl.BlockSpec((nhp, PEER_NKEYS, tm), lambda i: (0, 0, i))],
        out_shape=[jax.ShapeDtypeStruct((T, D_MODEL), F32),
                   jax.ShapeDtypeStruct((T * nt, 128), F32),
                   jax.ShapeDtypeStruct((nhp, PEER_NKEYS, T), F32)],
        compiler_params=_cparams(("parallel",)),
        name="mix",
    )(retg, o1, o2, o3, l1, l2, l3, proj, proj, x2, wro, wao, wo, g2, wq, sk)


def _topk_rows(s, ids, k, fill):
    vals, idxs = [], []
    for _ in range(k):
        m = jnp.max(s, axis=0, keepdims=True)
        i = jnp.min(jnp.where(s == m, ids, fill), axis=0, keepdims=True)
        vals.append(m)
        idxs.append(i)
        s = jnp.where(ids == i, -jnp.inf, s)
    return jnp.concatenate(vals, axis=0), jnp.concatenate(idxs, axis=0)


_CAND_B = [PEER_TOPK // (a + 1) for a in range(PEER_TOPK)]
_CAND_ROWS = -(-sum(_CAND_B) // 8) * 8


def _topk_consts(W):
    K = PEER_TOPK
    key_ids = lax.broadcasted_iota(jnp.int32, (PEER_NKEYS, W), 0)
    npad = _CAND_ROWS - sum(_CAND_B)
    cand_ids = jnp.concatenate(
        [jnp.full((1, W), a * K + b, jnp.int32) for a in range(K) for b in range(_CAND_B[a])]
        + [jnp.full((npad, W), K * K, jnp.int32)], axis=0)
    pad = jnp.full((npad, W), -jnp.inf, F32)
    return key_ids, cand_ids, pad


def _topk_head(sc_ref, h, consts):
    K = PEER_TOPK
    key_ids, cand_ids, pad = consts
    v0, i0 = _topk_rows(sc_ref[2 * h], key_ids, K, PEER_NKEYS)
    v1, i1 = _topk_rows(sc_ref[2 * h + 1], key_ids, K, PEER_NKEYS)
    cand = jnp.concatenate([v0[a:a + 1] + v1[0:_CAND_B[a]] for a in range(K)] + [pad], axis=0)
    best_s, best_c = _topk_rows(cand, cand_ids, K, K * K)
    ca = best_c // K
    cb = best_c % K
    ia = jnp.zeros_like(best_c)
    ib = jnp.zeros_like(best_c)
    for a in range(K):
        ia = jnp.where(ca == a, i0[a:a + 1], ia)
        ib = jnp.where(cb == a, i1[a:a + 1], ib)
    ex = jnp.exp(best_s - best_s[0:1])
    return (ia * PEER_NKEYS + ib).astype(F32), ex / jnp.sum(ex, axis=0, keepdims=True)


def _topk_finish(e_scr, g_scr, off_ref, par_ref, gate_ref):
    W = e_scr.shape[-1]
    e = e_scr[...].reshape(PEER_PICKS, W).T.astype(jnp.int32)
    off_ref[...] = lax.shift_right_logical(e, 1) * 8
    par_ref[...] = (e & 1).astype(F32)
    gate_ref[...] = g_scr[...].reshape(PEER_PICKS, W).T


def _topk_kernel(sc_ref, off_ref, par_ref, gate_ref, e_scr, g_scr):
    consts = _topk_consts(sc_ref.shape[-1])

    def head(h, carry):
        e_scr[h], g_scr[h] = _topk_head(sc_ref, h, consts)
        return carry

    lax.fori_loop(0, PEER_HEADS, head, 0)
    _topk_finish(e_scr, g_scr, off_ref, par_ref, gate_ref)


def _topk(scT):
    T = scT.shape[-1]
    tb = min(TOPK_TB, T)
    out_spec = pl.BlockSpec((tb, PEER_PICKS), lambda i: (i, 0))
    return pl.pallas_call(
        _topk_kernel,
        grid=(T // tb,),
        in_specs=[pl.BlockSpec((2 * PEER_HEADS, PEER_NKEYS, tb), lambda i: (0, 0, i))],
        out_specs=[out_spec, out_spec, out_spec],
        out_shape=[jax.ShapeDtypeStruct((T, PEER_PICKS), jnp.int32),
                   jax.ShapeDtypeStruct((T, PEER_PICKS), F32),
                   jax.ShapeDtypeStruct((T, PEER_PICKS), F32)],
        scratch_shapes=[pltpu.VMEM((PEER_HEADS, PEER_TOPK, tb), F32),
                        pltpu.VMEM((PEER_HEADS, PEER_TOPK, tb), F32)],
        compiler_params=_cparams(("parallel",)),
        name="topk",
    )(scT)


PEER_U_VALU = 16
PEER_V_VALU = 8


PACK_TE = 2048


def _pack_kernel(x_ref, o_ref):
    s = pl.program_id(1)
    half = PACK_TE // 2
    bits = lambda v: pltpu.bitcast(v.astype(BF16).astype(F32), jnp.uint32)
    even = bits(x_ref[pl.ds(0, half, stride=2), :])
    odd = bits(x_ref[pl.ds(1, half, stride=2), :])
    o_ref[pl.ds(s, half, stride=D_MODEL // 128), :] = (even >> 16) | (odd & jnp.uint32(0xFFFF0000))


def _pack_table(tab):
    E = tab.shape[0]
    return pl.pallas_call(
        _pack_kernel,
        grid=(E // PACK_TE, D_MODEL // 128),
        in_specs=[pl.BlockSpec((PACK_TE, 128), lambda i, s: (i, s))],
        out_specs=pl.BlockSpec((4 * PACK_TE, 128), lambda i, s: (i, 0)),
        out_shape=jax.ShapeDtypeStruct((4 * E, 128), jnp.uint32),
        compiler_params=_cparams(("parallel", "arbitrary")),
        name="pack_table",
    )(tab)


def _split3(a):
    p0 = a.astype(BF16)
    r1 = a - p0.astype(F32)
    p1 = r1.astype(BF16)
    p2 = (r1 - p1.astype(F32)).astype(BF16)
    return p0, p1, p2


def _pair_expand(v, which, base, pairs, pieces=_split3):
    j = lax.broadcasted_iota(jnp.int32, (PEER_PICKS, 16 * pairs), 0)
    k = lax.broadcasted_iota(jnp.int32, (PEER_PICKS, 16 * pairs), 1)
    onehot = jnp.where(j == base + 2 * (k // 16) + which, 1.0, 0.0).astype(BF16)
    return sum(jnp.dot(p, onehot, preferred_element_type=F32) for p in pieces(v))


def _pair_targets(par, which, base, pairs):
    r = lax.broadcasted_iota(jnp.int32, (par.shape[0], 16 * pairs), 1) % 16
    one_piece = lambda p: (p.astype(BF16),)
    return 0.5 * (r.astype(F32) - _pair_expand(par, which, base, pairs, one_piece))


def _tile_words(tab_ref, off_ref, t, j):
    off = pl.multiple_of(off_ref[t, j], 8)
    return tab_ref[pl.ds(off, 8), :]


def _fill_lane_broadcasts(dst_ref, src, n):
    for j in range(n):
        dst_ref[j] = jnp.broadcast_to(src[:, j:j + 1], dst_ref.shape[1:])


def _expert_rows(words, shv):
    return pltpu.bitcast((words << shv) & jnp.uint32(0xFFFF0000), F32)


def _sublane_sums(ps):
    sub = lax.broadcasted_iota(jnp.int32, (8, 128), 0)
    lo4 = sub < 4
    t = [jnp.where(lo4, ps[j], ps[j + 4]) + pltpu.roll(jnp.where(lo4, ps[j + 4], ps[j]), 4, 0)
         for j in range(4)]
    m2 = (sub & 2) == 0
    u = [jnp.where(m2, t[j] + pltpu.roll(t[j], 6, 0), t[j + 2] + pltpu.roll(t[j + 2], 2, 0))
         for j in range(2)]
    m1 = (sub & 1) == 0
    return jnp.where(m1, u[0] + pltpu.roll(u[0], 7, 0), u[1] + pltpu.roll(u[1], 1, 0))


def _token_loop(tb, body, per_group=None):
    def group(i, carry):
        if per_group is not None:
            per_group(i)
        for s in range(PEER_UNROLL):
            body(i * PEER_UNROLL + s)
        return carry

    lax.fori_loop(0, tb // PEER_UNROLL, group, 0)


def _peer_u_kernel(off0_ref, par0_ref, gate0_ref, x_ref, tab_ref, sc_ref, w_ref, offo_ref, paro_ref,
                   off_ref, sem_ref, offv_ref, par_ref, gate_ref, e_scr, g_scr, tga_ref, tgb_ref, v3_ref, shb_ref,
                   actv_ref):
    tb = x_ref.shape[0]
    nv = PEER_U_VALU
    pairs = (PEER_PICKS - nv) // 2
    nblk = 16 * pairs // 128

    @pl.when(pl.program_id(0) == 0)
    def _():
        offv_ref[...] = off0_ref[...]
        par_ref[...] = par0_ref[...]
        gate_ref[...] = gate0_ref[...]

    def offsets_copy(g):
        rows = pl.ds(g * PEER_UNROLL, PEER_UNROLL)
        return pltpu.make_async_copy(offv_ref.at[rows], off_ref.at[rows], sem_ref.at[g])

    for g in range(tb // PEER_UNROLL):
        offsets_copy(g).start()
    offo_ref[...] = offv_ref[...]
    par = par_ref[...]
    paro_ref[...] = par
    tga_ref[...] = _pair_targets(par, 0, nv, pairs)
    tgb_ref[...] = _pair_targets(par, 1, nv, pairs)
    _fill_lane_broadcasts(shb_ref, ((1.0 - par) * 16.0).astype(jnp.uint32), nv)
    m_iota = lax.broadcasted_iota(jnp.int32, (8, 16 * pairs), 0).astype(F32)
    zero = jnp.zeros((8, 128), F32)
    ones = jnp.ones((8, 128), BF16)
    nt = (((1,), (1,)), ((), ()))

    def body(t):
        xt = x_ref[t]
        row = pl.ds(t, 1)
        rs = []
        for g in range(nv // 8):
            ps = []
            for jj in range(8):
                j = g * 8 + jj
                shv = jnp.broadcast_to(shb_ref[j, row, :], (8, 128))
                ps.append(_expert_rows(_tile_words(tab_ref, off_ref, t, j), shv) * xt)
            rs.append(_sublane_sums(ps))
        r = jnp.concatenate(rs, axis=0)
        hi = r.astype(BF16)
        lo = (r - hi.astype(F32)).astype(BF16)
        av = (lax.dot_general(ones, hi, nt, preferred_element_type=F32)
              + lax.dot_general(ones, lo, nt, preferred_element_type=F32))
        actv_ref[row, 0:nv] = av[0:1]
        tiles = [pltpu.bitcast(_tile_words(tab_ref, off_ref, t, j), BF16) for j in range(nv, PEER_PICKS)]
        wmat = jnp.concatenate(
            [jnp.concatenate([tiles[2 * q], tiles[2 * q + 1]], axis=1) for q in range(pairs)], axis=0)
        xf = jnp.concatenate([jnp.concatenate([xt, zero], axis=1),
                              jnp.concatenate([zero, xt], axis=1)], axis=0)
        xh = xf.astype(BF16)
        xl = (xf - xh.astype(F32)).astype(BF16)
        o = lax.dot_general(jnp.concatenate([xh, xl], axis=0), wmat, nt, preferred_element_type=F32)
        o = o[0:16] + o[16:32]
        za = jnp.where(tga_ref[row, :] == m_iota, o[0:8], 0.0)
        zb = jnp.where(tgb_ref[row, :] == m_iota, o[8:16], 0.0)
        for half, z in enumerate((za, zb)):
            v = _sublane_sums([z[:, i * 128:(i + 1) * 128] for i in range(nblk)] + [zero] * (8 - nblk))
            for i in range(nblk):
                v3_ref[half * nblk + i, row, :] = v[i:i + 1]

    consts = _topk_consts(128)

    def next_block_head(h):
        offsets_copy(h).wait()
        for c in range(tb // 128):
            lanes = slice(c * 128, (c + 1) * 128)
            e_scr[h, :, lanes], g_scr[h, :, lanes] = _topk_head(sc_ref.at[:, :, lanes], h, consts)

    _token_loop(tb, body, next_block_head)
    lane = lax.broadcasted_iota(jnp.int32, (128, PEER_PICKS), 0)
    pick = lax.broadcasted_iota(jnp.int32, (128, PEER_PICKS), 1)
    act = jnp.zeros((tb, PEER_PICKS), F32)
    for c in range(2 * nblk):
        sel = jnp.where(pick == nv + 2 * (8 * (c % nblk) + lane // 16) + c // nblk, 1.0, 0.0).astype(BF16)
        for p in _split3(v3_ref[c]):
            act = act + jnp.dot(p, sel, preferred_element_type=F32)
    act = jnp.where(pick[0:1] < nv, actv_ref[...], act)
    gelu = 0.5 * act * (1.0 + lax.erf(act * (2.0 ** -0.5)))
    w_ref[...] = gate_ref[...] * gelu
    _topk_finish(e_scr, g_scr, offv_ref, par_ref, gate_ref)


def _peer_specs(tb, tab):
    rows = pl.BlockSpec((tb, PEER_PICKS), lambda i: (i, 0))
    smem = pl.BlockSpec((tb, PEER_PICKS), lambda i: (i, 0), memory_space=pltpu.SMEM)
    table = pl.BlockSpec(tab.shape, lambda i: (0, 0), pipeline_mode=pl.Buffered(1))
    return rows, smem, table


def _peer_u(scT, x3, tab):
    T = x3.shape[0]
    tb = PEER_TB
    assert tb // PEER_UNROLL == PEER_HEADS
    nb = T // tb
    rows, _, table = _peer_specs(tb, tab)
    first = pl.BlockSpec((tb, PEER_PICKS), lambda i: (0, 0))
    pairs = (PEER_PICKS - PEER_U_VALU) // 2
    off0, par0, gate0 = _topk(scT[:, :, :tb])
    return pl.pallas_call(
        _peer_u_kernel,
        grid=(nb,),
        in_specs=[first, first, first, pl.BlockSpec((tb, 8, 128), lambda i: (i, 0, 0)), table,
                  pl.BlockSpec((2 * PEER_HEADS, PEER_NKEYS, tb), lambda i: (0, 0, jnp.minimum(i + 1, nb - 1)))],
        out_specs=[rows, rows, rows],
        out_shape=[jax.ShapeDtypeStruct((T, PEER_PICKS), F32),
                   jax.ShapeDtypeStruct((T, PEER_PICKS), jnp.int32),
                   jax.ShapeDtypeStruct((T, PEER_PICKS), F32)],
        scratch_shapes=[pltpu.SMEM((tb, PEER_PICKS), jnp.int32),
                        pltpu.SemaphoreType.DMA((tb // PEER_UNROLL,)),
                        pltpu.VMEM((tb, PEER_PICKS), jnp.int32),
                        pltpu.VMEM((tb, PEER_PICKS), F32), pltpu.VMEM((tb, PEER_PICKS), F32),
                        pltpu.VMEM((PEER_HEADS, PEER_TOPK, tb), F32), pltpu.VMEM((PEER_HEADS, PEER_TOPK, tb), F32),
                        pltpu.VMEM((tb, 16 * pairs), F32), pltpu.VMEM((tb, 16 * pairs), F32),
                        pltpu.VMEM((pairs // 4, tb, 128), F32),
                        pltpu.VMEM((PEER_U_VALU, tb, 128), jnp.uint32),
                        pltpu.VMEM((tb, PEER_PICKS), F32)],
        compiler_params=_cparams(("arbitrary",)),
        name="peer_u",
    )(off0, par0, gate0, x3, tab, scT)


def _peer_v_kernel(off_ref, par_ref, w_ref, tab_ref, o_ref, wa_ref, wb_ref, tga_ref, tgb_ref, shb_ref, wbc_ref):
    tb = o_ref.shape[0]
    nv = PEER_V_VALU
    pairs = (PEER_PICKS - nv) // 2
    par = par_ref[...]
    w = w_ref[...]
    wa_ref[...] = _pair_expand(w, 0, nv, pairs)
    wb_ref[...] = _pair_expand(w, 1, nv, pairs)
    tga_ref[...] = _pair_targets(par, 0, nv, pairs)
    tgb_ref[...] = _pair_targets(par, 1, nv, pairs)
    _fill_lane_broadcasts(shb_ref, ((1.0 - par) * 16.0).astype(jnp.uint32), nv)
    _fill_lane_broadcasts(wbc_ref, w, nv)
    m_iota = lax.broadcasted_iota(jnp.int32, (8, 16 * pairs), 0).astype(F32)
    nacc = 4

    def body(t):
        row = pl.ds(t, 1)
        acc = [jnp.zeros((8, 128), F32) for _ in range(nacc)]
        for j in range(nv):
            shv = jnp.broadcast_to(shb_ref[j, row, :], (8, 128))
            wv = jnp.broadcast_to(wbc_ref[j, row, :], (8, 128))
            acc[j % nacc] = acc[j % nacc] + wv * _expert_rows(_tile_words(tab_ref, off_ref, t, j), shv)
        tiles = [pltpu.bitcast(_tile_words(tab_ref, off_ref, t, j), BF16) for j in range(nv, PEER_PICKS)]
        vmat = jnp.concatenate([jnp.concatenate(tiles[0::2], axis=0),
                                jnp.concatenate(tiles[1::2], axis=0)], axis=1)
        la = jnp.where(tga_ref[row, :] == m_iota, wa_ref[row, :], 0.0)
        lb = jnp.where(tgb_ref[row, :] == m_iota, wb_ref[row, :], 0.0)
        lf = jnp.concatenate([la, lb], axis=0)
        lh = lf.astype(BF16)
        ll = (lf - lh.astype(F32)).astype(BF16)
        o = jnp.dot(jnp.concatenate([lh, ll], axis=0), vmat, preferred_element_type=F32)
        o_ref[t] = (((o[0:8, 0:128] + o[8:16, 128:256]) + (o[16:24, 0:128] + o[24:32, 128:256]))
                    + ((acc[0] + acc[1]) + (acc[2] + acc[3])))

    _token_loop(tb, body)


def _peer_v(off, par, w, tab):
    T = par.shape[0]
    tb = PEER_TB
    rows, smem, table = _peer_specs(tb, tab)
    pairs = (PEER_PICKS - PEER_V_VALU) // 2
    return pl.pallas_call(
        _peer_v_kernel,
        grid=(T // tb,),
        in_specs=[smem, rows, rows, table],
        out_specs=pl.BlockSpec((tb, 8, 128), lambda i: (i, 0, 0)),
        out_shape=jax.ShapeDtypeStruct((T, 8, 128), F32),
        scratch_shapes=[pltpu.VMEM((tb, 16 * pairs), F32) for _ in range(4)]
        + [pltpu.VMEM((PEER_V_VALU, tb, 128), jnp.uint32), pltpu.VMEM((PEER_V_VALU, tb, 128), F32)],
        compiler_params=_cparams(("parallel",)),
        name="peer_v",
    )(off, par, w, tab)


def _final_kernel(h_ref, p_ref, g_ref, o_ref):
    nt = D_MODEL // 128
    tm = h_ref.shape[0]
    p = jnp.concatenate([p_ref[pl.ds(c, tm, stride=nt), :] for c in range(nt)], axis=1)
    h = h_ref[...] + p
    o_ref[...] = h * lax.rsqrt(jnp.mean(h * h, axis=-1, keepdims=True) + EPS) * g_ref[...]


def _final(h, p, g):
    T = h.shape[0]
    row = pl.BlockSpec((FINAL_TM, D_MODEL), lambda i: (i, 0))
    return pl.pallas_call(
        _final_kernel,
        grid=(T // FINAL_TM,),
        in_specs=[row, pl.BlockSpec((FINAL_TM * (D_MODEL // 128), 128), lambda i: (i, 0)),
                  _const_spec((1, D_MODEL))],
        out_specs=row,
        out_shape=jax.ShapeDtypeStruct((T, D_MODEL), F32),
        compiler_params=_cparams(("parallel",)),
        name="final_norm",
    )(h, p, g)


def _layer(h2, B, S, norm1_g, w_in, w_ret_out, w_att_out, w_out, norm2_g, peer_wq, peer_subkeys, peer_u, peer_v):
    T = B * S
    aq, ak, av = 3072, 3072 + ATT_W, 3072 + 2 * ATT_W
    att_cols = [w_in[:, o + g * ATT_OUT_W:o + (g + 1) * ATT_OUT_W] for g in range(len(ATT_GROUPS)) for o in (aq, ak, av)]
    w_perm = jnp.concatenate([w_in[:, :3072], w_in[:, 7680:9728]] + att_cols, axis=1).astype(BF16)
    main, a0, a1, a2 = _proj(h2, norm1_g.reshape(1, D_MODEL), w_perm, B, S)
    retg = _retention(main.reshape(B, S, MAIN_W)).reshape(T, RET_V_W)
    (o1, l1), (o2, l2), (o3, l3) = (_att_group(a, g) for g, a in enumerate((a0.reshape(B, 1, S, ATT_SEC_W), a1, a2)))
    sk = peer_subkeys.reshape(2 * PEER_HEADS, PEER_NKEYS, PEER_DKEY // 2).astype(BF16)
    h_mid, xn2, scT = _mix(retg, o1, o2, o3, l1, l2, l3, main, h2,
                           w_ret_out.astype(BF16), w_att_out.astype(BF16), w_out.astype(BF16),
                           norm2_g.reshape(1, D_MODEL), peer_wq.astype(BF16), sk)
    w, off, par = _peer_u(scT, xn2.reshape(T, 8, 128), _pack_table(peer_u))
    pout = _peer_v(off, par, w, _pack_table(peer_v))
    return h_mid, pout.reshape(T * 8, 128)


def kernel(x, norm1_g, w_in, w_ret_out, w_att_out, w_out, norm2_g, peer_wq, peer_subkeys, peer_u, peer_v, normf_g):
    B, S, D = x.shape
    assert w_in.shape[0] == 1, "single-layer block"
    h, pout = _layer(x.reshape(B * S, D), B, S, norm1_g[0], w_in[0], w_ret_out[0], w_att_out[0], w_out[0],
                     norm2_g[0], peer_wq[0], peer_subkeys[0], peer_u[0], peer_v[0])
    return _final(h, pout, normf_g.reshape(1, D)).reshape(B, S, D)
```

```python
import functools
import math

import jax
import jax.numpy as jnp
from jax import lax
from jax.experimental import pallas as pl
from jax.experimental.pallas import tpu as pltpu

F32 = jnp.float32
BF16 = jnp.bfloat16

D_MODEL = 1024
RET_HEADS = 8
RET_DK = 64
RET_DV = 128
RET_CHUNK = 128
ATT_GROUPS = ((128, 1), (512, 4), (2048, 16))
ATT_HPG = 4
ATT_HEADS = 12
ATT_DH = 128
ATT_BLOCK = 128
ATT_BATCH = (4, 4, 2)
PEER_HEADS = 8
PEER_NKEYS = 128
PEER_DKEY = 256
PEER_TOPK = 16
PEER_PICKS = PEER_HEADS * PEER_TOPK
EPS = 1e-6

RET_QK_W = RET_HEADS * RET_DK
RET_V_W = RET_HEADS * RET_DV
ATT_W = ATT_HEADS * ATT_DH
ATT_OUT_W = ATT_HPG * ATT_DH
OFF_RQ, OFF_RK, OFF_RV, OFF_RG = 0, 512, 1024, 2048
OFF_GR, OFF_GA = 3072, 4096
MAIN_W = 5120
ATT_SEC_W = 3 * ATT_OUT_W
IN_W = MAIN_W + len(ATT_GROUPS) * ATT_SEC_W

VMEM_LIMIT_BYTES = 60 * 1024 * 1024

PROJ_TM = 256
PROJ_TN = 512
MIX_TM = 256
TOPK_TB = 512
PEER_TB = 256
PEER_UNROLL = 32
FINAL_TM = 512


def _cparams(sem):
    return pltpu.CompilerParams(dimension_semantics=sem, vmem_limit_bytes=VMEM_LIMIT_BYTES)


def _const_spec(shape):
    nd = len(shape)
    return pl.BlockSpec(shape, lambda *_: (0,) * nd)


def _proj_kernel(x_ref, g_ref, w_ref, main_ref, a0_ref, a1_ref, a2_ref, scr_ref):
    x = x_ref[...]
    ms = jnp.mean(x * x, axis=-1, keepdims=True)
    xn = (x * lax.rsqrt(ms + EPS) * g_ref[...]).astype(BF16)

    def chunk(j):
        return jnp.dot(xn, w_ref[:, j * PROJ_TN:(j + 1) * PROJ_TN], preferred_element_type=F32)

    nmain = MAIN_W // PROJ_TN
    for j in range(nmain):
        main_ref[:, j * PROJ_TN:(j + 1) * PROJ_TN] = chunk(j).astype(BF16)
    for s in range(3):
        a0_ref[:, s * PROJ_TN:(s + 1) * PROJ_TN] = chunk(nmain + s).astype(BF16)
    for g, a_ref in ((1, a1_ref), (2, a2_ref)):
        d = ATT_GROUPS[g][1]
        for s in range(3):
            res = chunk(nmain + 3 * g + s)
            for c in range(PROJ_TN // 128):
                scr_ref[c] = res[:, c * 128:(c + 1) * 128]
            for r in range(d):
                piece = jnp.concatenate(
                    [scr_ref[c, pl.ds(r, PROJ_TM // d, stride=d), :] for c in range(PROJ_TN // 128)], axis=1)
                a_ref[0, r, :, s * PROJ_TN:(s + 1) * PROJ_TN] = piece.astype(BF16)


def _proj(x2, g, w, B, S):
    T = x2.shape[0]
    tiles = S // PROJ_TM
    d1, d2 = ATT_GROUPS[1][1], ATT_GROUPS[2][1]
    return pl.pallas_call(
        _proj_kernel,
        grid=(T // PROJ_TM,),
        in_specs=[
            pl.BlockSpec((PROJ_TM, D_MODEL), lambda i: (i, 0)),
            _const_spec((1, D_MODEL)),
            pl.BlockSpec((D_MODEL, IN_W), lambda i: (0, 0), pipeline_mode=pl.Buffered(1)),
        ],
        out_specs=[
            pl.BlockSpec((PROJ_TM, MAIN_W), lambda i: (i, 0)),
            pl.BlockSpec((PROJ_TM, ATT_SEC_W), lambda i: (i, 0)),
            pl.BlockSpec((1, d1, PROJ_TM // d1, ATT_SEC_W), lambda i: (i // tiles, 0, i % tiles, 0)),
            pl.BlockSpec((1, d2, PROJ_TM // d2, ATT_SEC_W), lambda i: (i // tiles, 0, i % tiles, 0)),
        ],
        out_shape=[
            jax.ShapeDtypeStruct((T, MAIN_W), BF16),
            jax.ShapeDtypeStruct((T, ATT_SEC_W), BF16),
            jax.ShapeDtypeStruct((B, d1, S // d1, ATT_SEC_W), BF16),
            jax.ShapeDtypeStruct((B, d2, S // d2, ATT_SEC_W), BF16),
        ],
        scratch_shapes=[pltpu.VMEM((PROJ_TN // 128, PROJ_TM, 128), F32)],
        compiler_params=_cparams(("parallel",)),
        name="proj",
    )(x2, g, w)


def _retention_kernel(q_ref, k_ref, v_ref, rg_ref, o_ref, state_ref, decay_ref):
    C = RET_CHUNK

    @pl.when(pl.program_id(1) == 0)
    def _():
        state_ref[...] = jnp.zeros_like(state_ref)
        pi = lax.broadcasted_iota(jnp.int32, (C, C), 0)
        pj = lax.broadcasted_iota(jnp.int32, (C, C), 1)
        diff = (pi - pj).astype(F32)
        for h in range(RET_HEADS):
            lg = math.log1p(-(2.0 ** (-5.0 - h)))
            decay_ref[h] = jnp.where(diff >= 0, jnp.exp(jnp.maximum(diff, 0.0) * lg), 0.0)

    pos = lax.broadcasted_iota(jnp.int32, (C, 1), 0).astype(F32)
    q = q_ref[0]
    k = k_ref[0]
    v = v_ref[0]
    rg = rg_ref[0]
    for h in range(RET_HEADS):
        lg = math.log1p(-(2.0 ** (-5.0 - h)))
        decay = decay_ref[h]
        w_k = jnp.exp((C - 1 - pos) * lg)
        w_q = jnp.exp((pos + 1.0) * lg)
        qh = q[:, h * RET_DK:(h + 1) * RET_DK]
        kh = k[:, h * RET_DK:(h + 1) * RET_DK].astype(F32) * (RET_DK ** -0.5)
        vh = v[:, h * RET_DV:(h + 1) * RET_DV]
        s = lax.dot_general(qh, kh.astype(BF16), (((1,), (1,)), ((), ())), preferred_element_type=F32)
        p = (s * decay).astype(BF16)
        inner = jnp.dot(p, vh, preferred_element_type=F32)
        st = state_ref[h]
        cross = jnp.dot((qh.astype(F32) * w_q).astype(BF16), st.astype(BF16), preferred_element_type=F32)
        kw = (kh * w_k).astype(BF16)
        kv = lax.dot_general(kw, vh, (((0,), (0,)), ((), ())), preferred_element_type=F32)
        state_ref[h] = math.exp(C * lg) * st + kv
        ret = inner + cross
        rn = ret * lax.rsqrt(jnp.mean(ret * ret, axis=-1, keepdims=True) + EPS)
        g = rg[:, h * RET_DV:(h + 1) * RET_DV].astype(F32)
        o_ref[0, :, h * RET_DV:(h + 1) * RET_DV] = (g * jax.nn.sigmoid(g) * rn).astype(BF16)


def _retention(proj3):
    B, S, _ = proj3.shape
    n = S // RET_CHUNK
    return pl.pallas_call(
        _retention_kernel,
        grid=(B, n),
        in_specs=[
            pl.BlockSpec((1, RET_CHUNK, RET_QK_W), lambda b, c: (b, c, OFF_RQ // RET_QK_W)),
            pl.BlockSpec((1, RET_CHUNK, RET_QK_W), lambda b, c: (b, c, OFF_RK // RET_QK_W)),
            pl.BlockSpec((1, RET_CHUNK, RET_V_W), lambda b, c: (b, c, OFF_RV // RET_V_W)),
            pl.BlockSpec((1, RET_CHUNK, RET_V_W), lambda b, c: (b, c, OFF_RG // RET_V_W)),
        ],
        out_specs=pl.BlockSpec((1, RET_CHUNK, RET_V_W), lambda b, c: (b, c, 0)),
        out_shape=jax.ShapeDtypeStruct((B, S, RET_V_W), BF16),
        scratch_shapes=[pltpu.VMEM((RET_HEADS, RET_DK, RET_DV), F32),
                        pltpu.VMEM((RET_HEADS, RET_CHUNK, RET_CHUNK), F32)],
        compiler_params=_cparams(("parallel", "arbitrary")),
        name="retention",
    )(proj3, proj3, proj3, proj3)


def _att_kernel(q_ref, kp_ref, kc_ref, vp_ref, vc_ref, o_ref, lse_ref, *, group, dilation, span):
    n = pl.program_id(1)
    r = pl.program_id(2)
    iq = lax.broadcasted_iota(jnp.int32, (ATT_BLOCK, 2 * ATT_BLOCK), 0)
    jk = lax.broadcasted_iota(jnp.int32, (ATT_BLOCK, 2 * ATT_BLOCK), 1)
    dist = iq + ATT_BLOCK - jk
    valid = (dist >= 0) & (dist <= span) & ((jk >= ATT_BLOCK) | (n > 0))
    distf = (dilation * dist).astype(F32)
    rows = pl.ds(r, ATT_BLOCK, stride=dilation) if dilation > 1 else slice(None)
    for bb in range(q_ref.shape[0]):
        q = q_ref[bb, 0]
        kk = jnp.concatenate([kp_ref[bb, 0], kc_ref[bb, 0]], axis=0)
        vv = jnp.concatenate([vp_ref[bb, 0], vc_ref[bb, 0]], axis=0)
        for i in range(ATT_HPG):
            slope = 2.0 ** (-8.0 * (group * ATT_HPG + i + 1) / ATT_HEADS)
            sl = slice(i * ATT_DH, (i + 1) * ATT_DH)
            s = lax.dot_general(q[:, sl], kk[:, sl], (((1,), (1,)), ((), ())), preferred_element_type=F32)
            s = s * (ATT_DH ** -0.5)
            s = jnp.where(valid, s - slope * distf, -jnp.inf)
            m = jnp.max(s, axis=-1, keepdims=True)
            p = jnp.exp(s - m)
            den = jnp.sum(p, axis=-1, keepdims=True)
            o_ref[i, bb, rows, :] = jnp.dot(p.astype(BF16), vv[:, sl], preferred_element_type=F32) / den
            lse_ref[i, bb, rows, :] = jnp.broadcast_to(m + jnp.log(den), (ATT_BLOCK, ATT_DH))


def _att_group(qkv, group):
    window, d = ATT_GROUPS[group]
    B, _, L, _ = qkv.shape
    S = L * d
    nL = L // ATT_BLOCK

    nbat = ATT_BATCH[group]

    def cur(sec):
        return pl.BlockSpec((nbat, 1, ATT_BLOCK, ATT_OUT_W), lambda b, n, r: (b, r, n, sec))

    def prev(sec):
        return pl.BlockSpec((nbat, 1, ATT_BLOCK, ATT_OUT_W), lambda b, n, r: (b, r, jnp.maximum(n - 1, 0), sec))

    out_spec = pl.BlockSpec((ATT_HPG, nbat, ATT_BLOCK * d, ATT_DH), lambda b, n, r: (0, b, n, 0))
    out_shape = jax.ShapeDtypeStruct((ATT_HPG, B, S, ATT_DH), F32)
    o, lse = pl.pallas_call(
        functools.partial(_att_kernel, group=group, dilation=d, span=window // d),
        grid=(B // nbat, nL, d),
        in_specs=[cur(0), prev(1), cur(1), prev(2), cur(2)],
        out_specs=[out_spec, out_spec],
        out_shape=[out_shape, out_shape],
        compiler_params=_cparams(("parallel", "arbitrary", "arbitrary")),
        name=f"att_g{group}",
    )(qkv, qkv, qkv, qkv, qkv)
    return o.reshape(ATT_HPG, B * S, ATT_DH), lse.reshape(ATT_HPG, B * S, ATT_DH)


def _mix_kernel(retg_ref, o1_ref, o2_ref, o3_ref, l1_ref, l2_ref, l3_ref, gr_ref, ga_ref, x_ref,
                wro_ref, wao_ref, wo_ref, g2_ref, wq_ref, sk_ref, h_ref, xn_ref, sc_ref):
    heads = lambda ref: jnp.concatenate([ref[i] for i in range(ATT_HPG)], axis=1)
    l1, l2, l3 = heads(l1_ref), heads(l2_ref), heads(l3_ref)
    mx = jnp.maximum(jnp.maximum(l1, l2), l3)
    e1, e2, e3 = jnp.exp(l1 - mx), jnp.exp(l2 - mx), jnp.exp(l3 - mx)
    att = (e1 * heads(o1_ref) + e2 * heads(o2_ref) + e3 * heads(o3_ref)) / (e1 + e2 + e3)
    a_branch = jnp.dot(att.astype(BF16), wao_ref[...], preferred_element_type=F32)
    r_branch = jnp.dot(retg_ref[...], wro_ref[...], preferred_element_type=F32)
    merged = (jax.nn.sigmoid(gr_ref[...].astype(F32)) * r_branch
              + jax.nn.sigmoid(ga_ref[...].astype(F32)) * a_branch)
    h = x_ref[...] + jnp.dot(merged.astype(BF16), wo_ref[...], preferred_element_type=F32)
    h_ref[...] = h
    xn = h * lax.rsqrt(jnp.mean(h * h, axis=-1, keepdims=True) + EPS) * g2_ref[...]
    for c in range(D_MODEL // 128):
        xn_ref[pl.ds(c, h.shape[0], stride=D_MODEL // 128), :] = xn[:, c * 128:(c + 1) * 128]
    qp = jnp.dot(xn.astype(BF16), wq_ref[...], preferred_element_type=F32).astype(BF16)
    half = PEER_DKEY // 2
    for hp in range(2 * PEER_HEADS):
        sc_ref[hp] = lax.dot_general(sk_ref[hp], qp[:, hp * half:(hp + 1) * half],
                                     (((1,), (1,)), ((), ())), preferred_element_type=F32)


def _mix(retg, o1, o2, o3, l1, l2, l3, proj, x2, wro, wao, wo, g2, wq, sk):
    T = x2.shape[0]
    tm = MIX_TM
    row = lambda w: pl.BlockSpec((tm, w), lambda i: (i, 0))
    hd = pl.BlockSpec((ATT_HPG, tm, ATT_DH), lambda i: (0, i, 0))
    nhp = 2 * PEER_HEADS
    nt = D_MODEL // 128
    return pl.pallas_call(
        _mix_kernel,
        grid=(T // tm,),
        in_specs=[
            row(RET_V_W), hd, hd, hd, hd, hd, hd,
            pl.BlockSpec((tm, D_MODEL), lambda i: (i, OFF_GR // D_MODEL)),
            pl.BlockSpec((tm, D_MODEL), lambda i: (i, OFF_GA // D_MODEL)),
            row(D_MODEL),
            _const_spec(wro.shape), _const_spec(wao.shape), _const_spec(wo.shape),
            _const_spec(g2.shape), _const_spec(wq.shape), _const_spec(sk.shape),
        ],
        out_specs=[row(D_MODEL), pl.BlockSpec((tm * nt, 128), lambda i: (i, 0)),
                   pl.BlockSpec((nhp, PEER_NKEYS, tm), lambda i: (0, 0, i))],
        out_shape=[jax.ShapeDtypeStruct((T, D_MODEL), F32),
                   jax.ShapeDtypeStruct((T * nt, 128), F32),
                   jax.ShapeDtypeStruct((nhp, PEER_NKEYS, T), F32)],
        compiler_params=_cparams(("parallel",)),
        name="mix",
    )(retg, o1, o2, o3, l1, l2, l3, proj, proj, x2, wro, wao, wo, g2, wq, sk)


def _topk_rows(s, ids, k, fill):
    vals, idxs = [], []
    for _ in range(k):
        m = jnp.max(s, axis=0, keepdims=True)
        i = jnp.min(jnp.where(s == m, ids, fill), axis=0, keepdims=True)
        vals.append(m)
        idxs.append(i)
        s = jnp.where(ids == i, -jnp.inf, s)
    return jnp.concatenate(vals, axis=0), jnp.concatenate(idxs, axis=0)


_CAND_B = [PEER_TOPK // (a + 1) for a in range(PEER_TOPK)]
_CAND_ROWS = -(-sum(_CAND_B) // 8) * 8


def _topk_consts(W):
    K = PEER_TOPK
    key_ids = lax.broadcasted_iota(jnp.int32, (PEER_NKEYS, W), 0)
    npad = _CAND_ROWS - sum(_CAND_B)
    cand_ids = jnp.concatenate(
        [jnp.full((1, W), a * K + b, jnp.int32) for a in range(K) for b in range(_CAND_B[a])]
        + [jnp.full((npad, W), K * K, jnp.int32)], axis=0)
    pad = jnp.full((npad, W), -jnp.inf, F32)
    return key_ids, cand_ids, pad


def _topk_head(sc_ref, h, consts):
    K = PEER_TOPK
    key_ids, cand_ids, pad = consts
    v0, i0 = _topk_rows(sc_ref[2 * h], key_ids, K, PEER_NKEYS)
    v1, i1 = _topk_rows(sc_ref[2 * h + 1], key_ids, K, PEER_NKEYS)
    cand = jnp.concatenate([v0[a:a + 1] + v1[0:_CAND_B[a]] for a in range(K)] + [pad], axis=0)
    best_s, best_c = _topk_rows(cand, cand_ids, K, K * K)
    ca = best_c // K
    cb = best_c % K
    ia = jnp.zeros_like(best_c)
    ib = jnp.zeros_like(best_c)
    for a in range(K):
        ia = jnp.where(ca == a, i0[a:a + 1], ia)
        ib = jnp.where(cb == a, i1[a:a + 1], ib)
    ex = jnp.exp(best_s - best_s[0:1])
    return (ia * PEER_NKEYS + ib).astype(F32), ex / jnp.sum(ex, axis=0, keepdims=True)


def _topk_finish(e_scr, g_scr, off_ref, par_ref, gate_ref):
    W = e_scr.shape[-1]
    e = e_scr[...].reshape(PEER_PICKS, W).T.astype(jnp.int32)
    off_ref[...] = lax.shift_right_logical(e, 1) * 8
    par_ref[...] = (e & 1).astype(F32)
    gate_ref[...] = g_scr[...].reshape(PEER_PICKS, W).T


def _topk_kernel(sc_ref, off_ref, par_ref, gate_ref, e_scr, g_scr):
    consts = _topk_consts(sc_ref.shape[-1])

    def head(h, carry):
        e_scr[h], g_scr[h] = _topk_head(sc_ref, h, consts)
        return carry

    lax.fori_loop(0, PEER_HEADS, head, 0)
    _topk_finish(e_scr, g_scr, off_ref, par_ref, gate_ref)


def _topk(scT):
    T = scT.shape[-1]
    tb = min(TOPK_TB, T)
    out_spec = pl.BlockSpec((tb, PEER_PICKS), lambda i: (i, 0))
    return pl.pallas_call(
        _topk_kernel,
        grid=(T // tb,),
        in_specs=[pl.BlockSpec((2 * PEER_HEADS, PEER_NKEYS, tb), lambda i: (0, 0, i))],
        out_specs=[out_spec, out_spec, out_spec],
        out_shape=[jax.ShapeDtypeStruct((T, PEER_PICKS), jnp.int32),
                   jax.ShapeDtypeStruct((T, PEER_PICKS), F32),
                   jax.ShapeDtypeStruct((T, PEER_PICKS), F32)],
        scratch_shapes=[pltpu.VMEM((PEER_HEADS, PEER_TOPK, tb), F32),
                        pltpu.VMEM((PEER_HEADS, PEER_TOPK, tb), F32)],
        compiler_params=_cparams(("parallel",)),
        name="topk",
    )(scT)


PEER_U_VALU = 16
PEER_V_VALU = 16


PACK_TE = 2048


def _pack_kernel(x_ref, o_ref):
    s = pl.program_id(1)
    half = PACK_TE // 2
    bits = lambda v: pltpu.bitcast(v.astype(BF16).astype(F32), jnp.uint32)
    even = bits(x_ref[pl.ds(0, half, stride=2), :])
    odd = bits(x_ref[pl.ds(1, half, stride=2), :])
    o_ref[pl.ds(s, half, stride=D_MODEL // 128), :] = (even >> 16) | (odd & jnp.uint32(0xFFFF0000))


def _pack_table(tab):
    E = tab.shape[0]
    return pl.pallas_call(
        _pack_kernel,
        grid=(E // PACK_TE, D_MODEL // 128),
        in_specs=[pl.BlockSpec((PACK_TE, 128), lambda i, s: (i, s))],
        out_specs=pl.BlockSpec((4 * PACK_TE, 128), lambda i, s: (i, 0)),
        out_shape=jax.ShapeDtypeStruct((4 * E, 128), jnp.uint32),
        compiler_params=_cparams(("parallel", "arbitrary")),
        name="pack_table",
    )(tab)


def _split3(a):
    p0 = a.astype(BF16)
    r1 = a - p0.astype(F32)
    p1 = r1.astype(BF16)
    p2 = (r1 - p1.astype(F32)).astype(BF16)
    return p0, p1, p2


def _pair_expand(v, which, base, pairs, pieces=_split3):
    j = lax.broadcasted_iota(jnp.int32, (PEER_PICKS, 16 * pairs), 0)
    k = lax.broadcasted_iota(jnp.int32, (PEER_PICKS, 16 * pairs), 1)
    onehot = jnp.where(j == base + 2 * (k // 16) + which, 1.0, 0.0).astype(BF16)
    return sum(jnp.dot(p, onehot, preferred_element_type=F32) for p in pieces(v))


def _pair_targets(par, which, base, pairs):
    r = lax.broadcasted_iota(jnp.int32, (par.shape[0], 16 * pairs), 1) % 16
    one_piece = lambda p: (p.astype(BF16),)
    return 0.5 * (r.astype(F32) - _pair_expand(par, which, base, pairs, one_piece))


def _tile_words(tab_ref, off_ref, t, j):
    off = pl.multiple_of(off_ref[t, j], 8)
    return tab_ref[pl.ds(off, 8), :]


def _fill_lane_broadcasts(dst_ref, src, n):
    for j in range(n):
        dst_ref[j] = jnp.broadcast_to(src[:, j:j + 1], dst_ref.shape[1:])


def _expert_rows(words, shv):
    return pltpu.bitcast((words << shv) & jnp.uint32(0xFFFF0000), F32)


def _sublane_sums(ps):
    sub = lax.broadcasted_iota(jnp.int32, (8, 128), 0)
    lo4 = sub < 4
    t = [jnp.where(lo4, ps[j], ps[j + 4]) + pltpu.roll(jnp.where(lo4, ps[j + 4], ps[j]), 4, 0)
         for j in range(4)]
    m2 = (sub & 2) == 0
    u = [jnp.where(m2, t[j] + pltpu.roll(t[j], 6, 0), t[j + 2] + pltpu.roll(t[j + 2], 2, 0))
         for j in range(2)]
    m1 = (sub & 1) == 0
    return jnp.where(m1, u[0] + pltpu.roll(u[0], 7, 0), u[1] + pltpu.roll(u[1], 1, 0))


def _token_loop(tb, body, per_group=None):
    def group(i, carry):
        if per_group is not None:
            per_group(i)
        for s in range(PEER_UNROLL):
            body(i * PEER_UNROLL + s)
        return carry

    lax.fori_loop(0, tb // PEER_UNROLL, group, 0)


def _peer_u_kernel(off0_ref, par0_ref, gate0_ref, x_ref, tab_ref, sc_ref, w_ref, offo_ref, paro_ref,
                   off_ref, sem_ref, offv_ref, par_ref, gate_ref, e_scr, g_scr, tga_ref, tgb_ref, v3_ref, shb_ref,
                   actv_ref):
    tb = x_ref.shape[0]
    nv = PEER_U_VALU
    pairs = (PEER_PICKS - nv) // 2
    nblk = 16 * pairs // 128

    @pl.when(pl.program_id(0) == 0)
    def _():
        offv_ref[...] = off0_ref[...]
        par_ref[...] = par0_ref[...]
        gate_ref[...] = gate0_ref[...]

    def offsets_copy(g):
        rows = pl.ds(g * PEER_UNROLL, PEER_UNROLL)
        return pltpu.make_async_copy(offv_ref.at[rows], off_ref.at[rows], sem_ref.at[g])

    for g in range(tb // PEER_UNROLL):
        offsets_copy(g).start()
    offo_ref[...] = offv_ref[...]
    par = par_ref[...]
    paro_ref[...] = par
    tga_ref[...] = _pair_targets(par, 0, nv, pairs)
    tgb_ref[...] = _pair_targets(par, 1, nv, pairs)
    _fill_lane_broadcasts(shb_ref, ((1.0 - par) * 16.0).astype(jnp.uint32), nv)
    m_iota = lax.broadcasted_iota(jnp.int32, (8, 16 * pairs), 0).astype(F32)
    zero = jnp.zeros((8, 128), F32)
    ones = jnp.ones((8, 128), BF16)
    nt = (((1,), (1,)), ((), ()))

    def body(t):
        xt = x_ref[t]
        row = pl.ds(t, 1)
        rs = []
        for g in range(nv // 8):
            ps = []
            for jj in range(8):
                j = g * 8 + jj
                shv = jnp.broadcast_to(shb_ref[j, row, :], (8, 128))
                ps.append(_expert_rows(_tile_words(tab_ref, off_ref, t, j), shv) * xt)
            rs.append(_sublane_sums(ps))
        r = jnp.concatenate(rs, axis=0)
        hi = r.astype(BF16)
        lo = (r - hi.astype(F32)).astype(BF16)
        av = (lax.dot_general(ones, hi, nt, preferred_element_type=F32)
              + lax.dot_general(ones, lo, nt, preferred_element_type=F32))
        actv_ref[row, 0:nv] = av[0:1]
        tiles = [pltpu.bitcast(_tile_words(tab_ref, off_ref, t, j), BF16) for j in range(nv, PEER_PICKS)]
        wmat = jnp.concatenate(
            [jnp.concatenate([tiles[2 * q], tiles[2 * q + 1]], axis=1) for q in range(pairs)], axis=0)
        xf = jnp.concatenate([jnp.concatenate([xt, zero], axis=1),
                              jnp.concatenate([zero, xt], axis=1)], axis=0)
        xh = xf.astype(BF16)
        xl = (xf - xh.astype(F32)).astype(BF16)
        o = lax.dot_general(jnp.concatenate([xh, xl], axis=0), wmat, nt, preferred_element_type=F32)
        o = o[0:16] + o[16:32]
        za = jnp.where(tga_ref[row, :] == m_iota, o[0:8], 0.0)
        zb = jnp.where(tgb_ref[row, :] == m_iota, o[8:16], 0.0)
        for half, z in enumerate((za, zb)):
            v = _sublane_sums([z[:, i * 128:(i + 1) * 128] for i in range(nblk)] + [zero] * (8 - nblk))
            for i in range(nblk):
                v3_ref[half * nblk + i, row, :] = v[i:i + 1]

    consts = _topk_consts(128)

    def next_block_head(h):
        offsets_copy(h).wait()
        for c in range(tb // 128):
            lanes = slice(c * 128, (c + 1) * 128)
            e_scr[h, :, lanes], g_scr[h, :, lanes] = _topk_head(sc_ref.at[:, :, lanes], h, consts)

    _token_loop(tb, body, next_block_head)
    lane = lax.broadcasted_iota(jnp.int32, (128, PEER_PICKS), 0)
    pick = lax.broadcasted_iota(jnp.int32, (128, PEER_PICKS), 1)
    act = jnp.zeros((tb, PEER_PICKS), F32)
    for c in range(2 * nblk):
        sel = jnp.where(pick == nv + 2 * (8 * (c % nblk) + lane // 16) + c // nblk, 1.0, 0.0).astype(BF16)
        for p in _split3(v3_ref[c]):
            act = act + jnp.dot(p, sel, preferred_element_type=F32)
    act = jnp.where(pick[0:1] < nv, actv_ref[...], act)
    gelu = 0.5 * act * (1.0 + lax.erf(act * (2.0 ** -0.5)))
    w_ref[...] = gate_ref[...] * gelu
    _topk_finish(e_scr, g_scr, offv_ref, par_ref, gate_ref)


def _peer_specs(tb, tab):
    rows = pl.BlockSpec((tb, PEER_PICKS), lambda i: (i, 0))
    smem = pl.BlockSpec((tb, PEER_PICKS), lambda i: (i, 0), memory_space=pltpu.SMEM)
    table = pl.BlockSpec(tab.shape, lambda i: (0, 0), pipeline_mode=pl.Buffered(1))
    return rows, smem, table


def _peer_u(scT, x3, tab):
    T = x3.shape[0]
    tb = PEER_TB
    assert tb // PEER_UNROLL == PEER_HEADS
    nb = T // tb
    rows, _, table = _peer_specs(tb, tab)
    first = pl.BlockSpec((tb, PEER_PICKS), lambda i: (0, 0))
    pairs = (PEER_PICKS - PEER_U_VALU) // 2
    off0, par0, gate0 = _topk(scT[:, :, :tb])
    return pl.pallas_call(
        _peer_u_kernel,
        grid=(nb,),
        in_specs=[first, first, first, pl.BlockSpec((tb, 8, 128), lambda i: (i, 0, 0)), table,
                  pl.BlockSpec((2 * PEER_HEADS, PEER_NKEYS, tb), lambda i: (0, 0, jnp.minimum(i + 1, nb - 1)))],
        out_specs=[rows, rows, rows],
        out_shape=[jax.ShapeDtypeStruct((T, PEER_PICKS), F32),
                   jax.ShapeDtypeStruct((T, PEER_PICKS), jnp.int32),
                   jax.ShapeDtypeStruct((T, PEER_PICKS), F32)],
        scratch_shapes=[pltpu.SMEM((tb, PEER_PICKS), jnp.int32),
                        pltpu.SemaphoreType.DMA((tb // PEER_UNROLL,)),
                        pltpu.VMEM((tb, PEER_PICKS), jnp.int32),
                        pltpu.VMEM((tb, PEER_PICKS), F32), pltpu.VMEM((tb, PEER_PICKS), F32),
                        pltpu.VMEM((PEER_HEADS, PEER_TOPK, tb), F32), pltpu.VMEM((PEER_HEADS, PEER_TOPK, tb), F32),
                        pltpu.VMEM((tb, 16 * pairs), F32), pltpu.VMEM((tb, 16 * pairs), F32),
                        pltpu.VMEM((pairs // 4, tb, 128), F32),
                        pltpu.VMEM((PEER_U_VALU, tb, 128), jnp.uint32),
                        pltpu.VMEM((tb, PEER_PICKS), F32)],
        compiler_params=_cparams(("arbitrary",)),
        name="peer_u",
    )(off0, par0, gate0, x3, tab, scT)


def _peer_v_kernel(off_ref, par_ref, w_ref, tab_ref, o_ref, wa_ref, wb_ref, tga_ref, tgb_ref, shb_ref, wbc_ref):
    tb = o_ref.shape[0]
    nv = PEER_V_VALU
    pairs = (PEER_PICKS - nv) // 2
    par = par_ref[...]
    w = w_ref[...]
    wa_ref[...] = _pair_expand(w, 0, nv, pairs)
    wb_ref[...] = _pair_expand(w, 1, nv, pairs)
    tga_ref[...] = _pair_targets(par, 0, nv, pairs)
    tgb_ref[...] = _pair_targets(par, 1, nv, pairs)
    _fill_lane_broadcasts(shb_ref, ((1.0 - par) * 16.0).astype(jnp.uint32), nv)
    _fill_lane_broadcasts(wbc_ref, w, nv)
    m_iota = lax.broadcasted_iota(jnp.int32, (8, 16 * pairs), 0).astype(F32)
    nacc = 4

    def body(t):
        row = pl.ds(t, 1)
        acc = [jnp.zeros((8, 128), F32) for _ in range(nacc)]
        for j in range(nv):
            shv = jnp.broadcast_to(shb_ref[j, row, :], (8, 128))
            wv = jnp.broadcast_to(wbc_ref[j, row, :], (8, 128))
            acc[j % nacc] = acc[j % nacc] + wv * _expert_rows(_tile_words(tab_ref, off_ref, t, j), shv)
        tiles = [pltpu.bitcast(_tile_words(tab_ref, off_ref, t, j), BF16) for j in range(nv, PEER_PICKS)]
        vmat = jnp.concatenate([jnp.concatenate(tiles[0::2], axis=0),
                                jnp.concatenate(tiles[1::2], axis=0)], axis=1)
        la = jnp.where(tga_ref[row, :] == m_iota, wa_ref[row, :], 0.0)
        lb = jnp.where(tgb_ref[row, :] == m_iota, wb_ref[row, :], 0.0)
        lf = jnp.concatenate([la, lb], axis=0)
        lh = lf.astype(BF16)
        ll = (lf - lh.astype(F32)).astype(BF16)
        o = jnp.dot(jnp.concatenate([lh, ll], axis=0), vmat, preferred_element_type=F32)
        o_ref[t] = (((o[0:8, 0:128] + o[8:16, 128:256]) + (o[16:24, 0:128] + o[24:32, 128:256]))
                    + ((acc[0] + acc[1]) + (acc[2] + acc[3])))

    _token_loop(tb, body)


def _peer_v(off, par, w, tab):
    T = par.shape[0]
    tb = PEER_TB
    rows, smem, table = _peer_specs(tb, tab)
    pairs = (PEER_PICKS - PEER_V_VALU) // 2
    return pl.pallas_call(
        _peer_v_kernel,
        grid=(T // tb,),
        in_specs=[smem, rows, rows, table],
        out_specs=pl.BlockSpec((tb, 8, 128), lambda i: (i, 0, 0)),
        out_shape=jax.ShapeDtypeStruct((T, 8, 128), F32),
        scratch_shapes=[pltpu.VMEM((tb, 16 * pairs), F32) for _ in range(4)]
        + [pltpu.VMEM((PEER_V_VALU, tb, 128), jnp.uint32), pltpu.VMEM((PEER_V_VALU, tb, 128), F32)],
        compiler_params=_cparams(("parallel",)),
        name="peer_v",
    )(off, par, w, tab)


def _final_kernel(h_ref, p_ref, g_ref, o_ref):
    nt = D_MODEL // 128
    tm = h_ref.shape[0]
    p = jnp.concatenate([p_ref[pl.ds(c, tm, stride=nt), :] for c in range(nt)], axis=1)
    h = h_ref[...] + p
    o_ref[...] = h * lax.rsqrt(jnp.mean(h * h, axis=-1, keepdims=True) + EPS) * g_ref[...]


def _final(h, p, g):
    T = h.shape[0]
    row = pl.BlockSpec((FINAL_TM, D_MODEL), lambda i: (i, 0))
    return pl.pallas_call(
        _final_kernel,
        grid=(T // FINAL_TM,),
        in_specs=[row, pl.BlockSpec((FINAL_TM * (D_MODEL // 128), 128), lambda i: (i, 0)),
                  _const_spec((1, D_MODEL))],
        out_specs=row,
        out_shape=jax.ShapeDtypeStruct((T, D_MODEL), F32),
        compiler_params=_cparams(("parallel",)),
        name="final_norm",
    )(h, p, g)


def _layer(h2, B, S, norm1_g, w_in, w_ret_out, w_att_out, w_out, norm2_g, peer_wq, peer_subkeys, peer_u, peer_v):
    T = B * S
    aq, ak, av = 3072, 3072 + ATT_W, 3072 + 2 * ATT_W
    att_cols = [w_in[:, o + g * ATT_OUT_W:o + (g + 1) * ATT_OUT_W] for g in range(len(ATT_GROUPS)) for o in (aq, ak, av)]
    w_perm = jnp.concatenate([w_in[:, :3072], w_in[:, 7680:9728]] + att_cols, axis=1).astype(BF16)
    main, a0, a1, a2 = _proj(h2, norm1_g.reshape(1, D_MODEL), w_perm, B, S)
    retg = _retention(main.reshape(B, S, MAIN_W)).reshape(T, RET_V_W)
    (o1, l1), (o2, l2), (o3, l3) = (_att_group(a, g) for g, a in enumerate((a0.reshape(B, 1, S, ATT_SEC_W), a1, a2)))
    sk = peer_subkeys.reshape(2 * PEER_HEADS, PEER_NKEYS, PEER_DKEY // 2).astype(BF16)
    h_mid, xn2, scT = _mix(retg, o1, o2, o3, l1, l2, l3, main, h2,
                           w_ret_out.astype(BF16), w_att_out.astype(BF16), w_out.astype(BF16),
                           norm2_g.reshape(1, D_MODEL), peer_wq.astype(BF16), sk)
    w, off, par = _peer_u(scT, xn2.reshape(T, 8, 128), _pack_table(peer_u))
    pout = _peer_v(off, par, w, _pack_table(peer_v))
    return h_mid, pout.reshape(T * 8, 128)


def kernel(x, norm1_g, w_in, w_ret_out, w_att_out, w_out, norm2_g, peer_wq, peer_subkeys, peer_u, peer_v, normf_g):
    B, S, D = x.shape
    assert w_in.shape[0] == 1, "single-layer block"
    h, pout = _layer(x.reshape(B * S, D), B, S, norm1_g[0], w_in[0], w_ret_out[0], w_att_out[0], w_out[0],
                     norm2_g[0], peer_wq[0], peer_subkeys[0], peer_u[0], peer_v[0])
    return _final(h, pout, normf_g.reshape(1, D)).reshape(B, S, D)
```

```python
import functools
import math

import jax
import jax.numpy as jnp
from jax import lax
from jax.experimental import pallas as pl
from jax.experimental.pallas import tpu as pltpu

F32 = jnp.float32
BF16 = jnp.bfloat16

D_MODEL = 1024
RET_HEADS = 8
RET_DK = 64
RET_DV = 128
RET_CHUNK = 128
RET_BATCH = 2
ATT_GROUPS = ((128, 1), (512, 4), (2048, 16))
ATT_HPG = 4
ATT_HEADS = 12
ATT_DH = 128
ATT_BLOCK = 128
ATT_BATCH = (4, 4, 2)
PEER_HEADS = 8
PEER_NKEYS = 128
PEER_DKEY = 256
PEER_TOPK = 16
PEER_PICKS = PEER_HEADS * PEER_TOPK
EPS = 1e-6

RET_QK_W = RET_HEADS * RET_DK
RET_V_W = RET_HEADS * RET_DV
ATT_W = ATT_HEADS * ATT_DH
ATT_OUT_W = ATT_HPG * ATT_DH
OFF_RQ, OFF_RK, OFF_RV, OFF_RG = 0, 512, 1024, 2048
OFF_GR, OFF_GA = 3072, 4096
MAIN_W = 5120
ATT_SEC_W = 3 * ATT_OUT_W
IN_W = MAIN_W + len(ATT_GROUPS) * ATT_SEC_W

VMEM_LIMIT_BYTES = 60 * 1024 * 1024

PROJ_TM = 256
PROJ_TN = 512
MIX_TM = 256
TOPK_TB = 512
PEER_TB = 256
PEER_UNROLL = 32
FINAL_TM = 512


def _cparams(sem):
    return pltpu.CompilerParams(dimension_semantics=sem, vmem_limit_bytes=VMEM_LIMIT_BYTES)


def _const_spec(shape):
    nd = len(shape)
    return pl.BlockSpec(shape, lambda *_: (0,) * nd)


def _proj_kernel(x_ref, g_ref, w_ref, main_ref, a0_ref, a1_ref, a2_ref, scr_ref):
    x = x_ref[...]
    ms = jnp.mean(x * x, axis=-1, keepdims=True)
    xn = (x * lax.rsqrt(ms + EPS) * g_ref[...]).astype(BF16)

    def chunk(j):
        return jnp.dot(xn, w_ref[:, j * PROJ_TN:(j + 1) * PROJ_TN], preferred_element_type=F32)

    nmain = MAIN_W // PROJ_TN
    for j in range(nmain):
        main_ref[:, j * PROJ_TN:(j + 1) * PROJ_TN] = chunk(j).astype(BF16)
    for s in range(3):
        a0_ref[:, s * PROJ_TN:(s + 1) * PROJ_TN] = chunk(nmain + s).astype(BF16)
    for g, a_ref in ((1, a1_ref), (2, a2_ref)):
        d = ATT_GROUPS[g][1]
        for s in range(3):
            res = chunk(nmain + 3 * g + s)
            for c in range(PROJ_TN // 128):
                scr_ref[c] = res[:, c * 128:(c + 1) * 128]
            for r in range(d):
                piece = jnp.concatenate(
                    [scr_ref[c, pl.ds(r, PROJ_TM // d, stride=d), :] for c in range(PROJ_TN // 128)], axis=1)
                a_ref[0, r, :, s * PROJ_TN:(s + 1) * PROJ_TN] = piece.astype(BF16)


def _proj(x2, g, w, B, S):
    T = x2.shape[0]
    tiles = S // PROJ_TM
    d1, d2 = ATT_GROUPS[1][1], ATT_GROUPS[2][1]
    return pl.pallas_call(
        _proj_kernel,
        grid=(T // PROJ_TM,),
        in_specs=[
            pl.BlockSpec((PROJ_TM, D_MODEL), lambda i: (i, 0)),
            _const_spec((1, D_MODEL)),
            pl.BlockSpec((D_MODEL, IN_W), lambda i: (0, 0), pipeline_mode=pl.Buffered(1)),
        ],
        out_specs=[
            pl.BlockSpec((PROJ_TM, MAIN_W), lambda i: (i, 0)),
            pl.BlockSpec((PROJ_TM, ATT_SEC_W), lambda i: (i, 0)),
            pl.BlockSpec((1, d1, PROJ_TM // d1, ATT_SEC_W), lambda i: (i // tiles, 0, i % tiles, 0)),
            pl.BlockSpec((1, d2, PROJ_TM // d2, ATT_SEC_W), lambda i: (i // tiles, 0, i % tiles, 0)),
        ],
        out_shape=[
            jax.ShapeDtypeStruct((T, MAIN_W), BF16),
            jax.ShapeDtypeStruct((T, ATT_SEC_W), BF16),
            jax.ShapeDtypeStruct((B, d1, S // d1, ATT_SEC_W), BF16),
            jax.ShapeDtypeStruct((B, d2, S // d2, ATT_SEC_W), BF16),
        ],
        scratch_shapes=[pltpu.VMEM((PROJ_TN // 128, PROJ_TM, 128), F32)],
        compiler_params=_cparams(("parallel",)),
        name="proj",
    )(x2, g, w)


def _retention_kernel(q_ref, k_ref, v_ref, rg_ref, o_ref, state_ref, decay_ref):
    C = RET_CHUNK

    @pl.when(pl.program_id(1) == 0)
    def _():
        state_ref[...] = jnp.zeros_like(state_ref)
        pi = lax.broadcasted_iota(jnp.int32, (C, C), 0)
        pj = lax.broadcasted_iota(jnp.int32, (C, C), 1)
        diff = (pi - pj).astype(F32)
        for h in range(RET_HEADS):
            lg = math.log1p(-(2.0 ** (-5.0 - h)))
            decay_ref[h] = jnp.where(diff >= 0, jnp.exp(jnp.maximum(diff, 0.0) * lg), 0.0)

    for bb in range(q_ref.shape[0]):
        one = pl.ds(bb, 1)
        _retention_sequence(q_ref.at[one], k_ref.at[one], v_ref.at[one], rg_ref.at[one], o_ref.at[one],
                            state_ref.at[bb], decay_ref)


def _retention_sequence(q_ref, k_ref, v_ref, rg_ref, o_ref, state_ref, decay_ref):
    C = RET_CHUNK
    pos = lax.broadcasted_iota(jnp.int32, (C, 1), 0).astype(F32)
    q = q_ref[0]
    k = k_ref[0]
    v = v_ref[0]
    rg = rg_ref[0]
    for h in range(RET_HEADS):
        lg = math.log1p(-(2.0 ** (-5.0 - h)))
        decay = decay_ref[h]
        w_k = jnp.exp((C - 1 - pos) * lg)
        w_q = jnp.exp((pos + 1.0) * lg)
        qh = q[:, h * RET_DK:(h + 1) * RET_DK]
        kh = k[:, h * RET_DK:(h + 1) * RET_DK].astype(F32) * (RET_DK ** -0.5)
        vh = v[:, h * RET_DV:(h + 1) * RET_DV]
        s = lax.dot_general(qh, kh.astype(BF16), (((1,), (1,)), ((), ())), preferred_element_type=F32)
        p = (s * decay).astype(BF16)
        inner = jnp.dot(p, vh, preferred_element_type=F32)
        st = state_ref[h]
        cross = jnp.dot((qh.astype(F32) * w_q).astype(BF16), st.astype(BF16), preferred_element_type=F32)
        kw = (kh * w_k).astype(BF16)
        kv = lax.dot_general(kw, vh, (((0,), (0,)), ((), ())), preferred_element_type=F32)
        state_ref[h] = math.exp(C * lg) * st + kv
        ret = inner + cross
        rn = ret * lax.rsqrt(jnp.mean(ret * ret, axis=-1, keepdims=True) + EPS)
        g = rg[:, h * RET_DV:(h + 1) * RET_DV].astype(F32)
        o_ref[0, :, h * RET_DV:(h + 1) * RET_DV] = (g * jax.nn.sigmoid(g) * rn).astype(BF16)


def _retention(proj3):
    B, S, _ = proj3.shape
    n = S // RET_CHUNK
    return pl.pallas_call(
        _retention_kernel,
        grid=(B // RET_BATCH, n),
        in_specs=[
            pl.BlockSpec((RET_BATCH, RET_CHUNK, RET_QK_W), lambda b, c: (b, c, OFF_RQ // RET_QK_W)),
            pl.BlockSpec((RET_BATCH, RET_CHUNK, RET_QK_W), lambda b, c: (b, c, OFF_RK // RET_QK_W)),
            pl.BlockSpec((RET_BATCH, RET_CHUNK, RET_V_W), lambda b, c: (b, c, OFF_RV // RET_V_W)),
            pl.BlockSpec((RET_BATCH, RET_CHUNK, RET_V_W), lambda b, c: (b, c, OFF_RG // RET_V_W)),
        ],
        out_specs=pl.BlockSpec((RET_BATCH, RET_CHUNK, RET_V_W), lambda b, c: (b, c, 0)),
        out_shape=jax.ShapeDtypeStruct((B, S, RET_V_W), BF16),
        scratch_shapes=[pltpu.VMEM((RET_BATCH, RET_HEADS, RET_DK, RET_DV), F32),
                        pltpu.VMEM((RET_HEADS, RET_CHUNK, RET_CHUNK), F32)],
        compiler_params=_cparams(("parallel", "arbitrary")),
        name="retention",
    )(proj3, proj3, proj3, proj3)


def _att_kernel(q_ref, kp_ref, kc_ref, vp_ref, vc_ref, o_ref, lse_ref, *, group, dilation, span):
    n = pl.program_id(1)
    r = pl.program_id(2)
    iq = lax.broadcasted_iota(jnp.int32, (ATT_BLOCK, 2 * ATT_BLOCK), 0)
    jk = lax.broadcasted_iota(jnp.int32, (ATT_BLOCK, 2 * ATT_BLOCK), 1)
    dist = iq + ATT_BLOCK - jk
    valid = (dist >= 0) & (dist <= span) & ((jk >= ATT_BLOCK) | (n > 0))
    distf = (dilation * dist).astype(F32)
    rows = pl.ds(r, ATT_BLOCK, stride=dilation) if dilation > 1 else slice(None)
    for bb in range(q_ref.shape[0]):
        q = q_ref[bb, 0]
        kk = jnp.concatenate([kp_ref[bb, 0], kc_ref[bb, 0]], axis=0)
        vv = jnp.concatenate([vp_ref[bb, 0], vc_ref[bb, 0]], axis=0)
        for i in range(ATT_HPG):
            slope = 2.0 ** (-8.0 * (group * ATT_HPG + i + 1) / ATT_HEADS)
            sl = slice(i * ATT_DH, (i + 1) * ATT_DH)
            s = lax.dot_general(q[:, sl], kk[:, sl], (((1,), (1,)), ((), ())), preferred_element_type=F32)
            s = s * (ATT_DH ** -0.5)
            s = jnp.where(valid, s - slope * distf, -jnp.inf)
            m = jnp.max(s, axis=-1, keepdims=True)
            p = jnp.exp(s - m)
            den = jnp.sum(p, axis=-1, keepdims=True)
            o_ref[i, bb, rows, :] = jnp.dot(p.astype(BF16), vv[:, sl], preferred_element_type=F32) / den
            lse_ref[i, bb, rows, :] = jnp.broadcast_to(m + jnp.log(den), (ATT_BLOCK, ATT_DH))


def _att_group(qkv, group):
    window, d = ATT_GROUPS[group]
    B, _, L, _ = qkv.shape
    S = L * d
    nL = L // ATT_BLOCK

    nbat = ATT_BATCH[group]

    def cur(sec):
        return pl.BlockSpec((nbat, 1, ATT_BLOCK, ATT_OUT_W), lambda b, n, r: (b, r, n, sec))

    def prev(sec):
        return pl.BlockSpec((nbat, 1, ATT_BLOCK, ATT_OUT_W), lambda b, n, r: (b, r, jnp.maximum(n - 1, 0), sec))

    out_spec = pl.BlockSpec((ATT_HPG, nbat, ATT_BLOCK * d, ATT_DH), lambda b, n, r: (0, b, n, 0))
    out_shape = jax.ShapeDtypeStruct((ATT_HPG, B, S, ATT_DH), F32)
    o, lse = pl.pallas_call(
        functools.partial(_att_kernel, group=group, dilation=d, span=window // d),
        grid=(B // nbat, nL, d),
        in_specs=[cur(0), prev(1), cur(1), prev(2), cur(2)],
        out_specs=[out_spec, out_spec],
        out_shape=[out_shape, out_shape],
        compiler_params=_cparams(("parallel", "arbitrary", "arbitrary")),
        name=f"att_g{group}",
    )(qkv, qkv, qkv, qkv, qkv)
    return o.reshape(ATT_HPG, B * S, ATT_DH), lse.reshape(ATT_HPG, B * S, ATT_DH)


def _mix_kernel(retg_ref, o1_ref, o2_ref, o3_ref, l1_ref, l2_ref, l3_ref, gr_ref, ga_ref, x_ref,
                wro_ref, wao_ref, wo_ref, g2_ref, wq_ref, sk_ref, h_ref, xn_ref, sc_ref):
    heads = lambda ref: jnp.concatenate([ref[i] for i in range(ATT_HPG)], axis=1)
    l1, l2, l3 = heads(l1_ref), heads(l2_ref), heads(l3_ref)
    mx = jnp.maximum(jnp.maximum(l1, l2), l3)
    e1, e2, e3 = jnp.exp(l1 - mx), jnp.exp(l2 - mx), jnp.exp(l3 - mx)
    att = (e1 * heads(o1_ref) + e2 * heads(o2_ref) + e3 * heads(o3_ref)) / (e1 + e2 + e3)
    a_branch = jnp.dot(att.astype(BF16), wao_ref[...], preferred_element_type=F32)
    r_branch = jnp.dot(retg_ref[...], wro_ref[...], preferred_element_type=F32)
    merged = (jax.nn.sigmoid(gr_ref[...].astype(F32)) * r_branch
              + jax.nn.sigmoid(ga_ref[...].astype(F32)) * a_branch)
    h = x_ref[...] + jnp.dot(merged.astype(BF16), wo_ref[...], preferred_element_type=F32)
    h_ref[...] = h
    xn = h * lax.rsqrt(jnp.mean(h * h, axis=-1, keepdims=True) + EPS) * g2_ref[...]
    for c in range(D_MODEL // 128):
        xn_ref[pl.ds(c, h.shape[0], stride=D_MODEL // 128), :] = xn[:, c * 128:(c + 1) * 128]
    qp = jnp.dot(xn.astype(BF16), wq_ref[...], preferred_element_type=F32).astype(BF16)
    half = PEER_DKEY // 2
    for hp in range(2 * PEER_HEADS):
        sc_ref[hp] = lax.dot_general(sk_ref[hp], qp[:, hp * half:(hp + 1) * half],
                                     (((1,), (1,)), ((), ())), preferred_element_type=F32)


def _mix(retg, o1, o2, o3, l1, l2, l3, proj, x2, wro, wao, wo, g2, wq, sk):
    T = x2.shape[0]
    tm = MIX_TM
    row = lambda w: pl.BlockSpec((tm, w), lambda i: (i, 0))
    hd = pl.BlockSpec((ATT_HPG, tm, ATT_DH), lambda i: (0, i, 0))
    nhp = 2 * PEER_HEADS
    nt = D_MODEL // 128
    return pl.pallas_call(
        _mix_kernel,
        grid=(T // tm,),
        in_specs=[
            row(RET_V_W), hd, hd, hd, hd, hd, hd,
            pl.BlockSpec((tm, D_MODEL), lambda i: (i, OFF_GR // D_MODEL)),
            pl.BlockSpec((tm, D_MODEL), lambda i: (i, OFF_GA // D_MODEL)),
            row(D_MODEL),
            _const_spec(wro.shape), _const_spec(wao.shape), _const_spec(wo.shape),
            _const_spec(g2.shape), _const_spec(wq.shape), _const_spec(sk.shape),
        ],
        out_specs=[row(D_MODEL), pl.BlockSpec((tm * nt, 128), lambda i: (i, 0)),
                   pl.BlockSpec((nhp, PEER_NKEYS, tm), lambda i: (0, 0, i))],
        out_shape=[jax.ShapeDtypeStruct((T, D_MODEL), F32),
                   jax.ShapeDtypeStruct((T * nt, 128), F32),
                   jax.ShapeDtypeStruct((nhp, PEER_NKEYS, T), F32)],
        compiler_params=_cparams(("parallel",)),
        name="mix",
    )(retg, o1, o2, o3, l1, l2, l3, proj, proj, x2, wro, wao, wo, g2, wq, sk)


def _topk_rows(s, ids, k, fill):
    vals, idxs = [], []
    for _ in range(k):
        m = jnp.max(s, axis=0, keepdims=True)
        i = jnp.min(jnp.where(s == m, ids, fill), axis=0, keepdims=True)
        vals.append(m)
        idxs.append(i)
        s = jnp.where(ids == i, -jnp.inf, s)
    return jnp.concatenate(vals, axis=0), jnp.concatenate(idxs, axis=0)


_CAND_B = [PEER_TOPK // (a + 1) for a in range(PEER_TOPK)]
_CAND_ROWS = -(-sum(_CAND_B) // 8) * 8


def _topk_consts(W):
    K = PEER_TOPK
    key_ids = lax.broadcasted_iota(jnp.int32, (PEER_NKEYS, W), 0)
    npad = _CAND_ROWS - sum(_CAND_B)
    cand_ids = jnp.concatenate(
        [jnp.full((1, W), a * K + b, jnp.int32) for a in range(K) for b in range(_CAND_B[a])]
        + [jnp.full((npad, W), K * K, jnp.int32)], axis=0)
    pad = jnp.full((npad, W), -jnp.inf, F32)
    return key_ids, cand_ids, pad


def _topk_head(sc_ref, h, consts):
    K = PEER_TOPK
    key_ids, cand_ids, pad = consts
    v0, i0 = _topk_rows(sc_ref[2 * h], key_ids, K, PEER_NKEYS)
    v1, i1 = _topk_rows(sc_ref[2 * h + 1], key_ids, K, PEER_NKEYS)
    cand = jnp.concatenate([v0[a:a + 1] + v1[0:_CAND_B[a]] for a in range(K)] + [pad], axis=0)
    best_s, best_c = _topk_rows(cand, cand_ids, K, K * K)
    ca = best_c // K
    cb = best_c % K
    ia = jnp.zeros_like(best_c)
    ib = jnp.zeros_like(best_c)
    for a in range(K):
        ia = jnp.where(ca == a, i0[a:a + 1], ia)
        ib = jnp.where(cb == a, i1[a:a + 1], ib)
    ex = jnp.exp(best_s - best_s[0:1])
    return (ia * PEER_NKEYS + ib).astype(F32), ex / jnp.sum(ex, axis=0, keepdims=True)


def _topk_finish(e_scr, g_scr, off_ref, par_ref, gate_ref):
    W = e_scr.shape[-1]
    e = e_scr[...].reshape(PEER_PICKS, W).T.astype(jnp.int32)
    off_ref[...] = lax.shift_right_logical(e, 1) * 8
    par_ref[...] = (e & 1).astype(F32)
    gate_ref[...] = g_scr[...].reshape(PEER_PICKS, W).T


def _topk_kernel(sc_ref, off_ref, par_ref, gate_ref, e_scr, g_scr):
    consts = _topk_consts(sc_ref.shape[-1])

    def head(h, carry):
        e_scr[h], g_scr[h] = _topk_head(sc_ref, h, consts)
        return carry

    lax.fori_loop(0, PEER_HEADS, head, 0)
    _topk_finish(e_scr, g_scr, off_ref, par_ref, gate_ref)


def _topk(scT):
    T = scT.shape[-1]
    tb = min(TOPK_TB, T)
    out_spec = pl.BlockSpec((tb, PEER_PICKS), lambda i: (i, 0))
    return pl.pallas_call(
        _topk_kernel,
        grid=(T // tb,),
        in_specs=[pl.BlockSpec((2 * PEER_HEADS, PEER_NKEYS, tb), lambda i: (0, 0, i))],
        out_specs=[out_spec, out_spec, out_spec],
        out_shape=[jax.ShapeDtypeStruct((T, PEER_PICKS), jnp.int32),
                   jax.ShapeDtypeStruct((T, PEER_PICKS), F32),
                   jax.ShapeDtypeStruct((T, PEER_PICKS), F32)],
        scratch_shapes=[pltpu.VMEM((PEER_HEADS, PEER_TOPK, tb), F32),
                        pltpu.VMEM((PEER_HEADS, PEER_TOPK, tb), F32)],
        compiler_params=_cparams(("parallel",)),
        name="topk",
    )(scT)


PEER_U_VALU = 16
PEER_V_VALU = 16


PACK_TE = 2048


def _pack_kernel(x_ref, o_ref):
    s = pl.program_id(1)
    half = PACK_TE // 2
    bits = lambda v: pltpu.bitcast(v.astype(BF16).astype(F32), jnp.uint32)
    even = bits(x_ref[pl.ds(0, half, stride=2), :])
    odd = bits(x_ref[pl.ds(1, half, stride=2), :])
    o_ref[pl.ds(s, half, stride=D_MODEL // 128), :] = (even >> 16) | (odd & jnp.uint32(0xFFFF0000))


def _pack_table(tab):
    E = tab.shape[0]
    return pl.pallas_call(
        _pack_kernel,
        grid=(E // PACK_TE, D_MODEL // 128),
        in_specs=[pl.BlockSpec((PACK_TE, 128), lambda i, s: (i, s))],
        out_specs=pl.BlockSpec((4 * PACK_TE, 128), lambda i, s: (i, 0)),
        out_shape=jax.ShapeDtypeStruct((4 * E, 128), jnp.uint32),
        compiler_params=_cparams(("parallel", "arbitrary")),
        name="pack_table",
    )(tab)


def _split3(a):
    p0 = a.astype(BF16)
    r1 = a - p0.astype(F32)
    p1 = r1.astype(BF16)
    p2 = (r1 - p1.astype(F32)).astype(BF16)
    return p0, p1, p2


def _pair_expand(v, which, base, pairs, pieces=_split3):
    j = lax.broadcasted_iota(jnp.int32, (PEER_PICKS, 16 * pairs), 0)
    k = lax.broadcasted_iota(jnp.int32, (PEER_PICKS, 16 * pairs), 1)
    onehot = jnp.where(j == base + 2 * (k // 16) + which, 1.0, 0.0).astype(BF16)
    return sum(jnp.dot(p, onehot, preferred_element_type=F32) for p in pieces(v))


def _pair_targets(par, which, base, pairs):
    r = lax.broadcasted_iota(jnp.int32, (par.shape[0], 16 * pairs), 1) % 16
    one_piece = lambda p: (p.astype(BF16),)
    return 0.5 * (r.astype(F32) - _pair_expand(par, which, base, pairs, one_piece))


def _tile_words(tab_ref, off_ref, t, j):
    off = pl.multiple_of(off_ref[t, j], 8)
    return tab_ref[pl.ds(off, 8), :]


def _fill_lane_broadcasts(dst_ref, src, n):
    for j in range(n):
        dst_ref[j] = jnp.broadcast_to(src[:, j:j + 1], dst_ref.shape[1:])


def _expert_rows(words, shv):
    return pltpu.bitcast((words << shv) & jnp.uint32(0xFFFF0000), F32)


def _sublane_sums(ps):
    sub = lax.broadcasted_iota(jnp.int32, (8, 128), 0)
    lo4 = sub < 4
    t = [jnp.where(lo4, ps[j], ps[j + 4]) + pltpu.roll(jnp.where(lo4, ps[j + 4], ps[j]), 4, 0)
         for j in range(4)]
    m2 = (sub & 2) == 0
    u = [jnp.where(m2, t[j] + pltpu.roll(t[j], 6, 0), t[j + 2] + pltpu.roll(t[j + 2], 2, 0))
         for j in range(2)]
    m1 = (sub & 1) == 0
    return jnp.where(m1, u[0] + pltpu.roll(u[0], 7, 0), u[1] + pltpu.roll(u[1], 1, 0))


def _token_loop(tb, body, per_group=None):
    def group(i, carry):
        if per_group is not None:
            per_group(i)
        for s in range(PEER_UNROLL):
            body(i * PEER_UNROLL + s)
        return carry

    lax.fori_loop(0, tb // PEER_UNROLL, group, 0)


def _peer_u_kernel(off0_ref, par0_ref, gate0_ref, x_ref, tab_ref, sc_ref, w_ref, offo_ref, paro_ref,
                   off_ref, sem_ref, offv_ref, par_ref, gate_ref, e_scr, g_scr, tga_ref, tgb_ref, v3_ref, shb_ref,
                   actv_ref):
    tb = x_ref.shape[0]
    nv = PEER_U_VALU
    pairs = (PEER_PICKS - nv) // 2
    nblk = 16 * pairs // 128

    @pl.when(pl.program_id(0) == 0)
    def _():
        offv_ref[...] = off0_ref[...]
        par_ref[...] = par0_ref[...]
        gate_ref[...] = gate0_ref[...]

    def offsets_copy(g):
        rows = pl.ds(g * PEER_UNROLL, PEER_UNROLL)
        return pltpu.make_async_copy(offv_ref.at[rows], off_ref.at[rows], sem_ref.at[g])

    for g in range(tb // PEER_UNROLL):
        offsets_copy(g).start()
    offo_ref[...] = offv_ref[...]
    par = par_ref[...]
    paro_ref[...] = par
    tga_ref[...] = _pair_targets(par, 0, nv, pairs)
    tgb_ref[...] = _pair_targets(par, 1, nv, pairs)
    _fill_lane_broadcasts(shb_ref, ((1.0 - par) * 16.0).astype(jnp.uint32), nv)
    m_iota = lax.broadcasted_iota(jnp.int32, (8, 16 * pairs), 0).astype(F32)
    zero = jnp.zeros((8, 128), F32)
    ones = jnp.ones((8, 128), BF16)
    nt = (((1,), (1,)), ((), ()))

    def body(t):
        xt = x_ref[t]
        row = pl.ds(t, 1)
        rs = []
        for g in range(nv // 8):
            ps = []
            for jj in range(8):
                j = g * 8 + jj
                shv = jnp.broadcast_to(shb_ref[j, row, :], (8, 128))
                ps.append(_expert_rows(_tile_words(tab_ref, off_ref, t, j), shv) * xt)
            rs.append(_sublane_sums(ps))
        r = jnp.concatenate(rs, axis=0)
        hi = r.astype(BF16)
        lo = (r - hi.astype(F32)).astype(BF16)
        av = (lax.dot_general(ones, hi, nt, preferred_element_type=F32)
              + lax.dot_general(ones, lo, nt, preferred_element_type=F32))
        actv_ref[row, 0:nv] = av[0:1]
        tiles = [pltpu.bitcast(_tile_words(tab_ref, off_ref, t, j), BF16) for j in range(nv, PEER_PICKS)]
        wmat = jnp.concatenate(
            [jnp.concatenate([tiles[2 * q], tiles[2 * q + 1]], axis=1) for q in range(pairs)], axis=0)
        xf = jnp.concatenate([jnp.concatenate([xt, zero], axis=1),
                              jnp.concatenate([zero, xt], axis=1)], axis=0)
        xh = xf.astype(BF16)
        xl = (xf - xh.astype(F32)).astype(BF16)
        o = lax.dot_general(jnp.concatenate([xh, xl], axis=0), wmat, nt, preferred_element_type=F32)
        o = o[0:16] + o[16:32]
        za = jnp.where(tga_ref[row, :] == m_iota, o[0:8], 0.0)
        zb = jnp.where(tgb_ref[row, :] == m_iota, o[8:16], 0.0)
        for half, z in enumerate((za, zb)):
            v = _sublane_sums([z[:, i * 128:(i + 1) * 128] for i in range(nblk)] + [zero] * (8 - nblk))
            for i in range(nblk):
                v3_ref[half * nblk + i, row, :] = v[i:i + 1]

    consts = _topk_consts(128)

    def next_block_head(h):
        offsets_copy(h).wait()
        for c in range(tb // 128):
            lanes = slice(c * 128, (c + 1) * 128)
            e_scr[h, :, lanes], g_scr[h, :, lanes] = _topk_head(sc_ref.at[:, :, lanes], h, consts)

    _token_loop(tb, body, next_block_head)
    lane = lax.broadcasted_iota(jnp.int32, (128, PEER_PICKS), 0)
    pick = lax.broadcasted_iota(jnp.int32, (128, PEER_PICKS), 1)
    act = jnp.zeros((tb, PEER_PICKS), F32)
    for c in range(2 * nblk):
        sel = jnp.where(pick == nv + 2 * (8 * (c % nblk) + lane // 16) + c // nblk, 1.0, 0.0).astype(BF16)
        for p in _split3(v3_ref[c]):
            act = act + jnp.dot(p, sel, preferred_element_type=F32)
    act = jnp.where(pick[0:1] < nv, actv_ref[...], act)
    gelu = 0.5 * act * (1.0 + lax.erf(act * (2.0 ** -0.5)))
    w_ref[...] = gate_ref[...] * gelu
    _topk_finish(e_scr, g_scr, offv_ref, par_ref, gate_ref)


def _peer_specs(tb, tab):
    rows = pl.BlockSpec((tb, PEER_PICKS), lambda i: (i, 0))
    smem = pl.BlockSpec((tb, PEER_PICKS), lambda i: (i, 0), memory_space=pltpu.SMEM)
    table = pl.BlockSpec(tab.shape, lambda i: (0, 0), pipeline_mode=pl.Buffered(1))
    return rows, smem, table


def _peer_u(scT, x3, tab):
    T = x3.shape[0]
    tb = PEER_TB
    assert tb // PEER_UNROLL == PEER_HEADS
    nb = T // tb
    rows, _, table = _peer_specs(tb, tab)
    first = pl.BlockSpec((tb, PEER_PICKS), lambda i: (0, 0))
    pairs = (PEER_PICKS - PEER_U_VALU) // 2
    off0, par0, gate0 = _topk(scT[:, :, :tb])
    return pl.pallas_call(
        _peer_u_kernel,
        grid=(nb,),
        in_specs=[first, first, first, pl.BlockSpec((tb, 8, 128), lambda i: (i, 0, 0)), table,
                  pl.BlockSpec((2 * PEER_HEADS, PEER_NKEYS, tb), lambda i: (0, 0, jnp.minimum(i + 1, nb - 1)))],
        out_specs=[rows, rows, rows],
        out_shape=[jax.ShapeDtypeStruct((T, PEER_PICKS), F32),
                   jax.ShapeDtypeStruct((T, PEER_PICKS), jnp.int32),
                   jax.ShapeDtypeStruct((T, PEER_PICKS), F32)],
        scratch_shapes=[pltpu.SMEM((tb, PEER_PICKS), jnp.int32),
                        pltpu.SemaphoreType.DMA((tb // PEER_UNROLL,)),
                        pltpu.VMEM((tb, PEER_PICKS), jnp.int32),
                        pltpu.VMEM((tb, PEER_PICKS), F32), pltpu.VMEM((tb, PEER_PICKS), F32),
                        pltpu.VMEM((PEER_HEADS, PEER_TOPK, tb), F32), pltpu.VMEM((PEER_HEADS, PEER_TOPK, tb), F32),
                        pltpu.VMEM((tb, 16 * pairs), F32), pltpu.VMEM((tb, 16 * pairs), F32),
                        pltpu.VMEM((pairs // 4, tb, 128), F32),
                        pltpu.VMEM((PEER_U_VALU, tb, 128), jnp.uint32),
                        pltpu.VMEM((tb, PEER_PICKS), F32)],
        compiler_params=_cparams(("arbitrary",)),
        name="peer_u",
    )(off0, par0, gate0, x3, tab, scT)


def _peer_v_kernel(off_ref, par_ref, w_ref, tab_ref, o_ref, wa_ref, wb_ref, tga_ref, tgb_ref, shb_ref, wbc_ref):
    tb = o_ref.shape[0]
    nv = PEER_V_VALU
    pairs = (PEER_PICKS - nv) // 2
    par = par_ref[...]
    w = w_ref[...]
    wa_ref[...] = _pair_expand(w, 0, nv, pairs)
    wb_ref[...] = _pair_expand(w, 1, nv, pairs)
    tga_ref[...] = _pair_targets(par, 0, nv, pairs)
    tgb_ref[...] = _pair_targets(par, 1, nv, pairs)
    _fill_lane_broadcasts(shb_ref, ((1.0 - par) * 16.0).astype(jnp.uint32), nv)
    _fill_lane_broadcasts(wbc_ref, w, nv)
    m_iota = lax.broadcasted_iota(jnp.int32, (8, 16 * pairs), 0).astype(F32)
    nacc = 4

    def body(t):
        row = pl.ds(t, 1)
        acc = [jnp.zeros((8, 128), F32) for _ in range(nacc)]
        for j in range(nv):
            shv = jnp.broadcast_to(shb_ref[j, row, :], (8, 128))
            wv = jnp.broadcast_to(wbc_ref[j, row, :], (8, 128))
            acc[j % nacc] = acc[j % nacc] + wv * _expert_rows(_tile_words(tab_ref, off_ref, t, j), shv)
        tiles = [pltpu.bitcast(_tile_words(tab_ref, off_ref, t, j), BF16) for j in range(nv, PEER_PICKS)]
        vmat = jnp.concatenate([jnp.concatenate(tiles[0::2], axis=0),
                                jnp.concatenate(tiles[1::2], axis=0)], axis=1)
        la = jnp.where(tga_ref[row, :] == m_iota, wa_ref[row, :], 0.0)
        lb = jnp.where(tgb_ref[row, :] == m_iota, wb_ref[row, :], 0.0)
        lf = jnp.concatenate([la, lb], axis=0)
        lh = lf.astype(BF16)
        ll = (lf - lh.astype(F32)).astype(BF16)
        o = jnp.dot(jnp.concatenate([lh, ll], axis=0), vmat, preferred_element_type=F32)
        o_ref[t] = (((o[0:8, 0:128] + o[8:16, 128:256]) + (o[16:24, 0:128] + o[24:32, 128:256]))
                    + ((acc[0] + acc[1]) + (acc[2] + acc[3])))

    _token_loop(tb, body)


def _peer_v(off, par, w, tab):
    T = par.shape[0]
    tb = PEER_TB
    rows, smem, table = _peer_specs(tb, tab)
    pairs = (PEER_PICKS - PEER_V_VALU) // 2
    return pl.pallas_call(
        _peer_v_kernel,
        grid=(T // tb,),
        in_specs=[smem, rows, rows, table],
        out_specs=pl.BlockSpec((tb, 8, 128), lambda i: (i, 0, 0)),
        out_shape=jax.ShapeDtypeStruct((T, 8, 128), F32),
        scratch_shapes=[pltpu.VMEM((tb, 16 * pairs), F32) for _ in range(4)]
        + [pltpu.VMEM((PEER_V_VALU, tb, 128), jnp.uint32), pltpu.VMEM((PEER_V_VALU, tb, 128), F32)],
        compiler_params=_cparams(("parallel",)),
        name="peer_v",
    )(off, par, w, tab)


def _final_kernel(h_ref, p_ref, g_ref, o_ref):
    nt = D_MODEL // 128
    tm = h_ref.shape[0]
    p = jnp.concatenate([p_ref[pl.ds(c, tm, stride=nt), :] for c in range(nt)], axis=1)
    h = h_ref[...] + p
    o_ref[...] = h * lax.rsqrt(jnp.mean(h * h, axis=-1, keepdims=True) + EPS) * g_ref[...]


def _final(h, p, g):
    T = h.shape[0]
    row = pl.BlockSpec((FINAL_TM, D_MODEL), lambda i: (i, 0))
    return pl.pallas_call(
        _final_kernel,
        grid=(T // FINAL_TM,),
        in_specs=[row, pl.BlockSpec((FINAL_TM * (D_MODEL // 128), 128), lambda i: (i, 0)),
                  _const_spec((1, D_MODEL))],
        out_specs=row,
        out_shape=jax.ShapeDtypeStruct((T, D_MODEL), F32),
        compiler_params=_cparams(("parallel",)),
        name="final_norm",
    )(h, p, g)


def _layer(h2, B, S, norm1_g, w_in, w_ret_out, w_att_out, w_out, norm2_g, peer_wq, peer_subkeys, peer_u, peer_v):
    T = B * S
    aq, ak, av = 3072, 3072 + ATT_W, 3072 + 2 * ATT_W
    att_cols = [w_in[:, o + g * ATT_OUT_W:o + (g + 1) * ATT_OUT_W] for g in range(len(ATT_GROUPS)) for o in (aq, ak, av)]
    w_perm = jnp.concatenate([w_in[:, :3072], w_in[:, 7680:9728]] + att_cols, axis=1).astype(BF16)
    main, a0, a1, a2 = _proj(h2, norm1_g.reshape(1, D_MODEL), w_perm, B, S)
    retg = _retention(main.reshape(B, S, MAIN_W)).reshape(T, RET_V_W)
    (o1, l1), (o2, l2), (o3, l3) = (_att_group(a, g) for g, a in enumerate((a0.reshape(B, 1, S, ATT_SEC_W), a1, a2)))
    sk = peer_subkeys.reshape(2 * PEER_HEADS, PEER_NKEYS, PEER_DKEY // 2).astype(BF16)
    h_mid, xn2, scT = _mix(retg, o1, o2, o3, l1, l2, l3, main, h2,
                           w_ret_out.astype(BF16), w_att_out.astype(BF16), w_out.astype(BF16),
                           norm2_g.reshape(1, D_MODEL), peer_wq.astype(BF16), sk)
    w, off, par = _peer_u(scT, xn2.reshape(T, 8, 128), _pack_table(peer_u))
    pout = _peer_v(off, par, w, _pack_table(peer_v))
    return h_mid, pout.reshape(T * 8, 128)


def kernel(x, norm1_g, w_in, w_ret_out, w_att_out, w_out, norm2_g, peer_wq, peer_subkeys, peer_u, peer_v, normf_g):
    B, S, D = x.shape
    assert w_in.shape[0] == 1, "single-layer block"
    h, pout = _layer(x.reshape(B * S, D), B, S, norm1_g[0], w_in[0], w_ret_out[0], w_att_out[0], w_out[0],
                     norm2_g[0], peer_wq[0], peer_subkeys[0], peer_u[0], peer_v[0])
    return _final(h, pout, normf_g.reshape(1, D)).reshape(B, S, D)
```
